```python
import math
import jax, jax.numpy as jnp
from jax import lax
import numpy as np

D_MODEL = 1024
BATCH = 4
SEQ = 8192
DEPTH = 4

BLOCK_Q = 128
EPS = 1e-6
MLA_HEADS = 6
MLA_Q_LORA = 384
MLA_KV_LORA = 256
MLA_NOPE = 64
MLA_ROPE = 32
MLA_V = 64
ROPE_THETA = 10000.0
DSA_HEADS = 5
DSA_DIM = 64
IDX_HEADS = 8
IDX_DIM = 64
DSA_TOPK = 256
SB_HEADS = 5
SB_DIM = 64
REL_BUCKETS = 32
REL_MAX_DIST = 128
D_FF = -(-8 * D_MODEL // (3 * 256)) * 256
PLE_DIM = 256

MIX_WIDTH = MLA_HEADS * MLA_V + DSA_HEADS * DSA_DIM + SB_HEADS * SB_DIM
IN_SPLITS = (MLA_Q_LORA, MLA_KV_LORA, MLA_ROPE,
             DSA_HEADS * DSA_DIM, DSA_DIM, DSA_DIM,
             IDX_HEADS * IDX_DIM, IDX_DIM, IDX_HEADS,
             SB_HEADS * SB_DIM, SB_HEADS * SB_DIM, SB_HEADS * SB_DIM)
D_IN = sum(IN_SPLITS)

kernel_name = "hybrid_mla_dsa_stickbreak_trunk"


def rms_norm(x, g):
    xf = x.astype(jnp.float32)
    y = xf * lax.rsqrt(jnp.mean(xf * xf, axis=-1, keepdims=True) + EPS)
    return (y * g.astype(jnp.float32)).astype(x.dtype)


def apply_rope(x, pos):
    half = x.shape[-1] // 2
    inv = ROPE_THETA ** (-jnp.arange(half, dtype=jnp.float32) / half)
    ang = pos.astype(jnp.float32)[:, None] * inv[None, :]
    cos = jnp.cos(ang)[None, :, None, :].astype(x.dtype)
    sin = jnp.sin(ang)[None, :, None, :].astype(x.dtype)
    x1, x2 = x[..., :half], x[..., half:]
    return jnp.concatenate([x1 * cos - x2 * sin, x1 * sin + x2 * cos], axis=-1)


def t5_bucket(dist):
    max_exact = REL_BUCKETS // 2
    d = jnp.maximum(dist, 1).astype(jnp.float32)
    large = max_exact + (jnp.log(d / max_exact) / math.log(REL_MAX_DIST / max_exact)
                         * (REL_BUCKETS - max_exact)).astype(jnp.int32)
    large = jnp.minimum(large, REL_BUCKETS - 1)
    return jnp.where(dist < max_exact, dist, large)


def mixer_block(q0, pos, mla_q, mla_k, mla_v, dsa_q, dsa_k, dsa_v,
                idx_q, idx_k, idx_w, sb_q, sb_k, sb_v, rel_bias, topk):
    B = mla_q.shape[0]
    sl = lambda a: lax.dynamic_slice_in_dim(a, q0, BLOCK_Q, axis=1)
    qpos = q0 + jnp.arange(BLOCK_Q, dtype=jnp.int32)
    causal = pos[None, :] <= qpos[:, None]
    strict = pos[None, :] < qpos[:, None]

    s = jnp.einsum('bqhd,bkhd->bhqk', sl(mla_q), mla_k).astype(jnp.float32) * (MLA_NOPE + MLA_ROPE) ** -0.5
    s = jnp.where(causal[None, None], s, -jnp.inf)
    pr = jax.nn.softmax(s, axis=-1).astype(mla_v.dtype)
    o_mla = jnp.einsum('bhqk,bkhd->bqhd', pr, mla_v).reshape(B, BLOCK_Q, MLA_HEADS * MLA_V)

    ilog = jnp.einsum('bqhd,bkd->bqhk', sl(idx_q), idx_k) * IDX_DIM ** -0.5
    isc = jnp.einsum('bqhk,bqh->bqk', jax.nn.relu(ilog), sl(idx_w)).astype(jnp.float32)
    isc = jnp.where(causal[None], isc, -jnp.inf)
    _, sel = lax.top_k(isc, topk)
    valid = sel <= qpos[None, :, None]
    gather = jax.vmap(lambda a, i: a[i])
    kg = gather(dsa_k, sel)
    vg = gather(dsa_v, sel)
    bias = rel_bias[t5_bucket(jnp.maximum(qpos[None, :, None] - sel, 0))]
    a = (jnp.einsum('bqhd,bqjd->bhqj', sl(dsa_q), kg).astype(jnp.float32) * DSA_DIM ** -0.5
         + jnp.transpose(bias, (0, 3, 1, 2)).astype(jnp.float32))
    a = jnp.where(valid[:, None], a, -jnp.inf)
    pa = jax.nn.softmax(a, axis=-1).astype(vg.dtype)
    o_dsa = jnp.einsum('bhqj,bqjd->bqhd', pa, vg).reshape(B, BLOCK_Q, DSA_HEADS * DSA_DIM)

    z = jnp.einsum('bqhd,bkhd->bhqk', sl(sb_q), sb_k).astype(jnp.float32) * SB_DIM ** -0.5
    m = strict[None, None]
    l1m = jnp.where(m, jax.nn.log_sigmoid(-z), 0.0)
    suffix = lax.cumsum(l1m, axis=3, reverse=True) - l1m
    w = jnp.where(m, jnp.exp(jax.nn.log_sigmoid(z) + suffix), 0.0).astype(sb_v.dtype)
    o_sb = jnp.einsum('bhqk,bkhd->bqhd', w, sb_v).reshape(B, BLOCK_Q, SB_HEADS * SB_DIM)

    return jnp.concatenate([o_mla, o_dsa, o_sb], axis=-1)


def setup_inputs(seed: int = 0) -> dict:
    key = jax.random.key(seed)
    ks = jax.random.split(key, 18)
    f32 = jnp.float32
    nrm = lambda k, shape, scale: jax.random.normal(k, shape, f32) * scale
    gain = lambda k, shape: 1.0 + 0.05 * jax.random.normal(k, shape, f32)
    return {
        "x": nrm(ks[0], (BATCH, SEQ, D_MODEL), 1.0),
        "p": nrm(ks[1], (DEPTH, BATCH, SEQ, PLE_DIM), 1.0),
        "w_in": nrm(ks[2], (DEPTH, D_MODEL, D_IN), D_MODEL ** -0.5),
        "attn_norm": gain(ks[3], (DEPTH, D_MODEL)),
        "mla_q_norm": gain(ks[4], (DEPTH, MLA_Q_LORA)),
        "mla_w_uq": nrm(ks[5], (DEPTH, MLA_Q_LORA, MLA_HEADS * (MLA_NOPE + MLA_ROPE)), MLA_Q_LORA ** -0.5),
        "mla_kv_norm": gain(ks[6], (DEPTH, MLA_KV_LORA)),
        "mla_w_ukv": nrm(ks[7], (DEPTH, MLA_KV_LORA, MLA_HEADS * (MLA_NOPE + MLA_V)), MLA_KV_LORA ** -0.5),
        "rel_bias": nrm(ks[8], (REL_BUCKETS, DSA_HEADS), 0.5),
        "w_o": nrm(ks[9], (DEPTH, MIX_WIDTH, D_MODEL), MIX_WIDTH ** -0.5),
        "ffn_norm": gain(ks[10], (DEPTH, D_MODEL)),
        "w_gate": nrm(ks[11], (DEPTH, D_MODEL, D_FF), D_MODEL ** -0.5),
        "w_up": nrm(ks[12], (DEPTH, D_MODEL, D_FF), D_MODEL ** -0.5),
        "w_down": nrm(ks[13], (DEPTH, D_FF, D_MODEL), D_FF ** -0.5),
        "ple_norm": gain(ks[14], (DEPTH, D_MODEL)),
        "w_ple_gate": nrm(ks[15], (DEPTH, D_MODEL, D_MODEL), D_MODEL ** -0.5),
        "w_ple_proj": nrm(ks[16], (DEPTH, PLE_DIM, D_MODEL), PLE_DIM ** -0.5),
        "final_norm": gain(ks[17], (D_MODEL,)),
    }


def reference(x, p, w_in, attn_norm, mla_q_norm, mla_w_uq, mla_kv_norm, mla_w_ukv,
              rel_bias, w_o, ffn_norm, w_gate, w_up, w_down, ple_norm, w_ple_gate,
              w_ple_proj, final_norm):
    B, S, _ = x.shape
    n_blk = S // BLOCK_Q
    topk = min(DSA_TOPK, S // 4)
    pos = jnp.arange(S, dtype=jnp.int32)
    offsets = [int(o) for o in np.cumsum(IN_SPLITS)[:-1]]
    starts = jnp.arange(n_blk, dtype=jnp.int32) * BLOCK_Q
    h = x
    for i in range(DEPTH):
        hn = rms_norm(h, attn_norm[i])
        proj = hn @ w_in[i]
        (c_q, c_kv, k_rope, dq, dk, dv, iq, ik, iw, sq, sk, sv) = jnp.split(proj, offsets, axis=-1)

        q = (rms_norm(c_q, mla_q_norm[i]) @ mla_w_uq[i]).reshape(B, S, MLA_HEADS, MLA_NOPE + MLA_ROPE)
        q = jnp.concatenate([q[..., :MLA_NOPE], apply_rope(q[..., MLA_NOPE:], pos)], axis=-1)
        kv = (rms_norm(c_kv, mla_kv_norm[i]) @ mla_w_ukv[i]).reshape(B, S, MLA_HEADS, MLA_NOPE + MLA_V)
        kr = apply_rope(k_rope[:, :, None, :], pos)
        mla_k = jnp.concatenate([kv[..., :MLA_NOPE],
                                 jnp.broadcast_to(kr, (B, S, MLA_HEADS, MLA_ROPE))], axis=-1)
        mla_v = kv[..., MLA_NOPE:]

        dsa_q = dq.reshape(B, S, DSA_HEADS, DSA_DIM)
        idx_q = iq.reshape(B, S, IDX_HEADS, IDX_DIM)
        idx_w = iw * IDX_HEADS ** -0.5

        sb_q = sq.reshape(B, S, SB_HEADS, SB_DIM)
        sb_k = sk.reshape(B, S, SB_HEADS, SB_DIM)
        sb_v = sv.reshape(B, S, SB_HEADS, SB_DIM)

        blocks = lax.map(lambda q0: mixer_block(q0, pos, q, mla_k, mla_v, dsa_q, dk, dv,
                                                idx_q, ik, idx_w, sb_q, sb_k, sb_v,
                                                rel_bias, topk), starts)
        mix = jnp.transpose(blocks, (1, 0, 2, 3)).reshape(B, S, MIX_WIDTH)
        h = h + mix @ w_o[i]

        hf = rms_norm(h, ffn_norm[i])
        h = h + (jax.nn.silu(hf @ w_gate[i]) * (hf @ w_up[i])) @ w_down[i]

        g = jax.nn.sigmoid(rms_norm(h, ple_norm[i]) @ w_ple_gate[i])
        h = h + g * (p[i] @ w_ple_proj[i])
    return rms_norm(h, final_norm)
```

```python
import functools
import math

import jax
import jax.numpy as jnp
from jax import lax
from jax.experimental import pallas as pl
from jax.experimental.pallas import tpu as pltpu

EPS = 1e-6
MLA_HEADS = 6
MLA_Q_LORA = 384
MLA_KV_LORA = 256
MLA_NOPE = 64
MLA_ROPE = 32
MLA_V = 64
ROPE_THETA = 10000.0
DSA_HEADS = 5
DSA_DIM = 64
IDX_HEADS = 8
IDX_DIM = 64
DSA_TOPK = 256
SB_HEADS = 5
SB_DIM = 64
REL_BUCKETS = 32
REL_MAX_DIST = 128

LANES = 128
HEAD = 64
VMEM_LIMIT = 56 * 1024 * 1024

TM = 512
MLA_TQ, MLA_TK = 256, 512
SB_T = 256
DSA_TQ, DSA_TK = 128, 512
DSA_SUB = DSA_TK // DSA_TQ
FF_CHUNK = 1408

NEG = -1e30
INT_MIN = -2 ** 31
NEG_INF_KEY = -2139095041

F32 = jnp.float32
BF16 = jnp.bfloat16

SEG_CQ = (0, 384)
SEG_CKV = (384, 640)
SEG_KRA = (640, 768)
SEG_KRB = (768, 896)
SEG_DQ_IK = (896, 1280)
SEG_IQ = (1280, 1792)
SEG_SQ_DK = (1792, 2176)
SEG_SK_DV = (2176, 2560)
SEG_SV_IW = (2560, 2944)
W_IN_COLS = 2944


def _rms(x, g):
    return x * lax.rsqrt(jnp.mean(x * x, axis=-1, keepdims=True) + EPS) * g


def _dot(a, b):
    return jnp.dot(a, b, preferred_element_type=F32)


def _dot_nt(a, b):
    return lax.dot_general(a, b, (((1,), (1,)), ((), ())), preferred_element_type=F32)


def _const_spec(shape):
    n = len(shape)
    return pl.BlockSpec(shape, lambda *_: (0,) * n, pipeline_mode=pl.Buffered(1))


def _params(*sem):
    return pltpu.CompilerParams(dimension_semantics=sem, vmem_limit_bytes=VMEM_LIMIT)


def _proj_kernel(h_ref, g_ref, w_ref, gq_ref, wqa_ref, wqb_ref, gkv_ref, wk_ref, wv_ref,
                 cos_ref, sin_ref,
                 mq_ref, mk_ref, mv_ref, dq_ref, dk_ref, dv_ref, iq_ref, ik_ref, iw_ref,
                 sq_ref, sk_ref, sv_ref):
    hn = _rms(h_ref[0], g_ref[...]).astype(BF16)

    def seg(ab):
        return _dot(hn, w_ref[:, ab[0]:ab[1]])

    cos = cos_ref[...]
    sin = sin_ref[...]

    cqn = _rms(seg(SEG_CQ), gq_ref[...]).astype(BF16)
    qa = _dot(cqn, wqa_ref[...])
    qb = _dot(cqn, wqb_ref[...])
    for h in range(MLA_HEADS):
        sl = slice(h * LANES, (h + 1) * LANES)
        mq_ref[0, h] = (qa[:, sl] * cos + qb[:, sl] * sin).astype(BF16)

    ckvn = _rms(seg(SEG_CKV), gkv_ref[...]).astype(BF16)
    kn = _dot(ckvn, wk_ref[...])
    kr = seg(SEG_KRA) * cos + seg(SEG_KRB) * sin
    for h in range(MLA_HEADS):
        mk_ref[0, h] = (kn[:, h * LANES:(h + 1) * LANES] + kr).astype(BF16)
    vv = _dot(ckvn, wv_ref[...])
    for h in range(MLA_HEADS):
        mv_ref[0, h] = vv[:, h * HEAD:(h + 1) * HEAD].astype(BF16)

    a = seg(SEG_DQ_IK)
    for h in range(DSA_HEADS):
        dq_ref[0, h] = a[:, h * HEAD:(h + 1) * HEAD].astype(BF16)
    ik_ref[0] = a[:, 320:384].astype(BF16)

    a = seg(SEG_IQ)
    for h in range(IDX_HEADS):
        iq_ref[0, h] = a[:, h * HEAD:(h + 1) * HEAD].astype(BF16)

    a = seg(SEG_SQ_DK)
    for h in range(SB_HEADS):
        sq_ref[0, h] = a[:, h * HEAD:(h + 1) * HEAD].astype(BF16)
    dk_ref[0] = a[:, 320:384].astype(BF16)

    a = seg(SEG_SK_DV)
    for h in range(SB_HEADS):
        sk_ref[0, h] = a[:, h * HEAD:(h + 1) * HEAD].astype(BF16)
    dv_ref[0] = a[:, 320:384].astype(BF16)

    a = seg(SEG_SV_IW)
    for h in range(SB_HEADS):
        sv_ref[0, h] = a[:, h * HEAD:(h + 1) * HEAD].astype(BF16)
    iw_ref[0] = a[:, 320:320 + IDX_HEADS] * IDX_HEADS ** -0.5


def _proj_call(h, g, w, gq, wqa, wqb, gkv, wk, wv, cos_t, sin_t):
    B, S, D = h.shape
    nt = S // TM
    tok = lambda w_: pl.BlockSpec((1, TM, w_), lambda b, t: (b, t, 0))
    heads = lambda n, w_: pl.BlockSpec((1, n, TM, w_), lambda b, t: (b, 0, t, 0))
    tab = pl.BlockSpec((TM, LANES), lambda b, t: (t, 0))
    hs = lambda n, w_: jax.ShapeDtypeStruct((B, n, S, w_), BF16)
    ts = lambda w_, dt: jax.ShapeDtypeStruct((B, S, w_), dt)
    return pl.pallas_call(
        _proj_kernel,
        grid=(B, nt),
        in_specs=[tok(D), _const_spec(g.shape), _const_spec(w.shape), _const_spec(gq.shape),
                  _const_spec(wqa.shape), _const_spec(wqb.shape), _const_spec(gkv.shape),
                  _const_spec(wk.shape), _const_spec(wv.shape), tab, tab],
        out_specs=[heads(MLA_HEADS, LANES), heads(MLA_HEADS, LANES), heads(MLA_HEADS, HEAD),
                   heads(DSA_HEADS, HEAD), tok(HEAD), tok(HEAD),
                   heads(IDX_HEADS, HEAD), tok(HEAD), tok(IDX_HEADS),
                   heads(SB_HEADS, HEAD), heads(SB_HEADS, HEAD), heads(SB_HEADS, HEAD)],
        out_shape=[hs(MLA_HEADS, LANES), hs(MLA_HEADS, LANES), hs(MLA_HEADS, HEAD),
                   hs(DSA_HEADS, HEAD), ts(HEAD, BF16), ts(HEAD, BF16),
                   hs(IDX_HEADS, HEAD), ts(HEAD, BF16), ts(IDX_HEADS, F32),
                   hs(SB_HEADS, HEAD), hs(SB_HEADS, HEAD), hs(SB_HEADS, HEAD)],
        compiler_params=_params("parallel", "parallel"),
        name="proj",
    )(h, g, w, gq, wqa, wqb, gkv, wk, wv, cos_t, sin_t)


def _mla_kernel(q_ref, k_ref, v_ref, o_ref, m_ref, l_ref, acc_ref):
    qi = pl.program_id(2)
    q = q_ref[0, 0]
    m_ref[...] = jnp.full(m_ref.shape, NEG, F32)
    l_ref[...] = jnp.zeros(l_ref.shape, F32)
    acc_ref[...] = jnp.zeros(acc_ref.shape, F32)
    scale2 = (MLA_NOPE + MLA_ROPE) ** -0.5 * math.log2(math.e)

    def step(j, masked):
        k0 = pl.multiple_of(j * MLA_TK, MLA_TK)
        s = _dot_nt(q, k_ref[0, 0, pl.ds(k0, MLA_TK), :]) * scale2
        if masked:
            kpos = k0 + lax.broadcasted_iota(jnp.int32, s.shape, 1)
            qpos = qi * MLA_TQ + lax.broadcasted_iota(jnp.int32, s.shape, 0)
            s = jnp.where(kpos <= qpos, s, NEG)
        m_prev = m_ref[...]
        m_new = jnp.maximum(m_prev, jnp.max(s, axis=1, keepdims=True))
        p = jnp.exp2(s - m_new)
        alpha = jnp.exp2(m_prev - m_new)
        l_ref[...] = alpha * l_ref[...] + jnp.sum(p, axis=1, keepdims=True)
        acc_ref[...] = alpha * acc_ref[...] + _dot(p.astype(BF16), v_ref[0, 0, pl.ds(k0, MLA_TK), :])
        m_ref[...] = m_new

    n_full = (qi * MLA_TQ) // MLA_TK
    n_all = ((qi + 1) * MLA_TQ + MLA_TK - 1) // MLA_TK

    def full_body(j, c):
        step(j, False)
        return c

    def diag_body(j, c):
        step(j, True)
        return c

    lax.fori_loop(0, n_full, full_body, 0)
    lax.fori_loop(n_full, n_all, diag_body, 0)
    o_ref[0, 0] = (acc_ref[...] / l_ref[...]).astype(BF16)


def _mla_call(q, k, v):
    B, H, S, _ = q.shape
    return pl.pallas_call(
        _mla_kernel,
        grid=(B, H, S // MLA_TQ),
        in_specs=[pl.BlockSpec((1, 1, MLA_TQ, LANES), lambda b, h, i: (b, h, i, 0)),
                  pl.BlockSpec((1, 1, S, LANES), lambda b, h, i: (b, h, 0, 0)),
                  pl.BlockSpec((1, 1, S, HEAD), lambda b, h, i: (b, h, 0, 0))],
        out_specs=pl.BlockSpec((1, 1, MLA_TQ, HEAD), lambda b, h, i: (b, h, i, 0)),
        out_shape=jax.ShapeDtypeStruct((B, H, S, HEAD), BF16),
        scratch_shapes=[pltpu.VMEM((MLA_TQ, 1), F32), pltpu.VMEM((MLA_TQ, 1), F32),
                        pltpu.VMEM((MLA_TQ, HEAD), F32)],
        compiler_params=_params("parallel", "parallel", "arbitrary"),
        name="mla",
    )(q, k, v)


def _sb_kernel(q_ref, k_ref, v_ref, u_ref, o_ref, c_ref, acc_ref):
    qi = pl.program_id(2)
    q = q_ref[0, 0]
    c_ref[...] = jnp.zeros(c_ref.shape, F32)
    acc_ref[...] = jnp.zeros(acc_ref.shape, F32)

    def step(j, masked):
        k0 = pl.multiple_of(j * SB_T, SB_T)
        z = _dot_nt(q, k_ref[0, 0, pl.ds(k0, SB_T), :])
        lsz = jnp.minimum(z, 0.0) - jnp.log(1.0 + jnp.exp(-jnp.abs(z)))
        l1m = lsz - z
        if masked:
            strict = (lax.broadcasted_iota(jnp.int32, z.shape, 1)
                      < lax.broadcasted_iota(jnp.int32, z.shape, 0))
            l1m = jnp.where(strict, l1m, 0.0)
        hi = l1m.astype(BF16)
        lo = (l1m - hi.astype(F32)).astype(BF16)
        suffix = _dot(hi, u_ref[...]) + _dot(lo, u_ref[...]) + c_ref[...]
        w = jnp.exp(lsz + suffix)
        if masked:
            w = jnp.where(strict, w, 0.0)
        acc_ref[...] += _dot(w.astype(BF16), v_ref[0, 0, pl.ds(k0, SB_T), :])
        c_ref[...] += jnp.sum(l1m, axis=1, keepdims=True)

    step(qi, True)

    def body(i, c):
        step(qi - 1 - i, False)
        return c

    lax.fori_loop(0, qi, body, 0)
    o_ref[0, 0] = acc_ref[...].astype(BF16)


def _sb_call(q, k, v, u):
    B, H, S, _ = q.shape
    return pl.pallas_call(
        _sb_kernel,
        grid=(B, H, S // SB_T),
        in_specs=[pl.BlockSpec((1, 1, SB_T, HEAD), lambda b, h, i: (b, h, i, 0)),
                  pl.BlockSpec((1, 1, S, HEAD), lambda b, h, i: (b, h, 0, 0)),
                  pl.BlockSpec((1, 1, S, HEAD), lambda b, h, i: (b, h, 0, 0)),
                  _const_spec(u.shape)],
        out_specs=pl.BlockSpec((1, 1, SB_T, HEAD), lambda b, h, i: (b, h, i, 0)),
        out_shape=jax.ShapeDtypeStruct((B, H, S, HEAD), BF16),
        scratch_shapes=[pltpu.VMEM((SB_T, 1), F32), pltpu.VMEM((SB_T, HEAD), F32)],
        compiler_params=_params("parallel", "parallel", "arbitrary"),
        name="stickbreak",
    )(q, k, v, u)


def _bias_kernel(rb_ref, o_ref):
    v = pl.program_id(0)
    shape = (DSA_TQ, DSA_TK)
    off = jnp.where(v < DSA_SUB, v * DSA_TQ, DSA_TK)
    dist = off + lax.broadcasted_iota(jnp.int32, shape, 0) - lax.broadcasted_iota(jnp.int32, shape, 1)
    max_exact = REL_BUCKETS // 2
    d = jnp.maximum(dist, 1).astype(F32)
    large = max_exact + (jnp.log(d / max_exact) / math.log(REL_MAX_DIST / max_exact)
                         * (REL_BUCKETS - max_exact)).astype(jnp.int32)
    large = jnp.minimum(large, REL_BUCKETS - 1)
    bucket = jnp.where(dist < max_exact, dist, large)
    for h in range(DSA_HEADS):
        t = jnp.zeros(shape, F32)
        for b in range(REL_BUCKETS):
            t = jnp.where(bucket == b, rb_ref[b, h], t)
        o_ref[0, h] = jnp.where(dist >= 0, t, NEG)


def _bias_call(rel_bias):
    nv = DSA_SUB + 1
    return pl.pallas_call(
        _bias_kernel,
        grid=(nv,),
        in_specs=[pl.BlockSpec(memory_space=pltpu.SMEM)],
        out_specs=pl.BlockSpec((1, DSA_HEADS, DSA_TQ, DSA_TK), lambda v: (v, 0, 0, 0)),
        out_shape=jax.ShapeDtypeStruct((nv, DSA_HEADS, DSA_TQ, DSA_TK), F32),
        compiler_params=_params("arbitrary"),
        name="bias_tiles",
    )(rel_bias)


def _dsa_kernel(far_ref, iq_ref, ik_ref, iw_ref, dq_ref, dk_ref, dv_ref, tab_ref, o_ref,
                keys_ref, m_ref, l_ref, acc_ref, *, topk):
    qi = pl.program_id(1)
    n_blk = qi // DSA_SUB + 1
    sub = qi % DSA_SUB
    tq, tk = DSA_TQ, DSA_TK

    iq = iq_ref[0].reshape(IDX_HEADS * tq, IDX_DIM)
    iw = iw_ref[0]
    wcol = [iw[:, h:h + 1] for h in range(IDX_HEADS)]

    def idx_step(j, masked):
        k0 = pl.multiple_of(j * tk, tk)
        x = _dot_nt(iq, ik_ref[0, pl.ds(k0, tk), :])
        isc = jnp.maximum(x[0:tq], 0.0) * wcol[0]
        for h in range(1, IDX_HEADS):
            isc = isc + jnp.maximum(x[h * tq:(h + 1) * tq], 0.0) * wcol[h]
        isc = jnp.where(isc == 0.0, 0.0, isc)
        if masked:
            kpos = k0 + lax.broadcasted_iota(jnp.int32, isc.shape, 1)
            qpos = qi * tq + lax.broadcasted_iota(jnp.int32, isc.shape, 0)
            isc = jnp.where(kpos <= qpos, isc, -jnp.inf)
        bits = pltpu.bitcast(isc, jnp.int32)
        keys_ref[j] = bits ^ (lax.shift_right_arithmetic(bits, 31) & 0x7FFFFFFF)

    def idx_body(j, c):
        idx_step(j, False)
        return c

    lax.fori_loop(0, n_blk - 1, idx_body, 0)
    idx_step(n_blk - 1, True)

    def count_ge(cand):
        def body(j, acc):
            c = jnp.where(keys_ref[j] >= cand, 1.0, 0.0)
            for t in range(tk // LANES):
                acc = acc + c[:, t * LANES:(t + 1) * LANES]
            return acc
        acc = lax.fori_loop(0, n_blk, body, jnp.zeros((tq, LANES), F32))
        return jnp.sum(acc, axis=1, keepdims=True)

    def bit_body(i, t_u):
        cand_u = t_u | lax.shift_left(jnp.int32(1), 31 - i)
        cnt = count_ge(cand_u ^ INT_MIN)
        return jnp.where(cnt >= topk, cand_u, t_u)

    t_u = lax.fori_loop(0, 32, bit_body, jnp.zeros((tq, 1), jnp.int32))
    thr = t_u ^ INT_MIN

    cnt_ge = count_ge(thr)
    tie = jnp.where(thr > NEG_INF_KEY, jnp.where(cnt_ge > topk, 1.0, 0.0), 0.0)

    @pl.when(jnp.max(tie) > 0.0)
    def _():
        need = jnp.where(tie > 0.0, topk - count_ge(thr + 1), float(tk * keys_ref.shape[0] + 1))
        row = lax.broadcasted_iota(jnp.int32, (tk, tk), 0)
        col = lax.broadcasted_iota(jnp.int32, (tk, tk), 1)
        before = jnp.where(row < col, 1.0, 0.0).astype(BF16)

        def body(j, run):
            kb = keys_ref[j]
            eq = jnp.where(kb == thr, 1.0, 0.0)
            rank = _dot(eq.astype(BF16), before) + run
            keys_ref[j] = jnp.where(kb == thr, jnp.where(rank >= need, thr - 1, kb), kb)
            return run + jnp.sum(eq, axis=1, keepdims=True)

        lax.fori_loop(0, n_blk, body, jnp.zeros((tq, 1), F32))

    dq = dq_ref[0].reshape(DSA_HEADS * tq, DSA_DIM)
    m_ref[...] = jnp.full(m_ref.shape, NEG, F32)
    l_ref[...] = jnp.zeros(l_ref.shape, F32)
    acc_ref[...] = jnp.zeros(acc_ref.shape, F32)

    def att_step(j, variant):
        k0 = pl.multiple_of(j * tk, tk)
        s = _dot_nt(dq, dk_ref[0, pl.ds(k0, tk), :])
        vb = dv_ref[0, pl.ds(k0, tk), :]
        sel = keys_ref[j] >= thr
        for h in range(DSA_HEADS):
            bias = far_ref[h] if variant is None else tab_ref[variant, h]
            a = jnp.where(sel, s[h * tq:(h + 1) * tq] + bias, NEG)
            m_prev = m_ref[h]
            m_new = jnp.maximum(m_prev, jnp.max(a, axis=1, keepdims=True))
            p = jnp.exp(a - m_new)
            alpha = jnp.exp(m_prev - m_new)
            l_ref[h] = alpha * l_ref[h] + jnp.sum(p, axis=1, keepdims=True)
            acc_ref[h] = alpha * acc_ref[h] + _dot(p.astype(BF16), vb)
            m_ref[h] = m_new

    def far_body(j, c):
        att_step(j, None)
        return c

    lax.fori_loop(0, n_blk - 2, far_body, 0)

    @pl.when(jnp.logical_and(n_blk >= 2, sub == 0))
    def _():
        att_step(n_blk - 2, DSA_SUB)

    @pl.when(jnp.logical_and(n_blk >= 2, sub != 0))
    def _():
        att_step(n_blk - 2, None)

    att_step(n_blk - 1, sub)
    for h in range(DSA_HEADS):
        o_ref[0, h] = (acc_ref[h] / l_ref[h]).astype(BF16)


def _dsa_call(far, iq, ik, iw, dq, dk, dv, tab, topk):
    B, _, S, _ = iq.shape
    tq = DSA_TQ
    qh = lambda n: pl.BlockSpec((1, n, tq, HEAD), lambda b, i: (b, 0, i, 0))
    kv = pl.BlockSpec((1, S, HEAD), lambda b, i: (b, 0, 0))
    return pl.pallas_call(
        functools.partial(_dsa_kernel, topk=topk),
        grid=(B, S // tq),
        in_specs=[pl.BlockSpec(memory_space=pltpu.SMEM),
                  qh(IDX_HEADS), kv, pl.BlockSpec((1, tq, IDX_HEADS), lambda b, i: (b, i, 0)),
                  qh(DSA_HEADS), kv, kv, _const_spec(tab.shape)],
        out_specs=qh(DSA_HEADS),
        out_shape=jax.ShapeDtypeStruct((B, DSA_HEADS, S, HEAD), BF16),
        scratch_shapes=[pltpu.VMEM((S // DSA_TK, tq, DSA_TK), jnp.int32),
                        pltpu.VMEM((DSA_HEADS, tq, 1), F32), pltpu.VMEM((DSA_HEADS, tq, 1), F32),
                        pltpu.VMEM((DSA_HEADS, tq, HEAD), F32)],
        compiler_params=_params("parallel", "arbitrary"),
        name="dsa",
    )(far, iq, ik, iw, dq, dk, dv, tab)


def _post_kernel(h_ref, om_ref, od_ref, os_ref, p_ref, wo_ref, gf_ref, wg_ref, wu_ref, wd_ref,
                 gp_ref, wpg_ref, wpp_ref, gfin_ref, out_ref, *, final):
    pieces = ([om_ref[0, i] for i in range(MLA_HEADS)] + [od_ref[0, i] for i in range(DSA_HEADS)]
              + [os_ref[0, i] for i in range(SB_HEADS)])
    mix = jnp.concatenate(pieces, axis=-1)
    h = h_ref[0] + _dot(mix, wo_ref[...])

    hf = _rms(h, gf_ref[...]).astype(BF16)
    d_ff = wg_ref.shape[1]
    ffn = jnp.zeros(h.shape, F32)
    for c0 in range(0, d_ff, FF_CHUNK):
        g = _dot(hf, wg_ref[:, c0:c0 + FF_CHUNK])
        u = _dot(hf, wu_ref[:, c0:c0 + FF_CHUNK])
        ffn = ffn + _dot((g * jax.nn.sigmoid(g) * u).astype(BF16), wd_ref[c0:c0 + FF_CHUNK, :])
    h = h + ffn

    gate = jax.nn.sigmoid(_dot(_rms(h, gp_ref[...]).astype(BF16), wpg_ref[...]))
    h = h + gate * _dot(p_ref[0, 0].astype(BF16), wpp_ref[...])
    if final:
        h = _rms(h, gfin_ref[...])
    out_ref[0] = h


def _post_call(h, om, od, os_, p, layer, wo, gf, wg, wu, wd, gp, wpg, wpp, gfin, final):
    B, S, D = h.shape
    heads = lambda n: pl.BlockSpec((1, n, TM, HEAD), lambda b, t: (b, 0, t, 0))
    tok = pl.BlockSpec((1, TM, D), lambda b, t: (b, t, 0))
    return pl.pallas_call(
        functools.partial(_post_kernel, final=final),
        grid=(B, S // TM),
        in_specs=[tok, heads(MLA_HEADS), heads(DSA_HEADS), heads(SB_HEADS),
                  pl.BlockSpec((1, 1, TM, p.shape[-1]), lambda b, t: (layer, b, t, 0)),
                  _const_spec(wo.shape), _const_spec(gf.shape), _const_spec(wg.shape),
                  _const_spec(wu.shape), _const_spec(wd.shape), _const_spec(gp.shape),
                  _const_spec(wpg.shape), _const_spec(wpp.shape), _const_spec(gfin.shape)],
        out_specs=tok,
        out_shape=jax.ShapeDtypeStruct((B, S, D), F32),
        compiler_params=_params("parallel", "parallel"),
        name="post",
    )(h, om, od, os_, p, wo, gf, wg, wu, wd, gp, wpg, wpp, gfin)


def _swap_halves(w):
    half = w.shape[-1] // 2
    return jnp.concatenate([w[..., half:], w[..., :half]], axis=-1)


def _pack_w_in(w_in):
    L, D, _ = w_in.shape
    z = lambda n: jnp.zeros((L, D, n), F32)
    o = 0
    cols = {}
    for name, n in (("cq", MLA_Q_LORA), ("ckv", MLA_KV_LORA), ("kr", MLA_ROPE),
                    ("dq", DSA_HEADS * DSA_DIM), ("dk", DSA_DIM), ("dv", DSA_DIM),
                    ("iq", IDX_HEADS * IDX_DIM), ("ik", IDX_DIM), ("iw", IDX_HEADS),
                    ("sq", SB_HEADS * SB_DIM), ("sk", SB_HEADS * SB_DIM), ("sv", SB_HEADS * SB_DIM)):
        cols[name] = w_in[:, :, o:o + n]
        o += n
    pad_r = LANES - MLA_NOPE - MLA_ROPE
    packed = jnp.concatenate([
        cols["cq"], cols["ckv"],
        z(MLA_NOPE), cols["kr"], z(pad_r),
        z(MLA_NOPE), _swap_halves(cols["kr"]), z(pad_r),
        cols["dq"] * DSA_DIM ** -0.5, cols["ik"],
        cols["iq"] * IDX_DIM ** -0.5,
        cols["sq"] * SB_DIM ** -0.5, cols["dk"],
        cols["sk"], cols["dv"],
        cols["sv"], cols["iw"], z(HEAD - IDX_HEADS)], axis=-1)
    assert packed.shape[-1] == W_IN_COLS
    return packed.astype(BF16)


def _pack_mla(w_uq, w_ukv):
    L = w_uq.shape[0]
    dq = MLA_NOPE + MLA_ROPE
    pad_r = LANES - dq
    uq = w_uq.reshape(L, MLA_Q_LORA, MLA_HEADS, dq)
    zq = lambda n: jnp.zeros((L, MLA_Q_LORA, MLA_HEADS, n), F32)
    wqa = jnp.concatenate([uq, zq(pad_r)], axis=-1)
    wqb = jnp.concatenate([zq(MLA_NOPE), _swap_halves(uq[..., MLA_NOPE:]), zq(pad_r)], axis=-1)
    ukv = w_ukv.reshape(L, MLA_KV_LORA, MLA_HEADS, MLA_NOPE + MLA_V)
    wk = jnp.concatenate([ukv[..., :MLA_NOPE],
                          jnp.zeros((L, MLA_KV_LORA, MLA_HEADS, LANES - MLA_NOPE), F32)], axis=-1)
    wv = ukv[..., MLA_NOPE:]
    flat = lambda a: a.reshape(L, a.shape[1], -1).astype(BF16)
    return flat(wqa), flat(wqb), flat(wk), flat(wv)


def _rope_tables(S):
    half = MLA_ROPE // 2
    inv = ROPE_THETA ** (-jnp.arange(half, dtype=F32) / half)
    ang = jnp.arange(S, dtype=jnp.int32).astype(F32)[:, None] * inv[None, :]
    cos, sin = jnp.cos(ang), jnp.sin(ang)
    pad_r = LANES - MLA_NOPE - MLA_ROPE
    cos_t = jnp.concatenate([jnp.ones((S, MLA_NOPE), F32), cos, cos, jnp.zeros((S, pad_r), F32)], axis=-1)
    sin_t = jnp.concatenate([jnp.zeros((S, MLA_NOPE), F32), -sin, sin, jnp.zeros((S, pad_r), F32)], axis=-1)
    return cos_t, sin_t


def kernel(x, p, w_in, attn_norm, mla_q_norm, mla_w_uq, mla_kv_norm, mla_w_ukv, rel_bias, w_o,
           ffn_norm, w_gate, w_up, w_down, ple_norm, w_ple_gate, w_ple_proj, final_norm):
    B, S, D = x.shape
    depth = w_in.shape[0]
    assert S % TM == 0 and S % DSA_TK == 0 and S % MLA_TK == 0
    assert w_gate.shape[-1] % FF_CHUNK == 0
    topk = min(DSA_TOPK, S // 4)

    w_in_p = _pack_w_in(w_in)
    wqa, wqb, wk, wv = _pack_mla(mla_w_uq, mla_w_ukv)
    cos_t, sin_t = _rope_tables(S)
    row = lambda g: g.reshape(depth, 1, -1)
    g_attn, g_q, g_kv, g_ffn, g_ple = map(row, (attn_norm, mla_q_norm, mla_kv_norm, ffn_norm, ple_norm))
    g_fin = final_norm.reshape(1, -1)
    wo, wg, wu, wd, wpg, wpp = (a.astype(BF16) for a in (w_o, w_gate, w_up, w_down, w_ple_gate, w_ple_proj))
    u = jnp.tril(jnp.ones((SB_T, SB_T), F32), -1).astype(BF16)

    tab = _bias_call(rel_bias)
    far = rel_bias[REL_BUCKETS - 1]

    h = x
    for i in range(depth):
        (mq, mk, mv, dq, dk, dv, iq, ik, iw, sq, sk, sv) = _proj_call(
            h, g_attn[i], w_in_p[i], g_q[i], wqa[i], wqb[i], g_kv[i], wk[i], wv[i], cos_t, sin_t)
        o_mla = _mla_call(mq, mk, mv)
        o_dsa = _dsa_call(far, iq, ik, iw, dq, dk, dv, tab, topk)
        o_sb = _sb_call(sq, sk, sv, u)
        h = _post_call(h, o_mla, o_dsa, o_sb, p, i, wo[i], g_ffn[i], wg[i], wu[i], wd[i],
                       g_ple[i], wpg[i], wpp[i], g_fin, final=(i == depth - 1))
    return h
```

```python
import functools
import math

import jax
import jax.numpy as jnp
from jax import lax
from jax.experimental import pallas as pl
from jax.experimental.pallas import tpu as pltpu

EPS = 1e-6
MLA_HEADS = 6
MLA_Q_LORA = 384
MLA_KV_LORA = 256
MLA_NOPE = 64
MLA_ROPE = 32
MLA_V = 64
ROPE_THETA = 10000.0
DSA_HEADS = 5
DSA_DIM = 64
IDX_HEADS = 8
IDX_DIM = 64
DSA_TOPK = 256
SB_HEADS = 5
SB_DIM = 64
REL_BUCKETS = 32
REL_MAX_DIST = 128

LANES = 128
HEAD = 64
VROWS = 80
OUT_W = 384
VMEM_LIMIT = 56 * 1024 * 1024

TM = 512
MLA_TQ, MLA_TK = 256, 512
SB_T = 256
SB_SUBK = 128
DSA_TQ, DSA_TK = 128, 512
DSA_SUB = DSA_TK // DSA_TQ
COUNT_ACC = 8
FF_CHUNK = 1408

NEG = -1e30
INT_MIN = -2 ** 31
NEG_INF_KEY = -2139095041
LOG2E = math.log2(math.e)

F32 = jnp.float32
BF16 = jnp.bfloat16

SEG_CQ = (0, 384)
SEG_CKV = (384, 640)
SEG_KRA = (640, 768)
SEG_KRB = (768, 896)
SEG_SK_IK = (896, 1280)
SEG_DK = (1280, 1408)
W_STD_COLS = 1408
ROW_DQ = (0, 320)
ROW_IQ = (320, 832)
ROW_SQ = (832, 1152)
ROW_SV = (1152, 1472)
ROW_DV = (1472, 1536)
ROW_IW = (1536, 1552)
W_T_ROWS = 1552


def _rms(x, g):
    return x * lax.rsqrt(jnp.mean(x * x, axis=-1, keepdims=True) + EPS) * g


def _dot(a, b):
    return jnp.dot(a, b, preferred_element_type=F32)


def _const_spec(shape):
    n = len(shape)
    return pl.BlockSpec(shape, lambda *_: (0,) * n, pipeline_mode=pl.Buffered(1))


def _params(*sem):
    return pltpu.CompilerParams(dimension_semantics=sem, vmem_limit_bytes=VMEM_LIMIT)


def _pipeline(n, stages):
    held = [dict() for _ in stages]
    for t in range(n + len(stages) - 1):
        for s, stage in enumerate(stages):
            i = t - s
            if 0 <= i < n:
                held[s][i] = stage(i, held[s - 1].pop(i) if s else None)


def _ones_rows(n):
    r = lax.broadcasted_iota(jnp.int32, (VROWS - HEAD, n), 0)
    return jnp.where(r == 0, 1.0, 0.0).astype(BF16)


def _proj_kernel(h_ref, g_ref, w_ref, wt_ref, gq_ref, wqa_ref, wqb_ref, gkv_ref, wk_ref, wv_ref,
                 cos_ref, sin_ref, cost_ref, sint_ref,
                 mqt_ref, mk_ref, mvt_ref, dqt_ref, dk_ref, dvt_ref, iqt_ref, ik_ref, iwt_ref,
                 sqt_ref, sk_ref, svt_ref):
    hn = _rms(h_ref[0], g_ref[...])
    hnt = hn.T.astype(BF16)
    hn = hn.astype(BF16)
    tm = hn.shape[0]
    ones = _ones_rows(tm)

    def seg(ab):
        return _dot(hn, w_ref[:, ab[0]:ab[1]])

    def seg_t(ab):
        return _dot(wt_ref[ab[0]:ab[1], :], hnt)

    cqn_t = _rms(seg(SEG_CQ), gq_ref[...]).T.astype(BF16)
    qa = _dot(wqa_ref[...], cqn_t)
    qb = _dot(wqb_ref[...], cqn_t)
    cos_t = cost_ref[...]
    sin_t = sint_ref[...]
    for h in range(MLA_HEADS):
        sl = slice(h * LANES, (h + 1) * LANES)
        mqt_ref[0, h] = (qa[sl] * cos_t + qb[sl] * sin_t).astype(BF16)

    ckvn = _rms(seg(SEG_CKV), gkv_ref[...])
    ckvn_t = ckvn.T.astype(BF16)
    ckvn = ckvn.astype(BF16)
    kn = _dot(ckvn, wk_ref[...])
    kr = seg(SEG_KRA) * cos_ref[...] + seg(SEG_KRB) * sin_ref[...]
    for h in range(MLA_HEADS):
        mk_ref[0, h] = (kn[:, h * LANES:(h + 1) * LANES] + kr).astype(BF16)
    vt = _dot(wv_ref[...], ckvn_t)
    for h in range(MLA_HEADS):
        mvt_ref[0, h, 0:HEAD, :] = vt[h * HEAD:(h + 1) * HEAD].astype(BF16)
        mvt_ref[0, h, HEAD:VROWS, :] = ones

    a = seg(SEG_SK_IK)
    for h in range(SB_HEADS):
        sk_ref[0, h] = a[:, h * HEAD:(h + 1) * HEAD].astype(BF16)
    ik_ref[0] = a[:, 320:384].astype(BF16)
    dk_ref[0] = seg(SEG_DK)[:, 0:HEAD].astype(BF16)

    a = seg_t(ROW_DQ)
    for h in range(DSA_HEADS):
        dqt_ref[0, h] = a[h * HEAD:(h + 1) * HEAD].astype(BF16)
    a = seg_t(ROW_IQ)
    for h in range(IDX_HEADS):
        iqt_ref[0, h] = a[h * HEAD:(h + 1) * HEAD].astype(BF16)
    a = seg_t(ROW_SQ)
    for h in range(SB_HEADS):
        sqt_ref[0, h] = a[h * HEAD:(h + 1) * HEAD].astype(BF16)
    a = seg_t(ROW_SV)
    for h in range(SB_HEADS):
        svt_ref[0, h] = a[h * HEAD:(h + 1) * HEAD].astype(BF16)
    dvt_ref[0, 0:HEAD, :] = seg_t(ROW_DV).astype(BF16)
    dvt_ref[0, HEAD:VROWS, :] = ones
    iwt_ref[0] = seg_t(ROW_IW)[0:IDX_HEADS] * IDX_HEADS ** -0.5


def _proj_call(h, g, w, wt, gq, wqa, wqb, gkv, wk, wv, cos_s, sin_s, cos_t, sin_t):
    B, S, D = h.shape
    nt = S // TM
    tok = lambda w_: pl.BlockSpec((1, TM, w_), lambda b, t: (b, t, 0))
    heads = lambda n, w_: pl.BlockSpec((1, n, TM, w_), lambda b, t: (b, 0, t, 0))
    heads_t = lambda n, r: pl.BlockSpec((1, n, r, TM), lambda b, t: (b, 0, 0, t))
    rows_t = lambda r: pl.BlockSpec((1, r, TM), lambda b, t: (b, 0, t))
    hs = lambda n, w_: jax.ShapeDtypeStruct((B, n, S, w_), BF16)
    hts = lambda n, r: jax.ShapeDtypeStruct((B, n, r, S), BF16)
    return pl.pallas_call(
        _proj_kernel,
        grid=(B, nt),
        in_specs=[tok(D), _const_spec(g.shape), _const_spec(w.shape), _const_spec(wt.shape),
                  _const_spec(gq.shape), _const_spec(wqa.shape), _const_spec(wqb.shape),
                  _const_spec(gkv.shape), _const_spec(wk.shape), _const_spec(wv.shape),
                  pl.BlockSpec((TM, LANES), lambda b, t: (t, 0)),
                  pl.BlockSpec((TM, LANES), lambda b, t: (t, 0)),
                  pl.BlockSpec((LANES, TM), lambda b, t: (0, t)),
                  pl.BlockSpec((LANES, TM), lambda b, t: (0, t))],
        out_specs=[heads_t(MLA_HEADS, LANES), heads(MLA_HEADS, LANES), heads_t(MLA_HEADS, VROWS),
                   heads_t(DSA_HEADS, HEAD), tok(HEAD), rows_t(VROWS),
                   heads_t(IDX_HEADS, HEAD), tok(HEAD), rows_t(IDX_HEADS),
                   heads_t(SB_HEADS, HEAD), heads(SB_HEADS, HEAD), heads_t(SB_HEADS, HEAD)],
        out_shape=[hts(MLA_HEADS, LANES), hs(MLA_HEADS, LANES), hts(MLA_HEADS, VROWS),
                   hts(DSA_HEADS, HEAD), jax.ShapeDtypeStruct((B, S, HEAD), BF16),
                   jax.ShapeDtypeStruct((B, VROWS, S), BF16),
                   hts(IDX_HEADS, HEAD), jax.ShapeDtypeStruct((B, S, HEAD), BF16),
                   jax.ShapeDtypeStruct((B, IDX_HEADS, S), F32),
                   hts(SB_HEADS, HEAD), hs(SB_HEADS, HEAD), hts(SB_HEADS, HEAD)],
        compiler_params=_params("parallel", "parallel"),
        name="proj",
    )(h, g, w, wt, gq, wqa, wqb, gkv, wk, wv, cos_s, sin_s, cos_t, sin_t)


def _finish(acc_ref, n_heads, o_ref):
    outs = [acc_ref[h, 0:HEAD, :] / acc_ref[h, HEAD:HEAD + 1, :] for h in range(n_heads)]
    pad = OUT_W - n_heads * HEAD
    if pad:
        outs.append(jnp.zeros((pad, outs[0].shape[1]), F32))
    o_ref[0] = jnp.concatenate(outs, axis=0).T.astype(BF16)


def _mla_kernel(qt_ref, k_ref, vt_ref, o_ref, m_ref, acc_ref):
    qi = pl.program_id(1)
    tq, tk = MLA_TQ, MLA_TK
    m_ref[...] = jnp.full(m_ref.shape, NEG, F32)
    acc_ref[...] = jnp.zeros(acc_ref.shape, F32)
    scale2 = (MLA_NOPE + MLA_ROPE) ** -0.5 * LOG2E

    def step(j, masked):
        k0 = pl.multiple_of(j * tk, tk)
        if masked:
            kpos = k0 + lax.broadcasted_iota(jnp.int32, (tk, tq), 0)
            qpos = qi * tq + lax.broadcasted_iota(jnp.int32, (tk, tq), 1)
            causal = kpos <= qpos

        def scores(h, _):
            return _dot(k_ref[0, h, pl.ds(k0, tk), :], qt_ref[0, h])

        def softmax_pv(h, s):
            s = s * scale2
            if masked:
                s = jnp.where(causal, s, NEG)
            m_prev = m_ref[h]
            m_new = jnp.maximum(m_prev, jnp.max(s, axis=0, keepdims=True))
            p = jnp.exp2(s - m_new)
            acc_ref[h] = (jnp.exp2(m_prev - m_new) * acc_ref[h]
                          + _dot(vt_ref[0, h, :, pl.ds(k0, tk)], p.astype(BF16)))
            m_ref[h] = m_new

        _pipeline(MLA_HEADS, [scores, softmax_pv])

    n_full = (qi * tq) // tk
    n_all = ((qi + 1) * tq + tk - 1) // tk

    def full_body(j, c):
        step(j, False)
        return c

    def diag_body(j, c):
        step(j, True)
        return c

    lax.fori_loop(0, n_full, full_body, 0)
    lax.fori_loop(n_full, n_all, diag_body, 0)
    _finish(acc_ref, MLA_HEADS, o_ref)


def _mla_call(qt, k, vt):
    B, H, _, S = qt.shape
    res = lambda shape: pl.BlockSpec(shape, lambda b, i: (b, 0, 0, 0), pipeline_mode=pl.Buffered(1))
    return pl.pallas_call(
        _mla_kernel,
        grid=(B, S // MLA_TQ),
        in_specs=[pl.BlockSpec((1, H, LANES, MLA_TQ), lambda b, i: (b, 0, 0, i)),
                  res((1, H, S, LANES)), res((1, H, VROWS, S))],
        out_specs=pl.BlockSpec((1, MLA_TQ, OUT_W), lambda b, i: (b, i, 0)),
        out_shape=jax.ShapeDtypeStruct((B, S, OUT_W), BF16),
        scratch_shapes=[pltpu.VMEM((H, 1, MLA_TQ), F32), pltpu.VMEM((H, VROWS, MLA_TQ), F32)],
        compiler_params=_params("parallel", "arbitrary"),
        name="mla",
    )(qt, k, vt)


def _sb_kernel(qt_ref, k_ref, vt_ref, u_ref, o_ref, c_ref, acc_ref):
    qi = pl.program_id(1)
    t = SB_T
    n_sub = t // SB_SUBK
    c_ref[...] = jnp.zeros(c_ref.shape, F32)
    acc_ref[...] = jnp.zeros(acc_ref.shape, F32)

    def step(j, masked):
        k0 = pl.multiple_of(j * t, t)
        if masked:
            strict = (lax.broadcasted_iota(jnp.int32, (t, t), 0)
                      < lax.broadcasted_iota(jnp.int32, (t, t), 1))

        def logits(h, _):
            return _dot(k_ref[0, h, pl.ds(k0, t), :], qt_ref[0, h])

        def suffix_sums(h, z):
            z2 = z * LOG2E
            lsz2 = jnp.minimum(z2, 0.0) - jnp.log2(1.0 + jnp.exp2(-jnp.abs(z2)))
            l1m2 = lsz2 - z2
            if masked:
                l1m2 = jnp.where(strict, l1m2, 0.0)
            hi = l1m2.astype(BF16)
            lo = (l1m2 - hi.astype(F32)).astype(BF16)
            sums = []
            for s in range(n_sub):
                rows = slice(s * SB_SUBK, (s + 1) * SB_SUBK)
                sums.append(_dot(u_ref[...], jnp.concatenate([hi[rows], lo[rows]], axis=0)))
            return lsz2, sums

        def weights_pv(h, carry):
            lsz2, sums = carry
            run = jnp.zeros((1, t), F32)
            blk = None
            for s in reversed(range(n_sub)):
                rows = slice(s * SB_SUBK, (s + 1) * SB_SUBK)
                w = jnp.exp2(lsz2[rows] + sums[s][0:SB_SUBK])
                if masked:
                    w = jnp.where(strict[rows], w, 0.0)
                part = _dot(vt_ref[0, h, :, pl.ds(k0 + s * SB_SUBK, SB_SUBK)], w.astype(BF16))
                blk = part if blk is None else blk + jnp.exp2(run) * part
                run = run + sums[s][SB_SUBK:SB_SUBK + 1]
            acc_ref[h] += jnp.exp2(c_ref[h]) * blk
            c_ref[h] += run

        _pipeline(SB_HEADS, [logits, suffix_sums, weights_pv])

    step(qi, True)

    def body(i, c):
        step(qi - 1 - i, False)
        return c

    lax.fori_loop(0, qi, body, 0)
    outs = [acc_ref[h] for h in range(SB_HEADS)]
    outs.append(jnp.zeros((OUT_W - SB_HEADS * HEAD, t), F32))
    o_ref[0] = jnp.concatenate(outs, axis=0).T.astype(BF16)


def _sb_call(qt, k, vt, u):
    B, H, _, S = qt.shape
    res = lambda shape: pl.BlockSpec(shape, lambda b, i: (b, 0, 0, 0), pipeline_mode=pl.Buffered(1))
    return pl.pallas_call(
        _sb_kernel,
        grid=(B, S // SB_T),
        in_specs=[pl.BlockSpec((1, H, HEAD, SB_T), lambda b, i: (b, 0, 0, i)),
                  res((1, H, S, HEAD)), res((1, H, HEAD, S)), _const_spec(u.shape)],
        out_specs=pl.BlockSpec((1, SB_T, OUT_W), lambda b, i: (b, i, 0)),
        out_shape=jax.ShapeDtypeStruct((B, S, OUT_W), BF16),
        scratch_shapes=[pltpu.VMEM((H, 1, SB_T), F32), pltpu.VMEM((H, HEAD, SB_T), F32)],
        compiler_params=_params("parallel", "arbitrary"),
        name="stickbreak",
    )(qt, k, vt, u)


def _bias_kernel(rb_ref, o_ref):
    v = pl.program_id(0)
    shape = (DSA_TK, DSA_TQ)
    off = jnp.where(v < DSA_SUB, v * DSA_TQ, DSA_TK)
    dist = off + lax.broadcasted_iota(jnp.int32, shape, 1) - lax.broadcasted_iota(jnp.int32, shape, 0)
    max_exact = REL_BUCKETS // 2
    d = jnp.maximum(dist, 1).astype(F32)
    large = max_exact + (jnp.log(d / max_exact) / math.log(REL_MAX_DIST / max_exact)
                         * (REL_BUCKETS - max_exact)).astype(jnp.int32)
    large = jnp.minimum(large, REL_BUCKETS - 1)
    bucket = jnp.where(dist < max_exact, dist, large)
    for h in range(DSA_HEADS):
        t = jnp.zeros(shape, F32)
        for b in range(REL_BUCKETS):
            t = jnp.where(bucket == b, rb_ref[b, h], t)
        o_ref[0, h] = jnp.where(dist >= 0, t, NEG)


def _bias_call(rel_bias):
    nv = DSA_SUB + 1
    return pl.pallas_call(
        _bias_kernel,
        grid=(nv,),
        in_specs=[pl.BlockSpec(memory_space=pltpu.SMEM)],
        out_specs=pl.BlockSpec((1, DSA_HEADS, DSA_TK, DSA_TQ), lambda v: (v, 0, 0, 0)),
        out_shape=jax.ShapeDtypeStruct((nv, DSA_HEADS, DSA_TK, DSA_TQ), F32),
        compiler_params=_params("arbitrary"),
        name="bias_tiles",
    )(rel_bias)


def _dsa_kernel(far_ref, iqt_ref, ik_ref, iwt_ref, dqt_ref, dk_ref, dvt_ref, tab_ref, o_ref,
                keys_ref, m_ref, acc_ref, *, topk):
    qi = pl.program_id(1)
    n_blk = qi // DSA_SUB + 1
    sub = qi % DSA_SUB
    tq, tk = DSA_TQ, DSA_TK

    iqt = jnp.concatenate([iqt_ref[0, h] for h in range(IDX_HEADS)], axis=1)
    iw = iwt_ref[0]

    def idx_step(j, masked):
        k0 = pl.multiple_of(j * tk, tk)
        kb = ik_ref[0, pl.ds(k0, tk), :]
        total = []

        def scores(g, _):
            return _dot(kb, iqt[:, 2 * g * tq:2 * (g + 1) * tq])

        def weigh(g, x):
            part = (jnp.maximum(x[:, 0:tq], 0.0) * iw[2 * g:2 * g + 1]
                    + jnp.maximum(x[:, tq:2 * tq], 0.0) * iw[2 * g + 1:2 * g + 2])
            total[:] = [part if not total else total[0] + part]

        _pipeline(IDX_HEADS // 2, [scores, weigh])
        isc = total[0]
        isc = jnp.where(isc == 0.0, 0.0, isc)
        if masked:
            kpos = k0 + lax.broadcasted_iota(jnp.int32, isc.shape, 0)
            qpos = qi * tq + lax.broadcasted_iota(jnp.int32, isc.shape, 1)
            isc = jnp.where(kpos <= qpos, isc, -jnp.inf)
        bits = pltpu.bitcast(isc, jnp.int32)
        keys_ref[j] = bits ^ (lax.shift_right_arithmetic(bits, 31) & 0x7FFFFFFF)

    def idx_body(j, c):
        idx_step(j, False)
        return c

    lax.fori_loop(0, n_blk - 1, idx_body, 0)
    idx_step(n_blk - 1, True)

    @pl.when(n_blk % 2 == 1)
    def _():
        keys_ref[n_blk] = jnp.full((tk, tq), INT_MIN, jnp.int32)

    def count_ge(cand):
        def body(i, acc):
            for j in (2 * i, 2 * i + 1):
                c = jnp.where(keys_ref[j] >= cand, 1.0, 0.0)
                acc = acc + jnp.sum(c.reshape(COUNT_ACC, tk // COUNT_ACC, tq), axis=0)
            return acc
        acc = lax.fori_loop(0, (n_blk + 1) // 2, body, jnp.zeros((tk // COUNT_ACC, tq), F32))
        return jnp.sum(acc, axis=0, keepdims=True)

    def bit_cond(state):
        i, _, done, _ = state
        return jnp.logical_and(i < 32, jnp.min(done) < 1.0)

    def bit_body(state):
        i, t_u, done, thr_hit = state
        cand_u = t_u | lax.shift_left(jnp.int32(1), 31 - i)
        cand = cand_u ^ INT_MIN
        cnt = count_ge(cand)
        hit = cnt == topk
        thr_hit = jnp.where(hit, jnp.where(done > 0.0, thr_hit, cand), thr_hit)
        done = jnp.where(hit, 1.0, done)
        return i + 1, jnp.where(cnt >= topk, cand_u, t_u), done, thr_hit

    zero = jnp.zeros((1, tq), jnp.int32)
    _, t_u, done, thr_hit = lax.while_loop(
        bit_cond, bit_body, (jnp.int32(0), zero, jnp.zeros((1, tq), F32), zero))
    thr = jnp.where(done > 0.0, thr_hit, t_u ^ INT_MIN)

    tie = jnp.where(done > 0.0, 0.0, jnp.where(thr > NEG_INF_KEY, 1.0, 0.0))

    @pl.when(jnp.max(tie) > 0.0)
    def _():
        need = jnp.where(tie > 0.0, topk - count_ge(thr + 1), float(tk * keys_ref.shape[0] + 1))
        row = lax.broadcasted_iota(jnp.int32, (tk, tk), 0)
        col = lax.broadcasted_iota(jnp.int32, (tk, tk), 1)
        before = jnp.where(col < row, 1.0, 0.0).astype(BF16)

        def body(j, run):
            kb = keys_ref[j]
            eq = jnp.where(kb == thr, 1.0, 0.0)
            rank = _dot(before, eq.astype(BF16)) + run
            keys_ref[j] = jnp.where(kb == thr, jnp.where(rank >= need, thr - 1, kb), kb)
            return run + jnp.sum(eq, axis=0, keepdims=True)

        lax.fori_loop(0, n_blk, body, jnp.zeros((1, tq), F32))

    dqt = jnp.concatenate([dqt_ref[0, h] for h in range(DSA_HEADS)], axis=1)
    m_ref[...] = jnp.full(m_ref.shape, NEG, F32)
    acc_ref[...] = jnp.zeros(acc_ref.shape, F32)

    def att_step(j, variant):
        k0 = pl.multiple_of(j * tk, tk)
        kb = dk_ref[0, pl.ds(k0, tk), :]
        vt = dvt_ref[0, :, pl.ds(k0, tk)]
        sel = keys_ref[j] >= thr

        def scores(g, _):
            return _dot(kb, dqt[:, 2 * g * tq:min(2 * (g + 1), DSA_HEADS) * tq])

        def softmax_pv(g, s):
            for h in range(2 * g, min(2 * (g + 1), DSA_HEADS)):
                bias = far_ref[h] if variant is None else tab_ref[variant, h]
                a = jnp.where(sel, s[:, (h - 2 * g) * tq:(h - 2 * g + 1) * tq] + bias, NEG)
                m_prev = m_ref[h]
                m_new = jnp.maximum(m_prev, jnp.max(a, axis=0, keepdims=True))
                p = jnp.exp(a - m_new)
                acc_ref[h] = jnp.exp(m_prev - m_new) * acc_ref[h] + _dot(vt, p.astype(BF16))
                m_ref[h] = m_new

        _pipeline((DSA_HEADS + 1) // 2, [scores, softmax_pv])

    def far_body(j, c):
        att_step(j, None)
        return c

    lax.fori_loop(0, n_blk - 2, far_body, 0)

    @pl.when(jnp.logical_and(n_blk >= 2, sub == 0))
    def _():
        att_step(n_blk - 2, DSA_SUB)

    @pl.when(jnp.logical_and(n_blk >= 2, sub != 0))
    def _():
        att_step(n_blk - 2, None)

    att_step(n_blk - 1, sub)
    _finish(acc_ref, DSA_HEADS, o_ref)


def _dsa_call(far, iqt, ik, iwt, dqt, dk, dvt, tab, topk):
    B, _, _, S = iqt.shape
    tq = DSA_TQ
    qh = lambda n: pl.BlockSpec((1, n, HEAD, tq), lambda b, i: (b, 0, 0, i))
    res = lambda shape: pl.BlockSpec(shape, lambda b, i: (b, 0, 0), pipeline_mode=pl.Buffered(1))
    return pl.pallas_call(
        functools.partial(_dsa_kernel, topk=topk),
        grid=(B, S // tq),
        in_specs=[pl.BlockSpec(memory_space=pltpu.SMEM),
                  qh(IDX_HEADS), res((1, S, HEAD)),
                  pl.BlockSpec((1, IDX_HEADS, tq), lambda b, i: (b, 0, i)),
                  qh(DSA_HEADS), res((1, S, HEAD)), res((1, VROWS, S)), _const_spec(tab.shape)],
        out_specs=pl.BlockSpec((1, tq, OUT_W), lambda b, i: (b, i, 0)),
        out_shape=jax.ShapeDtypeStruct((B, S, OUT_W), BF16),
        scratch_shapes=[pltpu.VMEM((S // DSA_TK + (S // DSA_TK) % 2, DSA_TK, tq), jnp.int32),
                        pltpu.VMEM((DSA_HEADS, 1, tq), F32),
                        pltpu.VMEM((DSA_HEADS, VROWS, tq), F32)],
        compiler_params=_params("parallel", "arbitrary"),
        name="dsa",
    )(far, iqt, ik, iwt, dqt, dk, dvt, tab)


def _post_kernel(h_ref, om_ref, od_ref, os_ref, p_ref, wo_ref, gf_ref, wg_ref, wu_ref, wd_ref,
                 gp_ref, wpg_ref, wpp_ref, gfin_ref, out_ref, *, final):
    h = (h_ref[0] + _dot(om_ref[0], wo_ref[0]) + _dot(od_ref[0], wo_ref[1])
         + _dot(os_ref[0], wo_ref[2]))

    hf = _rms(h, gf_ref[...]).astype(BF16)
    d_ff = wg_ref.shape[1]
    ffn = jnp.zeros(h.shape, F32)
    for c0 in range(0, d_ff, FF_CHUNK):
        g = _dot(hf, wg_ref[:, c0:c0 + FF_CHUNK])
        u = _dot(hf, wu_ref[:, c0:c0 + FF_CHUNK])
        ffn = ffn + _dot((g * jax.nn.sigmoid(g) * u).astype(BF16), wd_ref[c0:c0 + FF_CHUNK, :])
    h = h + ffn

    gate = jax.nn.sigmoid(_dot(_rms(h, gp_ref[...]).astype(BF16), wpg_ref[...]))
    h = h + gate * _dot(p_ref[0, 0].astype(BF16), wpp_ref[...])
    if final:
        h = _rms(h, gfin_ref[...])
    out_ref[0] = h


def _post_call(h, om, od, os_, p, layer, wo, gf, wg, wu, wd, gp, wpg, wpp, gfin, final):
    B, S, D = h.shape
    mix = pl.BlockSpec((1, TM, OUT_W), lambda b, t: (b, t, 0))
    tok = pl.BlockSpec((1, TM, D), lambda b, t: (b, t, 0))
    return pl.pallas_call(
        functools.partial(_post_kernel, final=final),
        grid=(B, S // TM),
        in_specs=[tok, mix, mix, mix,
                  pl.BlockSpec((1, 1, TM, p.shape[-1]), lambda b, t: (layer, b, t, 0)),
                  _const_spec(wo.shape), _const_spec(gf.shape), _const_spec(wg.shape),
                  _const_spec(wu.shape), _const_spec(wd.shape), _const_spec(gp.shape),
                  _const_spec(wpg.shape), _const_spec(wpp.shape), _const_spec(gfin.shape)],
        out_specs=tok,
        out_shape=jax.ShapeDtypeStruct((B, S, D), F32),
        compiler_params=_params("parallel", "parallel"),
        name="post",
    )(h, om, od, os_, p, wo, gf, wg, wu, wd, gp, wpg, wpp, gfin)


def _swap_halves(w):
    half = w.shape[-1] // 2
    return jnp.concatenate([w[..., half:], w[..., :half]], axis=-1)


def _pack_w_in(w_in):
    L, D, _ = w_in.shape
    z = lambda n: jnp.zeros((L, D, n), F32)
    o = 0
    cols = {}
    for name, n in (("cq", MLA_Q_LORA), ("ckv", MLA_KV_LORA), ("kr", MLA_ROPE),
                    ("dq", DSA_HEADS * DSA_DIM), ("dk", DSA_DIM), ("dv", DSA_DIM),
                    ("iq", IDX_HEADS * IDX_DIM), ("ik", IDX_DIM), ("iw", IDX_HEADS),
                    ("sq", SB_HEADS * SB_DIM), ("sk", SB_HEADS * SB_DIM), ("sv", SB_HEADS * SB_DIM)):
        cols[name] = w_in[:, :, o:o + n]
        o += n
    pad_r = LANES - MLA_NOPE - MLA_ROPE
    std = jnp.concatenate([
        cols["cq"], cols["ckv"],
        z(MLA_NOPE), cols["kr"], z(pad_r),
        z(MLA_NOPE), _swap_halves(cols["kr"]), z(pad_r),
        cols["sk"], cols["ik"],
        cols["dk"], z(LANES - DSA_DIM)], axis=-1)
    tr = jnp.concatenate([
        cols["dq"] * DSA_DIM ** -0.5, cols["iq"] * IDX_DIM ** -0.5, cols["sq"] * SB_DIM ** -0.5,
        cols["sv"], cols["dv"], cols["iw"], z(ROW_IW[1] - ROW_IW[0] - IDX_HEADS)], axis=-1)
    assert std.shape[-1] == W_STD_COLS and tr.shape[-1] == W_T_ROWS
    return std.astype(BF16), jnp.swapaxes(tr, 1, 2).astype(BF16)


def _pack_mla(w_uq, w_ukv):
    L = w_uq.shape[0]
    dq = MLA_NOPE + MLA_ROPE
    pad_r = LANES - dq
    uq = w_uq.reshape(L, MLA_Q_LORA, MLA_HEADS, dq)
    zq = lambda n: jnp.zeros((L, MLA_Q_LORA, MLA_HEADS, n), F32)
    wqa = jnp.concatenate([uq, zq(pad_r)], axis=-1)
    wqb = jnp.concatenate([zq(MLA_NOPE), _swap_halves(uq[..., MLA_NOPE:]), zq(pad_r)], axis=-1)
    ukv = w_ukv.reshape(L, MLA_KV_LORA, MLA_HEADS, MLA_NOPE + MLA_V)
    wk = jnp.concatenate([ukv[..., :MLA_NOPE],
                          jnp.zeros((L, MLA_KV_LORA, MLA_HEADS, LANES - MLA_NOPE), F32)], axis=-1)
    wv = ukv[..., MLA_NOPE:]
    flat = lambda a: a.reshape(L, a.shape[1], -1)
    tr = lambda a: jnp.swapaxes(flat(a), 1, 2).astype(BF16)
    return tr(wqa), tr(wqb), flat(wk).astype(BF16), tr(wv)


def _rope_tables(S):
    half = MLA_ROPE // 2
    inv = ROPE_THETA ** (-jnp.arange(half, dtype=F32) / half)
    ang = jnp.arange(S, dtype=jnp.int32).astype(F32)[:, None] * inv[None, :]
    cos, sin = jnp.cos(ang), jnp.sin(ang)
    pad_r = LANES - MLA_NOPE - MLA_ROPE
    cos_t = jnp.concatenate([jnp.ones((S, MLA_NOPE), F32), cos, cos, jnp.zeros((S, pad_r), F32)], axis=-1)
    sin_t = jnp.concatenate([jnp.zeros((S, MLA_NOPE), F32), -sin, sin, jnp.zeros((S, pad_r), F32)], axis=-1)
    return cos_t, sin_t


def _pack_w_o(w_o):
    L, _, D = w_o.shape
    a = MLA_HEADS * MLA_V
    b = a + DSA_HEADS * DSA_DIM
    pad = jnp.zeros((L, OUT_W - DSA_HEADS * DSA_DIM, D), F32)
    blocks = [w_o[:, :a], jnp.concatenate([w_o[:, a:b], pad], axis=1),
              jnp.concatenate([w_o[:, b:], pad], axis=1)]
    return jnp.stack(blocks, axis=1).astype(BF16)


def kernel(x, p, w_in, attn_norm, mla_q_norm, mla_w_uq, mla_kv_norm, mla_w_ukv, rel_bias, w_o,
           ffn_norm, w_gate, w_up, w_down, ple_norm, w_ple_gate, w_ple_proj, final_norm):
    B, S, D = x.shape
    depth = w_in.shape[0]
    assert S % TM == 0 and S % DSA_TK == 0 and S % MLA_TK == 0
    assert w_gate.shape[-1] % FF_CHUNK == 0
    topk = min(DSA_TOPK, S // 4)

    w_std, w_tr = _pack_w_in(w_in)
    wqa, wqb, wk, wv = _pack_mla(mla_w_uq, mla_w_ukv)
    cos_s, sin_s = _rope_tables(S)
    cos_t, sin_t = cos_s.T, sin_s.T
    row = lambda g: g.reshape(depth, 1, -1)
    g_attn, g_q, g_kv, g_ffn, g_ple = map(row, (attn_norm, mla_q_norm, mla_kv_norm, ffn_norm, ple_norm))
    g_fin = final_norm.reshape(1, -1)
    wo = _pack_w_o(w_o)
    wg, wu, wd, wpg, wpp = (a.astype(BF16) for a in (w_gate, w_up, w_down, w_ple_gate, w_ple_proj))
    tri = jnp.triu(jnp.ones((SB_SUBK, SB_SUBK), F32), 1)
    tail = jnp.zeros((16, SB_SUBK), F32).at[0].set(1.0)
    u = jnp.concatenate([tri, tail], axis=0)
    u = jnp.concatenate([u, u], axis=1).astype(BF16)

    tab = _bias_call(rel_bias)
    far = rel_bias[REL_BUCKETS - 1]

    h = x
    for i in range(depth):
        (mqt, mk, mvt, dqt, dk, dvt, iqt, ik, iwt, sqt, sk, svt) = _proj_call(
            h, g_attn[i], w_std[i], w_tr[i], g_q[i], wqa[i], wqb[i], g_kv[i], wk[i], wv[i],
            cos_s, sin_s, cos_t, sin_t)
        o_mla = _mla_call(mqt, mk, mvt)
        o_dsa = _dsa_call(far, iqt, ik, iwt, dqt, dk, dvt, tab, topk)
        o_sb = _sb_call(sqt, sk, svt, u)
        h = _post_call(h, o_mla, o_dsa, o_sb, p, i, wo[i], g_ffn[i], wg[i], wu[i], wd[i],
                       g_ple[i], wpg[i], wpp[i], g_fin, final=(i == depth - 1))
    return h
```

```python
import functools
import math

import jax
import jax.numpy as jnp
from jax import lax
from jax.experimental import pallas as pl
from jax.experimental.pallas import tpu as pltpu

EPS = 1e-6
MLA_HEADS = 6
MLA_Q_LORA = 384
MLA_KV_LORA = 256
MLA_NOPE = 64
MLA_ROPE = 32
MLA_V = 64
ROPE_THETA = 10000.0
DSA_HEADS = 5
DSA_DIM = 64
IDX_HEADS = 8
IDX_DIM = 64
DSA_TOPK = 256
SB_HEADS = 5
SB_DIM = 64
REL_BUCKETS = 32
REL_MAX_DIST = 128

LANES = 128
HEAD = 64
VROWS = 80
OUT_W = 384
VMEM_LIMIT = 56 * 1024 * 1024

TM = 512
MLA_TQ, MLA_TK = 256, 512
SB_T = 256
SB_SUBK = 128
DSA_TQ, DSA_TK = 128, 512
DSA_SUB = DSA_TK // DSA_TQ
COUNT_ACC = 8
FF_CHUNK = 1408

NEG = -1e30
INT_MIN = -2 ** 31
NEG_INF_KEY = -2139095041
LOG2E = math.log2(math.e)

F32 = jnp.float32
BF16 = jnp.bfloat16

SEG_CQ = (0, 384)
SEG_CKV = (384, 640)
SEG_KRA = (640, 768)
SEG_KRB = (768, 896)
SEG_SK_IK = (896, 1280)
SEG_DK = (1280, 1408)
W_STD_COLS = 1408
ROW_DQ = (0, 320)
ROW_IQ = (320, 832)
ROW_SQ = (832, 1152)
ROW_SV = (1152, 1472)
ROW_DV = (1472, 1536)
ROW_IW = (1536, 1552)
W_T_ROWS = 1552


def _rms(x, g):
    return x * lax.rsqrt(jnp.mean(x * x, axis=-1, keepdims=True) + EPS) * g


def _dot(a, b):
    return jnp.dot(a, b, preferred_element_type=F32)


def _const_spec(shape):
    n = len(shape)
    return pl.BlockSpec(shape, lambda *_: (0,) * n, pipeline_mode=pl.Buffered(1))


def _params(*sem):
    return pltpu.CompilerParams(dimension_semantics=sem, vmem_limit_bytes=VMEM_LIMIT)


def _pipeline(n, stages):
    held = [dict() for _ in stages]
    for t in range(n + len(stages) - 1):
        for s, stage in enumerate(stages):
            i = t - s
            if 0 <= i < n:
                held[s][i] = stage(i, held[s - 1].pop(i) if s else None)


def _ones_rows(n):
    r = lax.broadcasted_iota(jnp.int32, (VROWS - HEAD, n), 0)
    return jnp.where(r == 0, 1.0, 0.0).astype(BF16)


def _proj_kernel(h_ref, g_ref, w_ref, wt_ref, gq_ref, wqa_ref, wqb_ref, gkv_ref, wk_ref, wv_ref,
                 cos_ref, sin_ref, cost_ref, sint_ref,
                 mqt_ref, mk_ref, mvt_ref, dqt_ref, dk_ref, dvt_ref, iqt_ref, ik_ref, iwt_ref,
                 sqt_ref, sk_ref, svt_ref):
    hn = _rms(h_ref[0], g_ref[...])
    hnt = hn.T.astype(BF16)
    hn = hn.astype(BF16)
    tm = hn.shape[0]
    ones = _ones_rows(tm)

    def seg(ab):
        return _dot(hn, w_ref[:, ab[0]:ab[1]])

    def seg_t(ab):
        return _dot(wt_ref[ab[0]:ab[1], :], hnt)

    cqn_t = _rms(seg(SEG_CQ), gq_ref[...]).T.astype(BF16)
    qa = _dot(wqa_ref[...], cqn_t)
    qb = _dot(wqb_ref[...], cqn_t)
    cos_t = cost_ref[...]
    sin_t = sint_ref[...]
    for h in range(MLA_HEADS):
        sl = slice(h * LANES, (h + 1) * LANES)
        mqt_ref[0, h] = (qa[sl] * cos_t + qb[sl] * sin_t).astype(BF16)

    ckvn = _rms(seg(SEG_CKV), gkv_ref[...])
    ckvn_t = ckvn.T.astype(BF16)
    ckvn = ckvn.astype(BF16)
    kn = _dot(ckvn, wk_ref[...])
    kr = seg(SEG_KRA) * cos_ref[...] + seg(SEG_KRB) * sin_ref[...]
    for h in range(MLA_HEADS):
        mk_ref[0, h] = (kn[:, h * LANES:(h + 1) * LANES] + kr).astype(BF16)
    vt = _dot(wv_ref[...], ckvn_t)
    for h in range(MLA_HEADS):
        mvt_ref[0, h, 0:HEAD, :] = vt[h * HEAD:(h + 1) * HEAD].astype(BF16)
        mvt_ref[0, h, HEAD:VROWS, :] = ones

    a = seg(SEG_SK_IK)
    for h in range(SB_HEADS):
        sk_ref[0, h] = a[:, h * HEAD:(h + 1) * HEAD].astype(BF16)
    ik_ref[0] = a[:, 320:384].astype(BF16)
    dk_ref[0] = seg(SEG_DK)[:, 0:HEAD].astype(BF16)

    a = seg_t(ROW_DQ)
    for h in range(DSA_HEADS):
        dqt_ref[0, h] = a[h * HEAD:(h + 1) * HEAD].astype(BF16)
    a = seg_t(ROW_IQ)
    for h in range(IDX_HEADS):
        iqt_ref[0, h] = a[h * HEAD:(h + 1) * HEAD].astype(BF16)
    a = seg_t(ROW_SQ)
    for h in range(SB_HEADS):
        sqt_ref[0, h] = a[h * HEAD:(h + 1) * HEAD].astype(BF16)
    a = seg_t(ROW_SV)
    for h in range(SB_HEADS):
        svt_ref[0, h] = a[h * HEAD:(h + 1) * HEAD].astype(BF16)
    dvt_ref[0, 0:HEAD, :] = seg_t(ROW_DV).astype(BF16)
    dvt_ref[0, HEAD:VROWS, :] = ones
    iwt_ref[0] = seg_t(ROW_IW)[0:IDX_HEADS] * IDX_HEADS ** -0.5


def _proj_call(h, g, w, wt, gq, wqa, wqb, gkv, wk, wv, cos_s, sin_s, cos_t, sin_t):
    B, S, D = h.shape
    nt = S // TM
    tok = lambda w_: pl.BlockSpec((1, TM, w_), lambda b, t: (b, t, 0))
    heads = lambda n, w_: pl.BlockSpec((1, n, TM, w_), lambda b, t: (b, 0, t, 0))
    heads_t = lambda n, r: pl.BlockSpec((1, n, r, TM), lambda b, t: (b, 0, 0, t))
    rows_t = lambda r: pl.BlockSpec((1, r, TM), lambda b, t: (b, 0, t))
    hs = lambda n, w_: jax.ShapeDtypeStruct((B, n, S, w_), BF16)
    hts = lambda n, r: jax.ShapeDtypeStruct((B, n, r, S), BF16)
    return pl.pallas_call(
        _proj_kernel,
        grid=(B, nt),
        in_specs=[tok(D), _const_spec(g.shape), _const_spec(w.shape), _const_spec(wt.shape),
                  _const_spec(gq.shape), _const_spec(wqa.shape), _const_spec(wqb.shape),
                  _const_spec(gkv.shape), _const_spec(wk.shape), _const_spec(wv.shape),
                  pl.BlockSpec((TM, LANES), lambda b, t: (t, 0)),
                  pl.BlockSpec((TM, LANES), lambda b, t: (t, 0)),
                  pl.BlockSpec((LANES, TM), lambda b, t: (0, t)),
                  pl.BlockSpec((LANES, TM), lambda b, t: (0, t))],
        out_specs=[heads_t(MLA_HEADS, LANES), heads(MLA_HEADS, LANES), heads_t(MLA_HEADS, VROWS),
                   heads_t(DSA_HEADS, HEAD), tok(HEAD), rows_t(VROWS),
                   heads_t(IDX_HEADS, HEAD), tok(HEAD), rows_t(IDX_HEADS),
                   heads_t(SB_HEADS, HEAD), heads(SB_HEADS, HEAD), heads_t(SB_HEADS, HEAD)],
        out_shape=[hts(MLA_HEADS, LANES), hs(MLA_HEADS, LANES), hts(MLA_HEADS, VROWS),
                   hts(DSA_HEADS, HEAD), jax.ShapeDtypeStruct((B, S, HEAD), BF16),
                   jax.ShapeDtypeStruct((B, VROWS, S), BF16),
                   hts(IDX_HEADS, HEAD), jax.ShapeDtypeStruct((B, S, HEAD), BF16),
                   jax.ShapeDtypeStruct((B, IDX_HEADS, S), F32),
                   hts(SB_HEADS, HEAD), hs(SB_HEADS, HEAD), hts(SB_HEADS, HEAD)],
        compiler_params=_params("parallel", "parallel"),
        name="proj",
    )(h, g, w, wt, gq, wqa, wqb, gkv, wk, wv, cos_s, sin_s, cos_t, sin_t)


def _finish(acc_ref, n_heads, o_ref):
    outs = [acc_ref[h, 0:HEAD, :] / acc_ref[h, HEAD:HEAD + 1, :] for h in range(n_heads)]
    pad = OUT_W - n_heads * HEAD
    if pad:
        outs.append(jnp.zeros((pad, outs[0].shape[1]), F32))
    o_ref[0] = jnp.concatenate(outs, axis=0).T.astype(BF16)


def _mla_kernel(qt_ref, k_ref, vt_ref, o_ref, m_ref, acc_ref, s_ref):
    qi = pl.program_id(1)
    tq, tk = MLA_TQ, MLA_TK
    m_ref[...] = jnp.full(m_ref.shape, NEG, F32)
    acc_ref[...] = jnp.zeros(acc_ref.shape, F32)
    scale2 = (MLA_NOPE + MLA_ROPE) ** -0.5 * LOG2E
    n_full = (qi * tq) // tk
    n_all = ((qi + 1) * tq + tk - 1) // tk

    def qk(h, j):
        k0 = pl.multiple_of(j * tk, tk)
        half = tk // 4
        return jnp.concatenate(
            [_dot(k_ref[0, h, pl.ds(k0 + i * half, half), :], qt_ref[0, h]) for i in range(4)],
            axis=0)

    s_ref[...] = qk(0, 0)

    def step(j, masked):
        k0 = pl.multiple_of(j * tk, tk)
        if masked:
            kpos = k0 + lax.broadcasted_iota(jnp.int32, (tk, tq), 0)
            qpos = qi * tq + lax.broadcasted_iota(jnp.int32, (tk, tq), 1)
            causal = kpos <= qpos

        def scores(h, _):
            if h == 0:
                return s_ref[...]
            if h == MLA_HEADS:
                return qk(0, jnp.minimum(j + 1, n_all - 1))
            return qk(h, j)

        def softmax_pv(h, s):
            if h == MLA_HEADS:
                s_ref[...] = s
                return
            s = s * scale2
            if masked:
                s = jnp.where(causal, s, NEG)
            m_prev = m_ref[h]
            m_new = jnp.maximum(m_prev, jnp.max(s, axis=0, keepdims=True))
            p = jnp.exp2(s - m_new)
            acc_ref[h] = (jnp.exp2(m_prev - m_new) * acc_ref[h]
                          + _dot(vt_ref[0, h, :, pl.ds(k0, tk)], p.astype(BF16)))
            m_ref[h] = m_new

        _pipeline(MLA_HEADS + 1, [scores, softmax_pv])

    def full_body(j, c):
        step(j, False)
        return c

    def diag_body(j, c):
        step(j, True)
        return c

    lax.fori_loop(0, n_full, full_body, 0)
    lax.fori_loop(n_full, n_all, diag_body, 0)
    _finish(acc_ref, MLA_HEADS, o_ref)


def _mla_call(qt, k, vt):
    B, H, _, S = qt.shape
    res = lambda shape: pl.BlockSpec(shape, lambda b, i: (b, 0, 0, 0), pipeline_mode=pl.Buffered(1))
    return pl.pallas_call(
        _mla_kernel,
        grid=(B, S // MLA_TQ),
        in_specs=[pl.BlockSpec((1, H, LANES, MLA_TQ), lambda b, i: (b, 0, 0, i)),
                  res((1, H, S, LANES)), res((1, H, VROWS, S))],
        out_specs=pl.BlockSpec((1, MLA_TQ, OUT_W), lambda b, i: (b, i, 0)),
        out_shape=jax.ShapeDtypeStruct((B, S, OUT_W), BF16),
        scratch_shapes=[pltpu.VMEM((H, 1, MLA_TQ), F32), pltpu.VMEM((H, VROWS, MLA_TQ), F32),
                        pltpu.VMEM((MLA_TK, MLA_TQ), F32)],
        compiler_params=_params("parallel", "arbitrary"),
        name="mla",
    )(qt, k, vt)


def _sb_kernel(qt_ref, k_ref, vt_ref, u_ref, o_ref, c_ref, acc_ref, z_ref):
    qi = pl.program_id(1)
    t = SB_T
    n_sub = t // SB_SUBK
    c_ref[...] = jnp.zeros(c_ref.shape, F32)
    acc_ref[...] = jnp.zeros(acc_ref.shape, F32)

    def qk(h, j):
        k0 = pl.multiple_of(j * t, t)
        return jnp.concatenate(
            [_dot(k_ref[0, h, pl.ds(k0 + s * SB_SUBK, SB_SUBK), :], qt_ref[0, h]) for s in range(n_sub)],
            axis=0)

    z_ref[...] = qk(0, qi)

    def step(j, masked):
        k0 = pl.multiple_of(j * t, t)
        if masked:
            strict = (lax.broadcasted_iota(jnp.int32, (t, t), 0)
                      < lax.broadcasted_iota(jnp.int32, (t, t), 1))

        def logits(h, _):
            if h == 0:
                return z_ref[...]
            if h == SB_HEADS:
                return qk(0, jnp.maximum(j - 1, 0))
            return qk(h, j)

        def suffix_sums(h, z):
            if h == SB_HEADS:
                z_ref[...] = z
                return None
            z2 = z * LOG2E
            neg_abs = pltpu.bitcast(pltpu.bitcast(z2, jnp.int32) | INT_MIN, F32)
            lsz2 = jnp.minimum(z2, 0.0) - jnp.log2(1.0 + jnp.exp2(neg_abs))
            l1m2 = lsz2 - z2
            if masked:
                l1m2 = jnp.where(strict, l1m2, 0.0)
            hi = l1m2.astype(BF16)
            lo = (l1m2 - hi.astype(F32)).astype(BF16)
            sums = []
            for s in range(n_sub):
                rows = slice(s * SB_SUBK, (s + 1) * SB_SUBK)
                sums.append(_dot(u_ref[...], jnp.concatenate([hi[rows], lo[rows]], axis=0)))
            return lsz2, sums

        def weights_pv(h, carry):
            if h == SB_HEADS:
                return
            lsz2, sums = carry
            run = jnp.zeros((1, t), F32)
            blk = None
            for s in reversed(range(n_sub)):
                rows = slice(s * SB_SUBK, (s + 1) * SB_SUBK)
                w = jnp.exp2(lsz2[rows] + sums[s][0:SB_SUBK])
                if masked:
                    w = jnp.where(strict[rows], w, 0.0)
                part = _dot(vt_ref[0, h, :, pl.ds(k0 + s * SB_SUBK, SB_SUBK)], w.astype(BF16))
                blk = part if blk is None else blk + jnp.exp2(run) * part
                run = run + sums[s][SB_SUBK:SB_SUBK + 1]
            acc_ref[h] += jnp.exp2(c_ref[h]) * blk
            c_ref[h] += run

        _pipeline(SB_HEADS + 1, [logits, suffix_sums, weights_pv])

    step(qi, True)

    def body(i, c):
        step(qi - 1 - i, False)
        return c

    lax.fori_loop(0, qi, body, 0)
    outs = [acc_ref[h] for h in range(SB_HEADS)]
    outs.append(jnp.zeros((OUT_W - SB_HEADS * HEAD, t), F32))
    o_ref[0] = jnp.concatenate(outs, axis=0).T.astype(BF16)


def _sb_call(qt, k, vt, u):
    B, H, _, S = qt.shape
    res = lambda shape: pl.BlockSpec(shape, lambda b, i: (b, 0, 0, 0), pipeline_mode=pl.Buffered(1))
    return pl.pallas_call(
        _sb_kernel,
        grid=(B, S // SB_T),
        in_specs=[pl.BlockSpec((1, H, HEAD, SB_T), lambda b, i: (b, 0, 0, i)),
                  res((1, H, S, HEAD)), res((1, H, HEAD, S)), _const_spec(u.shape)],
        out_specs=pl.BlockSpec((1, SB_T, OUT_W), lambda b, i: (b, i, 0)),
        out_shape=jax.ShapeDtypeStruct((B, S, OUT_W), BF16),
        scratch_shapes=[pltpu.VMEM((H, 1, SB_T), F32), pltpu.VMEM((H, HEAD, SB_T), F32),
                        pltpu.VMEM((SB_T, SB_T), F32)],
        compiler_params=_params("parallel", "arbitrary"),
        name="stickbreak",
    )(qt, k, vt, u)


def _bias_kernel(rb_ref, o_ref):
    v = pl.program_id(0)
    shape = (DSA_TK, DSA_TQ)
    off = jnp.where(v < DSA_SUB, v * DSA_TQ, DSA_TK)
    dist = off + lax.broadcasted_iota(jnp.int32, shape, 1) - lax.broadcasted_iota(jnp.int32, shape, 0)
    max_exact = REL_BUCKETS // 2
    d = jnp.maximum(dist, 1).astype(F32)
    large = max_exact + (jnp.log(d / max_exact) / math.log(REL_MAX_DIST / max_exact)
                         * (REL_BUCKETS - max_exact)).astype(jnp.int32)
    large = jnp.minimum(large, REL_BUCKETS - 1)
    bucket = jnp.where(dist < max_exact, dist, large)
    for h in range(DSA_HEADS):
        t = jnp.zeros(shape, F32)
        for b in range(REL_BUCKETS):
            t = jnp.where(bucket == b, rb_ref[b, h], t)
        o_ref[0, h] = jnp.where(dist >= 0, t, NEG)


def _bias_call(rel_bias):
    nv = DSA_SUB + 1
    return pl.pallas_call(
        _bias_kernel,
        grid=(nv,),
        in_specs=[pl.BlockSpec(memory_space=pltpu.SMEM)],
        out_specs=pl.BlockSpec((1, DSA_HEADS, DSA_TK, DSA_TQ), lambda v: (v, 0, 0, 0)),
        out_shape=jax.ShapeDtypeStruct((nv, DSA_HEADS, DSA_TK, DSA_TQ), F32),
        compiler_params=_params("arbitrary"),
        name="bias_tiles",
    )(rel_bias)


def _dsa_kernel(far_ref, iqt_ref, ik_ref, iwt_ref, dqt_ref, dk_ref, dvt_ref, tab_ref, o_ref,
                keys_ref, m_ref, acc_ref, x_ref, *, topk):
    qi = pl.program_id(1)
    n_blk = qi // DSA_SUB + 1
    sub = qi % DSA_SUB
    tq, tk = DSA_TQ, DSA_TK

    iqt = jnp.concatenate([iqt_ref[0, h] for h in range(IDX_HEADS)], axis=1)
    iw = iwt_ref[0]

    def pair_scores(k_ref_, qt_all, g, j, n_heads):
        k0 = pl.multiple_of(j * tk, tk)
        cols = qt_all[:, 2 * g * tq:min(2 * (g + 1), n_heads) * tq]
        half = tk // 2
        return jnp.concatenate(
            [_dot(k_ref_[0, pl.ds(k0 + i * half, half), :], cols) for i in range(2)], axis=0)

    x_ref[...] = pair_scores(ik_ref, iqt, 0, 0, IDX_HEADS)

    def idx_step(j, masked):
        k0 = pl.multiple_of(j * tk, tk)
        n_pairs = IDX_HEADS // 2
        total = []

        def scores(g, _):
            if g == 0:
                return x_ref[...]
            if g == n_pairs:
                return pair_scores(ik_ref, iqt, 0, jnp.minimum(j + 1, n_blk - 1), IDX_HEADS)
            return pair_scores(ik_ref, iqt, g, j, IDX_HEADS)

        def weigh(g, x):
            if g == n_pairs:
                x_ref[...] = x
                return
            part = (jnp.maximum(x[:, 0:tq], 0.0) * iw[2 * g:2 * g + 1]
                    + jnp.maximum(x[:, tq:2 * tq], 0.0) * iw[2 * g + 1:2 * g + 2])
            total[:] = [part if not total else total[0] + part]

        _pipeline(n_pairs + 1, [scores, weigh])
        isc = total[0]
        isc = jnp.where(isc == 0.0, 0.0, isc)
        if masked:
            kpos = k0 + lax.broadcasted_iota(jnp.int32, isc.shape, 0)
            qpos = qi * tq + lax.broadcasted_iota(jnp.int32, isc.shape, 1)
            isc = jnp.where(kpos <= qpos, isc, -jnp.inf)
        bits = pltpu.bitcast(isc, jnp.int32)
        keys_ref[j] = bits ^ (lax.shift_right_arithmetic(bits, 31) & 0x7FFFFFFF)

    def idx_body(j, c):
        idx_step(j, False)
        return c

    lax.fori_loop(0, n_blk - 1, idx_body, 0)
    idx_step(n_blk - 1, True)

    @pl.when(n_blk % 2 == 1)
    def _():
        keys_ref[n_blk] = jnp.full((tk, tq), INT_MIN, jnp.int32)

    def count_ge(cand):
        def body(i, acc):
            for j in (2 * i, 2 * i + 1):
                c = jnp.where(keys_ref[j] >= cand, 1.0, 0.0)
                acc = acc + jnp.sum(c.reshape(COUNT_ACC, tk // COUNT_ACC, tq), axis=0)
            return acc
        acc = lax.fori_loop(0, (n_blk + 1) // 2, body, jnp.zeros((tk // COUNT_ACC, tq), F32))
        return jnp.sum(acc, axis=0, keepdims=True)

    def bit_body(i, state):
        t_u, done, thr_hit = state
        cand_u = t_u | lax.shift_left(jnp.int32(1), 31 - i)
        cand = cand_u ^ INT_MIN
        cnt = count_ge(cand)
        hit = cnt == topk
        thr_hit = jnp.where(hit, jnp.where(done > 0.0, thr_hit, cand), thr_hit)
        done = jnp.where(hit, 1.0, done)
        return jnp.where(cnt >= topk, cand_u, t_u), done, thr_hit

    zero = jnp.zeros((1, tq), jnp.int32)
    t_u, done, thr_hit = lax.fori_loop(0, 32, bit_body, (zero, jnp.zeros((1, tq), F32), zero))
    thr = jnp.where(done > 0.0, thr_hit, t_u ^ INT_MIN)

    tie = jnp.where(done > 0.0, 0.0, jnp.where(thr > NEG_INF_KEY, 1.0, 0.0))

    @pl.when(jnp.max(tie) > 0.0)
    def _():
        need = jnp.where(tie > 0.0, topk - count_ge(thr + 1), float(tk * keys_ref.shape[0] + 1))
        row = lax.broadcasted_iota(jnp.int32, (tk, tk), 0)
        col = lax.broadcasted_iota(jnp.int32, (tk, tk), 1)
        before = jnp.where(col < row, 1.0, 0.0).astype(BF16)

        def body(j, run):
            kb = keys_ref[j]
            eq = jnp.where(kb == thr, 1.0, 0.0)
            rank = _dot(before, eq.astype(BF16)) + run
            keys_ref[j] = jnp.where(kb == thr, jnp.where(rank >= need, thr - 1, kb), kb)
            return run + jnp.sum(eq, axis=0, keepdims=True)

        lax.fori_loop(0, n_blk, body, jnp.zeros((1, tq), F32))

    dqt = jnp.concatenate([dqt_ref[0, h] for h in range(DSA_HEADS)], axis=1)
    m_ref[...] = jnp.full(m_ref.shape, NEG, F32)
    acc_ref[...] = jnp.zeros(acc_ref.shape, F32)

    x_ref[...] = pair_scores(dk_ref, dqt, 0, 0, DSA_HEADS)

    def att_step(j, variant):
        k0 = pl.multiple_of(j * tk, tk)
        vt = dvt_ref[0, :, pl.ds(k0, tk)]
        sel = keys_ref[j] >= thr
        n_pairs = (DSA_HEADS + 1) // 2

        def scores(g, _):
            if g == 0:
                return x_ref[...]
            if g == n_pairs:
                return pair_scores(dk_ref, dqt, 0, jnp.minimum(j + 1, n_blk - 1), DSA_HEADS)
            return pair_scores(dk_ref, dqt, g, j, DSA_HEADS)

        def softmax_pv(g, s):
            if g == n_pairs:
                x_ref[...] = s
                return
            for h in range(2 * g, min(2 * (g + 1), DSA_HEADS)):
                bias = far_ref[h] if variant is None else tab_ref[variant, h]
                a = jnp.where(sel, s[:, (h - 2 * g) * tq:(h - 2 * g + 1) * tq] + bias, NEG)
                m_prev = m_ref[h]
                m_new = jnp.maximum(m_prev, jnp.max(a, axis=0, keepdims=True))
                p = jnp.exp(a - m_new)
                acc_ref[h] = jnp.exp(m_prev - m_new) * acc_ref[h] + _dot(vt, p.astype(BF16))
                m_ref[h] = m_new

        _pipeline(n_pairs + 1, [scores, softmax_pv])

    def far_body(j, c):
        att_step(j, None)
        return c

    lax.fori_loop(0, n_blk - 2, far_body, 0)

    @pl.when(jnp.logical_and(n_blk >= 2, sub == 0))
    def _():
        att_step(n_blk - 2, DSA_SUB)

    @pl.when(jnp.logical_and(n_blk >= 2, sub != 0))
    def _():
        att_step(n_blk - 2, None)

    att_step(n_blk - 1, sub)
    _finish(acc_ref, DSA_HEADS, o_ref)


def _dsa_call(far, iqt, ik, iwt, dqt, dk, dvt, tab, topk):
    B, _, _, S = iqt.shape
    tq = DSA_TQ
    qh = lambda n: pl.BlockSpec((1, n, HEAD, tq), lambda b, i: (b, 0, 0, i))
    res = lambda shape: pl.BlockSpec(shape, lambda b, i: (b, 0, 0), pipeline_mode=pl.Buffered(1))
    return pl.pallas_call(
        functools.partial(_dsa_kernel, topk=topk),
        grid=(B, S // tq),
        in_specs=[pl.BlockSpec(memory_space=pltpu.SMEM),
                  qh(IDX_HEADS), res((1, S, HEAD)),
                  pl.BlockSpec((1, IDX_HEADS, tq), lambda b, i: (b, 0, i)),
                  qh(DSA_HEADS), res((1, S, HEAD)), res((1, VROWS, S)), _const_spec(tab.shape)],
        out_specs=pl.BlockSpec((1, tq, OUT_W), lambda b, i: (b, i, 0)),
        out_shape=jax.ShapeDtypeStruct((B, S, OUT_W), BF16),
        scratch_shapes=[pltpu.VMEM((S // DSA_TK + (S // DSA_TK) % 2, DSA_TK, tq), jnp.int32),
                        pltpu.VMEM((DSA_HEADS, 1, tq), F32),
                        pltpu.VMEM((DSA_HEADS, VROWS, tq), F32),
                        pltpu.VMEM((DSA_TK, 2 * tq), F32)],
        compiler_params=_params("parallel", "arbitrary"),
        name="dsa",
    )(far, iqt, ik, iwt, dqt, dk, dvt, tab)


def _post_kernel(h_ref, om_ref, od_ref, os_ref, p_ref, wo_ref, gf_ref, wg_ref, wu_ref, wd_ref,
                 gp_ref, wpg_ref, wpp_ref, gfin_ref, out_ref, *, final):
    h = (h_ref[0] + _dot(om_ref[0], wo_ref[0]) + _dot(od_ref[0], wo_ref[1])
         + _dot(os_ref[0], wo_ref[2]))

    hf = _rms(h, gf_ref[...]).astype(BF16)
    d_ff = wg_ref.shape[1]
    ffn = jnp.zeros(h.shape, F32)
    for c0 in range(0, d_ff, FF_CHUNK):
        g = _dot(hf, wg_ref[:, c0:c0 + FF_CHUNK])
        u = _dot(hf, wu_ref[:, c0:c0 + FF_CHUNK])
        ffn = ffn + _dot((g * jax.nn.sigmoid(g) * u).astype(BF16), wd_ref[c0:c0 + FF_CHUNK, :])
    h = h + ffn

    gate = jax.nn.sigmoid(_dot(_rms(h, gp_ref[...]).astype(BF16), wpg_ref[...]))
    h = h + gate * _dot(p_ref[0, 0].astype(BF16), wpp_ref[...])
    if final:
        h = _rms(h, gfin_ref[...])
    out_ref[0] = h


def _post_call(h, om, od, os_, p, layer, wo, gf, wg, wu, wd, gp, wpg, wpp, gfin, final):
    B, S, D = h.shape
    mix = pl.BlockSpec((1, TM, OUT_W), lambda b, t: (b, t, 0))
    tok = pl.BlockSpec((1, TM, D), lambda b, t: (b, t, 0))
    return pl.pallas_call(
        functools.partial(_post_kernel, final=final),
        grid=(B, S // TM),
        in_specs=[tok, mix, mix, mix,
                  pl.BlockSpec((1, 1, TM, p.shape[-1]), lambda b, t: (layer, b, t, 0)),
                  _const_spec(wo.shape), _const_spec(gf.shape), _const_spec(wg.shape),
                  _const_spec(wu.shape), _const_spec(wd.shape), _const_spec(gp.shape),
                  _const_spec(wpg.shape), _const_spec(wpp.shape), _const_spec(gfin.shape)],
        out_specs=tok,
        out_shape=jax.ShapeDtypeStruct((B, S, D), F32),
        compiler_params=_params("parallel", "parallel"),
        name="post",
    )(h, om, od, os_, p, wo, gf, wg, wu, wd, gp, wpg, wpp, gfin)


def _swap_halves(w):
    half = w.shape[-1] // 2
    return jnp.concatenate([w[..., half:], w[..., :half]], axis=-1)


def _pack_w_in(w_in):
    L, D, _ = w_in.shape
    z = lambda n: jnp.zeros((L, D, n), F32)
    o = 0
    cols = {}
    for name, n in (("cq", MLA_Q_LORA), ("ckv", MLA_KV_LORA), ("kr", MLA_ROPE),
                    ("dq", DSA_HEADS * DSA_DIM), ("dk", DSA_DIM), ("dv", DSA_DIM),
                    ("iq", IDX_HEADS * IDX_DIM), ("ik", IDX_DIM), ("iw", IDX_HEADS),
                    ("sq", SB_HEADS * SB_DIM), ("sk", SB_HEADS * SB_DIM), ("sv", SB_HEADS * SB_DIM)):
        cols[name] = w_in[:, :, o:o + n]
        o += n
    pad_r = LANES - MLA_NOPE - MLA_ROPE
    std = jnp.concatenate([
        cols["cq"], cols["ckv"],
        z(MLA_NOPE), cols["kr"], z(pad_r),
        z(MLA_NOPE), _swap_halves(cols["kr"]), z(pad_r),
        cols["sk"], cols["ik"],
        cols["dk"], z(LANES - DSA_DIM)], axis=-1)
    tr = jnp.concatenate([
        cols["dq"] * DSA_DIM ** -0.5, cols["iq"] * IDX_DIM ** -0.5, cols["sq"] * SB_DIM ** -0.5,
        cols["sv"], cols["dv"], cols["iw"], z(ROW_IW[1] - ROW_IW[0] - IDX_HEADS)], axis=-1)
    assert std.shape[-1] == W_STD_COLS and tr.shape[-1] == W_T_ROWS
    return std.astype(BF16), jnp.swapaxes(tr, 1, 2).astype(BF16)


def _pack_mla(w_uq, w_ukv):
    L = w_uq.shape[0]
    dq = MLA_NOPE + MLA_ROPE
    pad_r = LANES - dq
    uq = w_uq.reshape(L, MLA_Q_LORA, MLA_HEADS, dq)
    zq = lambda n: jnp.zeros((L, MLA_Q_LORA, MLA_HEADS, n), F32)
    wqa = jnp.concatenate([uq, zq(pad_r)], axis=-1)
    wqb = jnp.concatenate([zq(MLA_NOPE), _swap_halves(uq[..., MLA_NOPE:]), zq(pad_r)], axis=-1)
    ukv = w_ukv.reshape(L, MLA_KV_LORA, MLA_HEADS, MLA_NOPE + MLA_V)
    wk = jnp.concatenate([ukv[..., :MLA_NOPE],
                          jnp.zeros((L, MLA_KV_LORA, MLA_HEADS, LANES - MLA_NOPE), F32)], axis=-1)
    wv = ukv[..., MLA_NOPE:]
    flat = lambda a: a.reshape(L, a.shape[1], -1)
    tr = lambda a: jnp.swapaxes(flat(a), 1, 2).astype(BF16)
    return tr(wqa), tr(wqb), flat(wk).astype(BF16), tr(wv)


def _rope_tables(S):
    half = MLA_ROPE // 2
    inv = ROPE_THETA ** (-jnp.arange(half, dtype=F32) / half)
    ang = jnp.arange(S, dtype=jnp.int32).astype(F32)[:, None] * inv[None, :]
    cos, sin = jnp.cos(ang), jnp.sin(ang)
    pad_r = LANES - MLA_NOPE - MLA_ROPE
    cos_t = jnp.concatenate([jnp.ones((S, MLA_NOPE), F32), cos, cos, jnp.zeros((S, pad_r), F32)], axis=-1)
    sin_t = jnp.concatenate([jnp.zeros((S, MLA_NOPE), F32), -sin, sin, jnp.zeros((S, pad_r), F32)], axis=-1)
    return cos_t, sin_t


def _pack_w_o(w_o):
    L, _, D = w_o.shape
    a = MLA_HEADS * MLA_V
    b = a + DSA_HEADS * DSA_DIM
    pad = jnp.zeros((L, OUT_W - DSA_HEADS * DSA_DIM, D), F32)
    blocks = [w_o[:, :a], jnp.concatenate([w_o[:, a:b], pad], axis=1),
              jnp.concatenate([w_o[:, b:], pad], axis=1)]
    return jnp.stack(blocks, axis=1).astype(BF16)


def kernel(x, p, w_in, attn_norm, mla_q_norm, mla_w_uq, mla_kv_norm, mla_w_ukv, rel_bias, w_o,
           ffn_norm, w_gate, w_up, w_down, ple_norm, w_ple_gate, w_ple_proj, final_norm):
    B, S, D = x.shape
    depth = w_in.shape[0]
    assert S % TM == 0 and S % DSA_TK == 0 and S % MLA_TK == 0
    assert w_gate.shape[-1] % FF_CHUNK == 0
    topk = min(DSA_TOPK, S // 4)

    w_std, w_tr = _pack_w_in(w_in)
    wqa, wqb, wk, wv = _pack_mla(mla_w_uq, mla_w_ukv)
    cos_s, sin_s = _rope_tables(S)
    cos_t, sin_t = cos_s.T, sin_s.T
    row = lambda g: g.reshape(depth, 1, -1)
    g_attn, g_q, g_kv, g_ffn, g_ple = map(row, (attn_norm, mla_q_norm, mla_kv_norm, ffn_norm, ple_norm))
    g_fin = final_norm.reshape(1, -1)
    wo = _pack_w_o(w_o)
    wg, wu, wd, wpg, wpp = (a.astype(BF16) for a in (w_gate, w_up, w_down, w_ple_gate, w_ple_proj))
    tri = jnp.triu(jnp.ones((SB_SUBK, SB_SUBK), F32), 1)
    tail = jnp.zeros((16, SB_SUBK), F32).at[0].set(1.0)
    u = jnp.concatenate([tri, tail], axis=0)
    u = jnp.concatenate([u, u], axis=1).astype(BF16)

    tab = _bias_call(rel_bias)
    far = rel_bias[REL_BUCKETS - 1]

    h = x
    for i in range(depth):
        (mqt, mk, mvt, dqt, dk, dvt, iqt, ik, iwt, sqt, sk, svt) = _proj_call(
            h, g_attn[i], w_std[i], w_tr[i], g_q[i], wqa[i], wqb[i], g_kv[i], wk[i], wv[i],
            cos_s, sin_s, cos_t, sin_t)
        o_mla = _mla_call(mqt, mk, mvt)
        o_dsa = _dsa_call(far, iqt, ik, iwt, dqt, dk, dvt, tab, topk)
        o_sb = _sb_call(sqt, sk, svt, u)
        h = _post_call(h, o_mla, o_dsa, o_sb, p, i, wo[i], g_ffn[i], wg[i], wu[i], wd[i],
                       g_ple[i], wpg[i], wpp[i], g_fin, final=(i == depth - 1))
    return h
```

```python
import functools
import math

import jax
import jax.numpy as jnp
from jax import lax
from jax.experimental import pallas as pl
from jax.experimental.pallas import tpu as pltpu

EPS = 1e-6
MLA_HEADS = 6
MLA_Q_LORA = 384
MLA_KV_LORA = 256
MLA_NOPE = 64
MLA_ROPE = 32
MLA_V = 64
ROPE_THETA = 10000.0
DSA_HEADS = 5
DSA_DIM = 64
IDX_HEADS = 8
IDX_DIM = 64
DSA_TOPK = 256
SB_HEADS = 5
SB_DIM = 64
REL_BUCKETS = 32
REL_MAX_DIST = 128

LANES = 128
HEAD = 64
VROWS = 80
OUT_W = 384
VMEM_LIMIT = 56 * 1024 * 1024

TM = 512
MLA_TQ, MLA_TK = 256, 512
SB_T = 256
SB_SUBK = 128
DSA_TQ, DSA_TK = 128, 512
DSA_SUB = DSA_TK // DSA_TQ
COUNT_ACC = 8
FF_CHUNK = 1408

NEG = -1e30
INT_MIN = -2 ** 31
NEG_INF_KEY = -2139095041
LOG2E = math.log2(math.e)

F32 = jnp.float32
BF16 = jnp.bfloat16

SEG_CQ = (0, 384)
SEG_CKV = (384, 640)
SEG_KRA = (640, 768)
SEG_KRB = (768, 896)
SEG_SK_IK = (896, 1280)
SEG_DK = (1280, 1408)
W_STD_COLS = 1408
ROW_DQ = (0, 320)
ROW_IQ = (320, 832)
ROW_SQ = (832, 1152)
ROW_SV = (1152, 1472)
ROW_DV = (1472, 1536)
ROW_IW = (1536, 1552)
W_T_ROWS = 1552


def _rms(x, g):
    return x * lax.rsqrt(jnp.mean(x * x, axis=-1, keepdims=True) + EPS) * g


def _dot(a, b):
    return jnp.dot(a, b, preferred_element_type=F32)


def _const_spec(shape):
    n = len(shape)
    return pl.BlockSpec(shape, lambda *_: (0,) * n, pipeline_mode=pl.Buffered(1))


def _params(*sem):
    return pltpu.CompilerParams(dimension_semantics=sem, vmem_limit_bytes=VMEM_LIMIT)


def _pipeline(n, stages):
    held = [dict() for _ in stages]
    for t in range(n + len(stages) - 1):
        for s, stage in enumerate(stages):
            i = t - s
            if 0 <= i < n:
                held[s][i] = stage(i, held[s - 1].pop(i) if s else None)


def _ones_rows(n):
    r = lax.broadcasted_iota(jnp.int32, (VROWS - HEAD, n), 0)
    return jnp.where(r == 0, 1.0, 0.0).astype(BF16)


def _proj_kernel(h_ref, g_ref, w_ref, wt_ref, gq_ref, wqa_ref, wqb_ref, gkv_ref, wk_ref, wv_ref,
                 cos_ref, sin_ref, cost_ref, sint_ref,
                 mqt_ref, mk_ref, mvt_ref, dqt_ref, dk_ref, dvt_ref, iqt_ref, ik_ref, iwt_ref,
                 sqt_ref, sk_ref, svt_ref):
    hn = _rms(h_ref[0], g_ref[...])
    hnt = hn.T.astype(BF16)
    hn = hn.astype(BF16)
    tm = hn.shape[0]
    ones = _ones_rows(tm)

    def seg(ab):
        return _dot(hn, w_ref[:, ab[0]:ab[1]])

    def seg_t(ab):
        return _dot(wt_ref[ab[0]:ab[1], :], hnt)

    cqn_t = _rms(seg(SEG_CQ), gq_ref[...]).T.astype(BF16)
    qa = _dot(wqa_ref[...], cqn_t)
    qb = _dot(wqb_ref[...], cqn_t)
    cos_t = cost_ref[...]
    sin_t = sint_ref[...]
    for h in range(MLA_HEADS):
        sl = slice(h * LANES, (h + 1) * LANES)
        mqt_ref[0, h] = (qa[sl] * cos_t + qb[sl] * sin_t).astype(BF16)

    ckvn = _rms(seg(SEG_CKV), gkv_ref[...])
    ckvn_t = ckvn.T.astype(BF16)
    ckvn = ckvn.astype(BF16)
    kn = _dot(ckvn, wk_ref[...])
    kr = seg(SEG_KRA) * cos_ref[...] + seg(SEG_KRB) * sin_ref[...]
    for h in range(MLA_HEADS):
        mk_ref[0, h] = (kn[:, h * LANES:(h + 1) * LANES] + kr).astype(BF16)
    vt = _dot(wv_ref[...], ckvn_t)
    for h in range(MLA_HEADS):
        mvt_ref[0, h, 0:HEAD, :] = vt[h * HEAD:(h + 1) * HEAD].astype(BF16)
        mvt_ref[0, h, HEAD:VROWS, :] = ones

    a = seg(SEG_SK_IK)
    for h in range(SB_HEADS):
        sk_ref[0, h] = a[:, h * HEAD:(h + 1) * HEAD].astype(BF16)
    ik_ref[0] = a[:, 320:384].astype(BF16)
    dk_ref[0] = seg(SEG_DK)[:, 0:HEAD].astype(BF16)

    a = seg_t(ROW_DQ)
    for h in range(DSA_HEADS):
        dqt_ref[0, h] = a[h * HEAD:(h + 1) * HEAD].astype(BF16)
    a = seg_t(ROW_IQ)
    for h in range(IDX_HEADS):
        iqt_ref[0, h] = a[h * HEAD:(h + 1) * HEAD].astype(BF16)
    a = seg_t(ROW_SQ)
    for h in range(SB_HEADS):
        sqt_ref[0, h] = a[h * HEAD:(h + 1) * HEAD].astype(BF16)
    a = seg_t(ROW_SV)
    for h in range(SB_HEADS):
        svt_ref[0, h] = a[h * HEAD:(h + 1) * HEAD].astype(BF16)
    dvt_ref[0, 0:HEAD, :] = seg_t(ROW_DV).astype(BF16)
    dvt_ref[0, HEAD:VROWS, :] = ones
    iwt_ref[0] = seg_t(ROW_IW)[0:IDX_HEADS] * IDX_HEADS ** -0.5


def _proj_call(h, g, w, wt, gq, wqa, wqb, gkv, wk, wv, cos_s, sin_s, cos_t, sin_t):
    B, S, D = h.shape
    nt = S // TM
    tok = lambda w_: pl.BlockSpec((1, TM, w_), lambda b, t: (b, t, 0))
    heads = lambda n, w_: pl.BlockSpec((1, n, TM, w_), lambda b, t: (b, 0, t, 0))
    heads_t = lambda n, r: pl.BlockSpec((1, n, r, TM), lambda b, t: (b, 0, 0, t))
    rows_t = lambda r: pl.BlockSpec((1, r, TM), lambda b, t: (b, 0, t))
    hs = lambda n, w_: jax.ShapeDtypeStruct((B, n, S, w_), BF16)
    hts = lambda n, r: jax.ShapeDtypeStruct((B, n, r, S), BF16)
    return pl.pallas_call(
        _proj_kernel,
        grid=(B, nt),
        in_specs=[tok(D), _const_spec(g.shape), _const_spec(w.shape), _const_spec(wt.shape),
                  _const_spec(gq.shape), _const_spec(wqa.shape), _const_spec(wqb.shape),
                  _const_spec(gkv.shape), _const_spec(wk.shape), _const_spec(wv.shape),
                  pl.BlockSpec((TM, LANES), lambda b, t: (t, 0)),
                  pl.BlockSpec((TM, LANES), lambda b, t: (t, 0)),
                  pl.BlockSpec((LANES, TM), lambda b, t: (0, t)),
                  pl.BlockSpec((LANES, TM), lambda b, t: (0, t))],
        out_specs=[heads_t(MLA_HEADS, LANES), heads(MLA_HEADS, LANES), heads_t(MLA_HEADS, VROWS),
                   heads_t(DSA_HEADS, HEAD), tok(HEAD), rows_t(VROWS),
                   heads_t(IDX_HEADS, HEAD), tok(HEAD), rows_t(IDX_HEADS),
                   heads_t(SB_HEADS, HEAD), heads(SB_HEADS, HEAD), heads_t(SB_HEADS, HEAD)],
        out_shape=[hts(MLA_HEADS, LANES), hs(MLA_HEADS, LANES), hts(MLA_HEADS, VROWS),
                   hts(DSA_HEADS, HEAD), jax.ShapeDtypeStruct((B, S, HEAD), BF16),
                   jax.ShapeDtypeStruct((B, VROWS, S), BF16),
                   hts(IDX_HEADS, HEAD), jax.ShapeDtypeStruct((B, S, HEAD), BF16),
                   jax.ShapeDtypeStruct((B, IDX_HEADS, S), F32),
                   hts(SB_HEADS, HEAD), hs(SB_HEADS, HEAD), hts(SB_HEADS, HEAD)],
        compiler_params=_params("parallel", "parallel"),
        name="proj",
    )(h, g, w, wt, gq, wqa, wqb, gkv, wk, wv, cos_s, sin_s, cos_t, sin_t)


def _finish(acc_ref, n_heads, o_ref):
    outs = [acc_ref[h, 0:HEAD, :] / acc_ref[h, HEAD:HEAD + 1, :] for h in range(n_heads)]
    pad = OUT_W - n_heads * HEAD
    if pad:
        outs.append(jnp.zeros((pad, outs[0].shape[1]), F32))
    o_ref[0] = jnp.concatenate(outs, axis=0).T.astype(BF16)


def _mla_kernel(qt_ref, k_ref, vt_ref, o_ref, m_ref, acc_ref, s_ref):
    qi = pl.program_id(1)
    tq, tk = MLA_TQ, MLA_TK
    m_ref[...] = jnp.full(m_ref.shape, NEG, F32)
    acc_ref[...] = jnp.zeros(acc_ref.shape, F32)
    scale2 = (MLA_NOPE + MLA_ROPE) ** -0.5 * LOG2E
    n_full = (qi * tq) // tk
    n_all = ((qi + 1) * tq + tk - 1) // tk

    def qk(h, j):
        k0 = pl.multiple_of(j * tk, tk)
        half = tk // 4
        return jnp.concatenate(
            [_dot(k_ref[0, h, pl.ds(k0 + i * half, half), :], qt_ref[0, h]) for i in range(4)],
            axis=0)

    s_ref[...] = qk(0, 0)

    def step(j, masked):
        k0 = pl.multiple_of(j * tk, tk)
        if masked:
            kpos = k0 + lax.broadcasted_iota(jnp.int32, (tk, tq), 0)
            qpos = qi * tq + lax.broadcasted_iota(jnp.int32, (tk, tq), 1)
            causal = kpos <= qpos

        def scores(h, _):
            if h == 0:
                return s_ref[...]
            if h == MLA_HEADS:
                return qk(0, jnp.minimum(j + 1, n_all - 1))
            return qk(h, j)

        def softmax_pv(h, s):
            if h == MLA_HEADS:
                s_ref[...] = s
                return
            s = s * scale2
            if masked:
                s = jnp.where(causal, s, NEG)
            m_prev = m_ref[h]
            m_new = jnp.maximum(m_prev, jnp.max(s, axis=0, keepdims=True))
            p = jnp.exp2(s - m_new)
            acc_ref[h] = (jnp.exp2(m_prev - m_new) * acc_ref[h]
                          + _dot(vt_ref[0, h, :, pl.ds(k0, tk)], p.astype(BF16)))
            m_ref[h] = m_new

        _pipeline(MLA_HEADS + 1, [scores, softmax_pv])

    def full_body(j, c):
        step(j, False)
        return c

    def diag_body(j, c):
        step(j, True)
        return c

    lax.fori_loop(0, n_full, full_body, 0)
    lax.fori_loop(n_full, n_all, diag_body, 0)
    _finish(acc_ref, MLA_HEADS, o_ref)


def _mla_call(qt, k, vt):
    B, H, _, S = qt.shape
    res = lambda shape: pl.BlockSpec(shape, lambda b, i: (b, 0, 0, 0), pipeline_mode=pl.Buffered(1))
    return pl.pallas_call(
        _mla_kernel,
        grid=(B, S // MLA_TQ),
        in_specs=[pl.BlockSpec((1, H, LANES, MLA_TQ), lambda b, i: (b, 0, 0, i)),
                  res((1, H, S, LANES)), res((1, H, VROWS, S))],
        out_specs=pl.BlockSpec((1, MLA_TQ, OUT_W), lambda b, i: (b, i, 0)),
        out_shape=jax.ShapeDtypeStruct((B, S, OUT_W), BF16),
        scratch_shapes=[pltpu.VMEM((H, 1, MLA_TQ), F32), pltpu.VMEM((H, VROWS, MLA_TQ), F32),
                        pltpu.VMEM((MLA_TK, MLA_TQ), F32)],
        compiler_params=_params("parallel", "arbitrary"),
        name="mla",
    )(qt, k, vt)


def _sb_kernel(qt_ref, k_ref, vt_ref, u_ref, o_ref, c_ref, acc_ref, z_ref):
    qi = pl.program_id(1)
    t = SB_T
    n_sub = t // SB_SUBK
    c_ref[...] = jnp.zeros(c_ref.shape, F32)
    acc_ref[...] = jnp.zeros(acc_ref.shape, F32)

    def qk(h, j):
        k0 = pl.multiple_of(j * t, t)
        return jnp.concatenate(
            [_dot(k_ref[0, h, pl.ds(k0 + s * SB_SUBK, SB_SUBK), :], qt_ref[0, h]) for s in range(n_sub)],
            axis=0)

    z_ref[...] = qk(0, qi)

    def step(j, masked):
        k0 = pl.multiple_of(j * t, t)
        if masked:
            strict = (lax.broadcasted_iota(jnp.int32, (t, t), 0)
                      < lax.broadcasted_iota(jnp.int32, (t, t), 1))

        def logits(h, _):
            if h == 0:
                return z_ref[...]
            if h == SB_HEADS:
                return qk(0, jnp.maximum(j - 1, 0))
            return qk(h, j)

        def suffix_sums(h, z):
            if h == SB_HEADS:
                z_ref[...] = z
                return None
            z2 = z * LOG2E
            neg_abs = pltpu.bitcast(pltpu.bitcast(z2, jnp.int32) | INT_MIN, F32)
            nl = jnp.maximum(z2, 0.0) + jnp.log2(1.0 + jnp.exp2(neg_abs))
            if masked:
                nl = jnp.where(strict, nl, 0.0)
            hi = nl.astype(BF16)
            lo = (nl - hi.astype(F32)).astype(BF16)
            sums = []
            for s in range(n_sub):
                rows = slice(s * SB_SUBK, (s + 1) * SB_SUBK)
                sums.append(_dot(u_ref[...], jnp.concatenate([hi[rows], lo[rows]], axis=0)))
            return z2, sums

        def weights_pv(h, carry):
            if h == SB_HEADS:
                return
            z2, sums = carry
            run = jnp.zeros((1, t), F32)
            blk = None
            for s in reversed(range(n_sub)):
                rows = slice(s * SB_SUBK, (s + 1) * SB_SUBK)
                w = jnp.exp2(z2[rows] - sums[s][0:SB_SUBK])
                if masked:
                    w = jnp.where(strict[rows], w, 0.0)
                part = _dot(vt_ref[0, h, :, pl.ds(k0 + s * SB_SUBK, SB_SUBK)], w.astype(BF16))
                blk = part if blk is None else blk + jnp.exp2(-run) * part
                run = run + sums[s][SB_SUBK:SB_SUBK + 1]
            acc_ref[h] += jnp.exp2(-c_ref[h]) * blk
            c_ref[h] += run

        _pipeline(SB_HEADS + 1, [logits, suffix_sums, weights_pv])

    step(qi, True)

    def body(i, c):
        step(qi - 1 - i, False)
        return c

    lax.fori_loop(0, qi, body, 0)
    outs = [acc_ref[h] for h in range(SB_HEADS)]
    outs.append(jnp.zeros((OUT_W - SB_HEADS * HEAD, t), F32))
    o_ref[0] = jnp.concatenate(outs, axis=0).T.astype(BF16)


def _sb_call(qt, k, vt, u):
    B, H, _, S = qt.shape
    res = lambda shape: pl.BlockSpec(shape, lambda b, i: (b, 0, 0, 0), pipeline_mode=pl.Buffered(1))
    return pl.pallas_call(
        _sb_kernel,
        grid=(B, S // SB_T),
        in_specs=[pl.BlockSpec((1, H, HEAD, SB_T), lambda b, i: (b, 0, 0, i)),
                  res((1, H, S, HEAD)), res((1, H, HEAD, S)), _const_spec(u.shape)],
        out_specs=pl.BlockSpec((1, SB_T, OUT_W), lambda b, i: (b, i, 0)),
        out_shape=jax.ShapeDtypeStruct((B, S, OUT_W), BF16),
        scratch_shapes=[pltpu.VMEM((H, 1, SB_T), F32), pltpu.VMEM((H, HEAD, SB_T), F32),
                        pltpu.VMEM((SB_T, SB_T), F32)],
        compiler_params=_params("parallel", "arbitrary"),
        name="stickbreak",
    )(qt, k, vt, u)


def _bias_kernel(rb_ref, o_ref):
    v = pl.program_id(0)
    shape = (DSA_TK, DSA_TQ)
    off = jnp.where(v < DSA_SUB, v * DSA_TQ, DSA_TK)
    dist = off + lax.broadcasted_iota(jnp.int32, shape, 1) - lax.broadcasted_iota(jnp.int32, shape, 0)
    max_exact = REL_BUCKETS // 2
    d = jnp.maximum(dist, 1).astype(F32)
    large = max_exact + (jnp.log(d / max_exact) / math.log(REL_MAX_DIST / max_exact)
                         * (REL_BUCKETS - max_exact)).astype(jnp.int32)
    large = jnp.minimum(large, REL_BUCKETS - 1)
    bucket = jnp.where(dist < max_exact, dist, large)
    for h in range(DSA_HEADS):
        t = jnp.zeros(shape, F32)
        for b in range(REL_BUCKETS):
            t = jnp.where(bucket == b, rb_ref[b, h] - rb_ref[REL_BUCKETS - 1, h], t)
        o_ref[0, h] = jnp.where(dist >= 0, t, NEG)


def _bias_call(rel_bias):
    nv = DSA_SUB + 1
    return pl.pallas_call(
        _bias_kernel,
        grid=(nv,),
        in_specs=[pl.BlockSpec(memory_space=pltpu.SMEM)],
        out_specs=pl.BlockSpec((1, DSA_HEADS, DSA_TK, DSA_TQ), lambda v: (v, 0, 0, 0)),
        out_shape=jax.ShapeDtypeStruct((nv, DSA_HEADS, DSA_TK, DSA_TQ), F32),
        compiler_params=_params("arbitrary"),
        name="bias_tiles",
    )(rel_bias)


def _dsa_kernel(iqt_ref, ik_ref, iwt_ref, dqt_ref, dk_ref, dvt_ref, tab_ref, o_ref,
                keys_ref, half_ref, m_ref, acc_ref, x_ref, *, topk):
    qi = pl.program_id(1)
    n_blk = qi // DSA_SUB + 1
    sub = qi % DSA_SUB
    tq, tk = DSA_TQ, DSA_TK

    iqt = jnp.concatenate([iqt_ref[0, h] for h in range(IDX_HEADS)], axis=1)
    iw = iwt_ref[0]

    def pair_scores(k_ref_, qt_all, g, j, n_heads):
        k0 = pl.multiple_of(j * tk, tk)
        cols = qt_all[:, 2 * g * tq:min(2 * (g + 1), n_heads) * tq]
        half = tk // 2
        return jnp.concatenate(
            [_dot(k_ref_[0, pl.ds(k0 + i * half, half), :], cols) for i in range(2)], axis=0)

    x_ref[...] = pair_scores(ik_ref, iqt, 0, 0, IDX_HEADS)

    def idx_step(j, masked):
        k0 = pl.multiple_of(j * tk, tk)
        n_pairs = IDX_HEADS // 2
        total = []

        def scores(g, _):
            if g == 0:
                return x_ref[...]
            if g == n_pairs:
                return pair_scores(ik_ref, iqt, 0, jnp.minimum(j + 1, n_blk - 1), IDX_HEADS)
            return pair_scores(ik_ref, iqt, g, j, IDX_HEADS)

        def weigh(g, x):
            if g == n_pairs:
                x_ref[...] = x
                return
            part = (jnp.maximum(x[:, 0:tq], 0.0) * iw[2 * g:2 * g + 1]
                    + jnp.maximum(x[:, tq:2 * tq], 0.0) * iw[2 * g + 1:2 * g + 2])
            total[:] = [part if not total else total[0] + part]

        _pipeline(n_pairs + 1, [scores, weigh])
        isc = total[0]
        isc = jnp.where(isc == 0.0, 0.0, isc)
        if masked:
            kpos = k0 + lax.broadcasted_iota(jnp.int32, isc.shape, 0)
            qpos = qi * tq + lax.broadcasted_iota(jnp.int32, isc.shape, 1)
            isc = jnp.where(kpos <= qpos, isc, -jnp.inf)
        bits = pltpu.bitcast(isc, jnp.int32)
        key = bits ^ (lax.shift_right_arithmetic(bits, 31) & 0x7FFFFFFF)
        keys_ref[j] = key
        half_ref[j] = lax.shift_right_arithmetic(key, 16).astype(jnp.int16)

    def idx_body(j, c):
        idx_step(j, False)
        return c

    lax.fori_loop(0, n_blk - 1, idx_body, 0)
    idx_step(n_blk - 1, True)

    n_trip = (n_blk + 1) // 2
    low16 = jnp.int16(-2 ** 15)

    @pl.when(n_blk % 2 == 1)
    def _():
        keys_ref[n_blk] = jnp.full((tk, tq), INT_MIN, jnp.int32)
        half_ref[n_blk] = jnp.full((tk, tq), low16, jnp.int16)

    def count_ge(cand):
        def body(i, acc):
            for j in (2 * i, 2 * i + 1):
                c = jnp.where(keys_ref[j] >= cand, 1.0, 0.0)
                acc = acc + jnp.sum(c.reshape(COUNT_ACC, tk // COUNT_ACC, tq), axis=0)
            return acc
        acc = lax.fori_loop(0, n_trip, body, jnp.zeros((tk // COUNT_ACC, tq), F32))
        return jnp.sum(acc, axis=0, keepdims=True)

    def count16(pred):
        rows = tk // COUNT_ACC
        def body(i, acc):
            for j in (2 * i, 2 * i + 1):
                c = jnp.where(pred(half_ref[j]), jnp.int16(1), jnp.int16(0))
                for r in range(COUNT_ACC):
                    acc = acc + c[r * rows:(r + 1) * rows]
            return acc
        acc = lax.fori_loop(0, n_trip, body, jnp.zeros((rows, tq), jnp.int16))
        return jnp.sum(acc.astype(jnp.int32), axis=0, keepdims=True)

    def to16(u):
        return (u ^ 0x8000).astype(jnp.int16)

    zero = jnp.zeros((1, tq), jnp.int32)

    def descend(base):
        def body(i, state):
            u, done, hit_u = state
            cand_u = u | lax.shift_left(jnp.int32(1), 15 - i)
            cand = to16(cand_u)
            cnt = base + count16(lambda blk: blk >= cand)
            hit = cnt == topk
            hit_u = jnp.where(hit, jnp.where(done > 0, hit_u, cand_u), hit_u)
            done = jnp.where(hit, 1, done)
            return jnp.where(cnt >= topk, cand_u, u), done, hit_u
        return lax.fori_loop(0, 16, body, (zero, zero, zero))

    hi_u, hi_done, hi_hit = descend(zero)
    hi = to16(hi_u)
    above = count16(lambda blk: blk > hi)

    def low_body(j, c):
        half_ref[j] = jnp.where(half_ref[j] == hi, (keys_ref[j] ^ 0x8000).astype(jnp.int16), low16)
        return c

    lax.fori_loop(0, 2 * n_trip, low_body, 0)
    lo_u, lo_done, lo_hit = descend(above)

    hi_word = lax.shift_left(hi_u - 2 ** 15, 16)
    thr = jnp.where(hi_done > 0, lax.shift_left(hi_hit - 2 ** 15, 16),
                    hi_word | jnp.where(lo_done > 0, lo_hit, lo_u))

    tie = jnp.where(hi_done + lo_done > 0, 0.0, jnp.where(thr > NEG_INF_KEY, 1.0, 0.0))

    @pl.when(jnp.max(tie) > 0.0)
    def _():
        need = jnp.where(tie > 0.0, topk - count_ge(thr + 1), float(tk * keys_ref.shape[0] + 1))
        row = lax.broadcasted_iota(jnp.int32, (tk, tk), 0)
        col = lax.broadcasted_iota(jnp.int32, (tk, tk), 1)
        before = jnp.where(col < row, 1.0, 0.0).astype(BF16)

        def body(j, run):
            kb = keys_ref[j]
            eq = jnp.where(kb == thr, 1.0, 0.0)
            rank = _dot(before, eq.astype(BF16)) + run
            keys_ref[j] = jnp.where(kb == thr, jnp.where(rank >= need, thr - 1, kb), kb)
            return run + jnp.sum(eq, axis=0, keepdims=True)

        lax.fori_loop(0, n_blk, body, jnp.zeros((1, tq), F32))

    dqt = jnp.concatenate([dqt_ref[0, h] for h in range(DSA_HEADS)], axis=1)
    m_ref[...] = jnp.full(m_ref.shape, NEG, F32)
    acc_ref[...] = jnp.zeros(acc_ref.shape, F32)

    x_ref[...] = pair_scores(dk_ref, dqt, 0, 0, DSA_HEADS)

    def att_step(j, variant):
        k0 = pl.multiple_of(j * tk, tk)
        vt = dvt_ref[0, :, pl.ds(k0, tk)]
        sel = keys_ref[j] >= thr
        n_pairs = (DSA_HEADS + 1) // 2

        def scores(g, _):
            if g == 0:
                return x_ref[...]
            if g == n_pairs:
                return pair_scores(dk_ref, dqt, 0, jnp.minimum(j + 1, n_blk - 1), DSA_HEADS)
            return pair_scores(dk_ref, dqt, g, j, DSA_HEADS)

        def softmax_pv(g, s):
            if g == n_pairs:
                x_ref[...] = s
                return
            for h in range(2 * g, min(2 * (g + 1), DSA_HEADS)):
                a = s[:, (h - 2 * g) * tq:(h - 2 * g + 1) * tq]
                if variant is not None:
                    a = a + tab_ref[variant, h]
                a = jnp.where(sel, a, NEG)
                m_prev = m_ref[h]
                m_new = jnp.maximum(m_prev, jnp.max(a, axis=0, keepdims=True))
                p = jnp.exp(a - m_new)
                acc_ref[h] = jnp.exp(m_prev - m_new) * acc_ref[h] + _dot(vt, p.astype(BF16))
                m_ref[h] = m_new

        _pipeline(n_pairs + 1, [scores, softmax_pv])

    def far_body(j, c):
        att_step(j, None)
        return c

    lax.fori_loop(0, n_blk - 2, far_body, 0)

    @pl.when(jnp.logical_and(n_blk >= 2, sub == 0))
    def _():
        att_step(n_blk - 2, DSA_SUB)

    @pl.when(jnp.logical_and(n_blk >= 2, sub != 0))
    def _():
        att_step(n_blk - 2, None)

    att_step(n_blk - 1, sub)
    _finish(acc_ref, DSA_HEADS, o_ref)


def _dsa_call(iqt, ik, iwt, dqt, dk, dvt, tab, topk):
    B, _, _, S = iqt.shape
    tq = DSA_TQ
    qh = lambda n: pl.BlockSpec((1, n, HEAD, tq), lambda b, i: (b, 0, 0, i))
    res = lambda shape: pl.BlockSpec(shape, lambda b, i: (b, 0, 0), pipeline_mode=pl.Buffered(1))
    return pl.pallas_call(
        functools.partial(_dsa_kernel, topk=topk),
        grid=(B, S // tq),
        in_specs=[qh(IDX_HEADS), res((1, S, HEAD)),
                  pl.BlockSpec((1, IDX_HEADS, tq), lambda b, i: (b, 0, i)),
                  qh(DSA_HEADS), res((1, S, HEAD)), res((1, VROWS, S)), _const_spec(tab.shape)],
        out_specs=pl.BlockSpec((1, tq, OUT_W), lambda b, i: (b, i, 0)),
        out_shape=jax.ShapeDtypeStruct((B, S, OUT_W), BF16),
        scratch_shapes=[pltpu.VMEM((S // DSA_TK + (S // DSA_TK) % 2, DSA_TK, tq), jnp.int32),
                        pltpu.VMEM((S // DSA_TK + (S // DSA_TK) % 2, DSA_TK, tq), jnp.int16),
                        pltpu.VMEM((DSA_HEADS, 1, tq), F32),
                        pltpu.VMEM((DSA_HEADS, VROWS, tq), F32),
                        pltpu.VMEM((DSA_TK, 2 * tq), F32)],
        compiler_params=_params("parallel", "arbitrary"),
        name="dsa",
    )(iqt, ik, iwt, dqt, dk, dvt, tab)


def _post_kernel(h_ref, om_ref, od_ref, os_ref, p_ref, wo_ref, gf_ref, wg_ref, wu_ref, wd_ref,
                 gp_ref, wpg_ref, wpp_ref, gfin_ref, out_ref, *, final):
    h = (h_ref[0] + _dot(om_ref[0], wo_ref[0]) + _dot(od_ref[0], wo_ref[1])
         + _dot(os_ref[0], wo_ref[2]))

    hf = _rms(h, gf_ref[...]).astype(BF16)
    d_ff = wg_ref.shape[1]
    ffn = jnp.zeros(h.shape, F32)
    for c0 in range(0, d_ff, FF_CHUNK):
        g = _dot(hf, wg_ref[:, c0:c0 + FF_CHUNK])
        u = _dot(hf, wu_ref[:, c0:c0 + FF_CHUNK])
        ffn = ffn + _dot((g * jax.nn.sigmoid(g) * u).astype(BF16), wd_ref[c0:c0 + FF_CHUNK, :])
    h = h + ffn

    gate = jax.nn.sigmoid(_dot(_rms(h, gp_ref[...]).astype(BF16), wpg_ref[...]))
    h = h + gate * _dot(p_ref[0, 0].astype(BF16), wpp_ref[...])
    if final:
        h = _rms(h, gfin_ref[...])
    out_ref[0] = h


def _post_call(h, om, od, os_, p, layer, wo, gf, wg, wu, wd, gp, wpg, wpp, gfin, final):
    B, S, D = h.shape
    mix = pl.BlockSpec((1, TM, OUT_W), lambda b, t: (b, t, 0))
    tok = pl.BlockSpec((1, TM, D), lambda b, t: (b, t, 0))
    return pl.pallas_call(
        functools.partial(_post_kernel, final=final),
        grid=(B, S // TM),
        in_specs=[tok, mix, mix, mix,
                  pl.BlockSpec((1, 1, TM, p.shape[-1]), lambda b, t: (layer, b, t, 0)),
                  _const_spec(wo.shape), _const_spec(gf.shape), _const_spec(wg.shape),
                  _const_spec(wu.shape), _const_spec(wd.shape), _const_spec(gp.shape),
                  _const_spec(wpg.shape), _const_spec(wpp.shape), _const_spec(gfin.shape)],
        out_specs=tok,
        out_shape=jax.ShapeDtypeStruct((B, S, D), F32),
        compiler_params=_params("parallel", "parallel"),
        name="post",
    )(h, om, od, os_, p, wo, gf, wg, wu, wd, gp, wpg, wpp, gfin)


def _swap_halves(w):
    half = w.shape[-1] // 2
    return jnp.concatenate([w[..., half:], w[..., :half]], axis=-1)


def _pack_w_in(w_in):
    L, D, _ = w_in.shape
    z = lambda n: jnp.zeros((L, D, n), F32)
    o = 0
    cols = {}
    for name, n in (("cq", MLA_Q_LORA), ("ckv", MLA_KV_LORA), ("kr", MLA_ROPE),
                    ("dq", DSA_HEADS * DSA_DIM), ("dk", DSA_DIM), ("dv", DSA_DIM),
                    ("iq", IDX_HEADS * IDX_DIM), ("ik", IDX_DIM), ("iw", IDX_HEADS),
                    ("sq", SB_HEADS * SB_DIM), ("sk", SB_HEADS * SB_DIM), ("sv", SB_HEADS * SB_DIM)):
        cols[name] = w_in[:, :, o:o + n]
        o += n
    pad_r = LANES - MLA_NOPE - MLA_ROPE
    std = jnp.concatenate([
        cols["cq"], cols["ckv"],
        z(MLA_NOPE), cols["kr"], z(pad_r),
        z(MLA_NOPE), _swap_halves(cols["kr"]), z(pad_r),
        cols["sk"], cols["ik"],
        cols["dk"], z(LANES - DSA_DIM)], axis=-1)
    tr = jnp.concatenate([
        cols["dq"] * DSA_DIM ** -0.5, cols["iq"] * IDX_DIM ** -0.5, cols["sq"] * SB_DIM ** -0.5,
        cols["sv"], cols["dv"], cols["iw"], z(ROW_IW[1] - ROW_IW[0] - IDX_HEADS)], axis=-1)
    assert std.shape[-1] == W_STD_COLS and tr.shape[-1] == W_T_ROWS
    return std.astype(BF16), jnp.swapaxes(tr, 1, 2).astype(BF16)


def _pack_mla(w_uq, w_ukv):
    L = w_uq.shape[0]
    dq = MLA_NOPE + MLA_ROPE
    pad_r = LANES - dq
    uq = w_uq.reshape(L, MLA_Q_LORA, MLA_HEADS, dq)
    zq = lambda n: jnp.zeros((L, MLA_Q_LORA, MLA_HEADS, n), F32)
    wqa = jnp.concatenate([uq, zq(pad_r)], axis=-1)
    wqb = jnp.concatenate([zq(MLA_NOPE), _swap_halves(uq[..., MLA_NOPE:]), zq(pad_r)], axis=-1)
    ukv = w_ukv.reshape(L, MLA_KV_LORA, MLA_HEADS, MLA_NOPE + MLA_V)
    wk = jnp.concatenate([ukv[..., :MLA_NOPE],
                          jnp.zeros((L, MLA_KV_LORA, MLA_HEADS, LANES - MLA_NOPE), F32)], axis=-1)
    wv = ukv[..., MLA_NOPE:]
    flat = lambda a: a.reshape(L, a.shape[1], -1)
    tr = lambda a: jnp.swapaxes(flat(a), 1, 2).astype(BF16)
    return tr(wqa), tr(wqb), flat(wk).astype(BF16), tr(wv)


def _rope_tables(S):
    half = MLA_ROPE // 2
    inv = ROPE_THETA ** (-jnp.arange(half, dtype=F32) / half)
    ang = jnp.arange(S, dtype=jnp.int32).astype(F32)[:, None] * inv[None, :]
    cos, sin = jnp.cos(ang), jnp.sin(ang)
    pad_r = LANES - MLA_NOPE - MLA_ROPE
    cos_t = jnp.concatenate([jnp.ones((S, MLA_NOPE), F32), cos, cos, jnp.zeros((S, pad_r), F32)], axis=-1)
    sin_t = jnp.concatenate([jnp.zeros((S, MLA_NOPE), F32), -sin, sin, jnp.zeros((S, pad_r), F32)], axis=-1)
    return cos_t, sin_t


def _pack_w_o(w_o):
    L, _, D = w_o.shape
    a = MLA_HEADS * MLA_V
    b = a + DSA_HEADS * DSA_DIM
    pad = jnp.zeros((L, OUT_W - DSA_HEADS * DSA_DIM, D), F32)
    blocks = [w_o[:, :a], jnp.concatenate([w_o[:, a:b], pad], axis=1),
              jnp.concatenate([w_o[:, b:], pad], axis=1)]
    return jnp.stack(blocks, axis=1).astype(BF16)


def kernel(x, p, w_in, attn_norm, mla_q_norm, mla_w_uq, mla_kv_norm, mla_w_ukv, rel_bias, w_o,
           ffn_norm, w_gate, w_up, w_down, ple_norm, w_ple_gate, w_ple_proj, final_norm):
    B, S, D = x.shape
    depth = w_in.shape[0]
    assert S % TM == 0 and S % DSA_TK == 0 and S % MLA_TK == 0
    assert w_gate.shape[-1] % FF_CHUNK == 0
    topk = min(DSA_TOPK, S // 4)

    w_std, w_tr = _pack_w_in(w_in)
    wqa, wqb, wk, wv = _pack_mla(mla_w_uq, mla_w_ukv)
    cos_s, sin_s = _rope_tables(S)
    cos_t, sin_t = cos_s.T, sin_s.T
    row = lambda g: g.reshape(depth, 1, -1)
    g_attn, g_q, g_kv, g_ffn, g_ple = map(row, (attn_norm, mla_q_norm, mla_kv_norm, ffn_norm, ple_norm))
    g_fin = final_norm.reshape(1, -1)
    wo = _pack_w_o(w_o)
    wg, wu, wd, wpg, wpp = (a.astype(BF16) for a in (w_gate, w_up, w_down, w_ple_gate, w_ple_proj))
    tri = jnp.triu(jnp.ones((SB_SUBK, SB_SUBK), F32), 0)
    tail = jnp.zeros((16, SB_SUBK), F32).at[0].set(1.0)
    u = jnp.concatenate([tri, tail], axis=0)
    u = jnp.concatenate([u, u], axis=1).astype(BF16)

    tab = _bias_call(rel_bias)

    h = x
    for i in range(depth):
        (mqt, mk, mvt, dqt, dk, dvt, iqt, ik, iwt, sqt, sk, svt) = _proj_call(
            h, g_attn[i], w_std[i], w_tr[i], g_q[i], wqa[i], wqb[i], g_kv[i], wk[i], wv[i],
            cos_s, sin_s, cos_t, sin_t)
        o_mla = _mla_call(mqt, mk, mvt)
        o_dsa = _dsa_call(iqt, ik, iwt, dqt, dk, dvt, tab, topk)
        o_sb = _sb_call(sqt, sk, svt, u)
        h = _post_call(h, o_mla, o_dsa, o_sb, p, i, wo[i], g_ffn[i], wg[i], wu[i], wd[i],
                       g_ple[i], wpg[i], wpp[i], g_fin, final=(i == depth - 1))
    return h
```

```python
import functools
import math

import jax
import jax.numpy as jnp
from jax import lax
from jax.experimental import pallas as pl
from jax.experimental.pallas import tpu as pltpu

EPS = 1e-6
MLA_HEADS = 6
MLA_Q_LORA = 384
MLA_KV_LORA = 256
MLA_NOPE = 64
MLA_ROPE = 32
MLA_V = 64
ROPE_THETA = 10000.0
DSA_HEADS = 5
DSA_DIM = 64
IDX_HEADS = 8
IDX_DIM = 64
DSA_TOPK = 256
SB_HEADS = 5
SB_DIM = 64
REL_BUCKETS = 32
REL_MAX_DIST = 128

LANES = 128
HEAD = 64
VROWS = 80
OUT_W = 384
VMEM_LIMIT = 56 * 1024 * 1024

TM = 512
MLA_TQ, MLA_TK = 256, 512
SB_T = 256
SB_SUBK = 128
DSA_TQ, DSA_TK = 128, 512
DSA_SUB = DSA_TK // DSA_TQ
COUNT_ACC = 8
DESCENT_CHECKS = (24, 28)
FF_CHUNK = 1408

NEG = -1e30
INT_MIN = -2 ** 31
NEG_INF_KEY = -2139095041
LOG2E = math.log2(math.e)

F32 = jnp.float32
BF16 = jnp.bfloat16

SEG_CQ = (0, 384)
SEG_CKV = (384, 640)
SEG_KRA = (640, 768)
SEG_KRB = (768, 896)
SEG_SK_IK = (896, 1280)
SEG_DK = (1280, 1408)
W_STD_COLS = 1408
ROW_DQ = (0, 320)
ROW_IQ = (320, 832)
ROW_SQ = (832, 1152)
ROW_SV = (1152, 1472)
ROW_DV = (1472, 1536)
ROW_IW = (1536, 1552)
W_T_ROWS = 1552


def _rms(x, g):
    return x * lax.rsqrt(jnp.mean(x * x, axis=-1, keepdims=True) + EPS) * g


def _dot(a, b):
    return jnp.dot(a, b, preferred_element_type=F32)


def _const_spec(shape):
    n = len(shape)
    return pl.BlockSpec(shape, lambda *_: (0,) * n, pipeline_mode=pl.Buffered(1))


def _params(*sem):
    return pltpu.CompilerParams(dimension_semantics=sem, vmem_limit_bytes=VMEM_LIMIT)


def _pipeline(n, stages):
    held = [dict() for _ in stages]
    for t in range(n + len(stages) - 1):
        for s, stage in enumerate(stages):
            i = t - s
            if 0 <= i < n:
                held[s][i] = stage(i, held[s - 1].pop(i) if s else None)


def _ones_rows(n):
    r = lax.broadcasted_iota(jnp.int32, (VROWS - HEAD, n), 0)
    return jnp.where(r == 0, 1.0, 0.0).astype(BF16)


def _proj_kernel(h_ref, g_ref, w_ref, wt_ref, gq_ref, wqa_ref, wqb_ref, gkv_ref, wk_ref, wv_ref,
                 cos_ref, sin_ref, cost_ref, sint_ref,
                 mqt_ref, mk_ref, mvt_ref, dqt_ref, dk_ref, dvt_ref, iqt_ref, ik_ref, iwt_ref,
                 sqt_ref, sk_ref, svt_ref):
    hn = _rms(h_ref[0], g_ref[...])
    hnt = hn.T.astype(BF16)
    hn = hn.astype(BF16)
    tm = hn.shape[0]
    ones = _ones_rows(tm)

    def seg(ab):
        return _dot(hn, w_ref[:, ab[0]:ab[1]])

    def seg_t(ab):
        return _dot(wt_ref[ab[0]:ab[1], :], hnt)

    cqn_t = _rms(seg(SEG_CQ), gq_ref[...]).T.astype(BF16)
    qa = _dot(wqa_ref[...], cqn_t)
    qb = _dot(wqb_ref[...], cqn_t)
    cos_t = cost_ref[...]
    sin_t = sint_ref[...]
    for h in range(MLA_HEADS):
        sl = slice(h * LANES, (h + 1) * LANES)
        mqt_ref[0, h] = (qa[sl] * cos_t + qb[sl] * sin_t).astype(BF16)

    ckvn = _rms(seg(SEG_CKV), gkv_ref[...])
    ckvn_t = ckvn.T.astype(BF16)
    ckvn = ckvn.astype(BF16)
    kn = _dot(ckvn, wk_ref[...])
    kr = seg(SEG_KRA) * cos_ref[...] + seg(SEG_KRB) * sin_ref[...]
    for h in range(MLA_HEADS):
        mk_ref[0, h] = (kn[:, h * LANES:(h + 1) * LANES] + kr).astype(BF16)
    vt = _dot(wv_ref[...], ckvn_t)
    for h in range(MLA_HEADS):
        mvt_ref[0, h, 0:HEAD, :] = vt[h * HEAD:(h + 1) * HEAD].astype(BF16)
        mvt_ref[0, h, HEAD:VROWS, :] = ones

    a = seg(SEG_SK_IK)
    for h in range(SB_HEADS):
        sk_ref[0, h] = a[:, h * HEAD:(h + 1) * HEAD].astype(BF16)
    ik_ref[0] = a[:, 320:384].astype(BF16)
    dk_ref[0] = seg(SEG_DK)[:, 0:HEAD].astype(BF16)

    a = seg_t(ROW_DQ)
    for h in range(DSA_HEADS):
        dqt_ref[0, h] = a[h * HEAD:(h + 1) * HEAD].astype(BF16)
    a = seg_t(ROW_IQ)
    for h in range(IDX_HEADS):
        iqt_ref[0, h] = a[h * HEAD:(h + 1) * HEAD].astype(BF16)
    a = seg_t(ROW_SQ)
    for h in range(SB_HEADS):
        sqt_ref[0, h] = a[h * HEAD:(h + 1) * HEAD].astype(BF16)
    a = seg_t(ROW_SV)
    for h in range(SB_HEADS):
        svt_ref[0, h] = a[h * HEAD:(h + 1) * HEAD].astype(BF16)
    dvt_ref[0, 0:HEAD, :] = seg_t(ROW_DV).astype(BF16)
    dvt_ref[0, HEAD:VROWS, :] = ones
    iwt_ref[0] = seg_t(ROW_IW)[0:IDX_HEADS] * IDX_HEADS ** -0.5


def _proj_call(h, g, w, wt, gq, wqa, wqb, gkv, wk, wv, cos_s, sin_s, cos_t, sin_t):
    B, S, D = h.shape
    nt = S // TM
    tok = lambda w_: pl.BlockSpec((1, TM, w_), lambda b, t: (b, t, 0))
    heads = lambda n, w_: pl.BlockSpec((1, n, TM, w_), lambda b, t: (b, 0, t, 0))
    heads_t = lambda n, r: pl.BlockSpec((1, n, r, TM), lambda b, t: (b, 0, 0, t))
    rows_t = lambda r: pl.BlockSpec((1, r, TM), lambda b, t: (b, 0, t))
    hs = lambda n, w_: jax.ShapeDtypeStruct((B, n, S, w_), BF16)
    hts = lambda n, r: jax.ShapeDtypeStruct((B, n, r, S), BF16)
    return pl.pallas_call(
        _proj_kernel,
        grid=(B, nt),
        in_specs=[tok(D), _const_spec(g.shape), _const_spec(w.shape), _const_spec(wt.shape),
                  _const_spec(gq.shape), _const_spec(wqa.shape), _const_spec(wqb.shape),
                  _const_spec(gkv.shape), _const_spec(wk.shape), _const_spec(wv.shape),
                  pl.BlockSpec((TM, LANES), lambda b, t: (t, 0)),
                  pl.BlockSpec((TM, LANES), lambda b, t: (t, 0)),
                  pl.BlockSpec((LANES, TM), lambda b, t: (0, t)),
                  pl.BlockSpec((LANES, TM), lambda b, t: (0, t))],
        out_specs=[heads_t(MLA_HEADS, LANES), heads(MLA_HEADS, LANES), heads_t(MLA_HEADS, VROWS),
                   heads_t(DSA_HEADS, HEAD), tok(HEAD), rows_t(VROWS),
                   heads_t(IDX_HEADS, HEAD), tok(HEAD), rows_t(IDX_HEADS),
                   heads_t(SB_HEADS, HEAD), heads(SB_HEADS, HEAD), heads_t(SB_HEADS, HEAD)],
        out_shape=[hts(MLA_HEADS, LANES), hs(MLA_HEADS, LANES), hts(MLA_HEADS, VROWS),
                   hts(DSA_HEADS, HEAD), jax.ShapeDtypeStruct((B, S, HEAD), BF16),
                   jax.ShapeDtypeStruct((B, VROWS, S), BF16),
                   hts(IDX_HEADS, HEAD), jax.ShapeDtypeStruct((B, S, HEAD), BF16),
                   jax.ShapeDtypeStruct((B, IDX_HEADS, S), F32),
                   hts(SB_HEADS, HEAD), hs(SB_HEADS, HEAD), hts(SB_HEADS, HEAD)],
        compiler_params=_params("parallel", "parallel"),
        name="proj",
    )(h, g, w, wt, gq, wqa, wqb, gkv, wk, wv, cos_s, sin_s, cos_t, sin_t)


def _finish(acc_ref, n_heads, o_ref):
    outs = [acc_ref[h, 0:HEAD, :] / acc_ref[h, HEAD:HEAD + 1, :] for h in range(n_heads)]
    pad = OUT_W - n_heads * HEAD
    if pad:
        outs.append(jnp.zeros((pad, outs[0].shape[1]), F32))
    o_ref[0] = jnp.concatenate(outs, axis=0).T.astype(BF16)


def _mla_kernel(qt_ref, k_ref, vt_ref, o_ref, m_ref, acc_ref, s_ref):
    qi = pl.program_id(1)
    tq, tk = MLA_TQ, MLA_TK
    m_ref[...] = jnp.full(m_ref.shape, NEG, F32)
    acc_ref[...] = jnp.zeros(acc_ref.shape, F32)
    scale2 = (MLA_NOPE + MLA_ROPE) ** -0.5 * LOG2E
    n_full = (qi * tq) // tk
    n_all = ((qi + 1) * tq + tk - 1) // tk

    def qk(h, j):
        k0 = pl.multiple_of(j * tk, tk)
        half = tk // 4
        return jnp.concatenate(
            [_dot(k_ref[0, h, pl.ds(k0 + i * half, half), :], qt_ref[0, h]) for i in range(4)],
            axis=0)

    s_ref[...] = qk(0, 0)

    def step(j, masked):
        k0 = pl.multiple_of(j * tk, tk)
        if masked:
            kpos = k0 + lax.broadcasted_iota(jnp.int32, (tk, tq), 0)
            qpos = qi * tq + lax.broadcasted_iota(jnp.int32, (tk, tq), 1)
            causal = kpos <= qpos

        def scores(h, _):
            if h == 0:
                return s_ref[...]
            if h == MLA_HEADS:
                return qk(0, jnp.minimum(j + 1, n_all - 1))
            return qk(h, j)

        def running_max(h, s):
            if h == MLA_HEADS:
                s_ref[...] = s
                return None
            s = s * scale2
            if masked:
                s = jnp.where(causal, s, NEG)
            m_prev = m_ref[h]
            m_new = jnp.maximum(m_prev, jnp.max(s, axis=0, keepdims=True))
            m_ref[h] = m_new
            return s, m_prev, m_new

        def weights_pv(h, carry):
            if h == MLA_HEADS:
                return
            s, m_prev, m_new = carry
            p = jnp.exp2(s - m_new)
            acc_ref[h] = (jnp.exp2(m_prev - m_new) * acc_ref[h]
                          + _dot(vt_ref[0, h, :, pl.ds(k0, tk)], p.astype(BF16)))

        _pipeline(MLA_HEADS + 1, [scores, running_max, weights_pv])

    def full_body(j, c):
        step(j, False)
        return c

    def diag_body(j, c):
        step(j, True)
        return c

    lax.fori_loop(0, n_full, full_body, 0)
    lax.fori_loop(n_full, n_all, diag_body, 0)
    _finish(acc_ref, MLA_HEADS, o_ref)


def _mla_call(qt, k, vt):
    B, H, _, S = qt.shape
    res = lambda shape: pl.BlockSpec(shape, lambda b, i: (b, 0, 0, 0), pipeline_mode=pl.Buffered(1))
    return pl.pallas_call(
        _mla_kernel,
        grid=(B, S // MLA_TQ),
        in_specs=[pl.BlockSpec((1, H, LANES, MLA_TQ), lambda b, i: (b, 0, 0, i)),
                  res((1, H, S, LANES)), res((1, H, VROWS, S))],
        out_specs=pl.BlockSpec((1, MLA_TQ, OUT_W), lambda b, i: (b, i, 0)),
        out_shape=jax.ShapeDtypeStruct((B, S, OUT_W), BF16),
        scratch_shapes=[pltpu.VMEM((H, 1, MLA_TQ), F32), pltpu.VMEM((H, VROWS, MLA_TQ), F32),
                        pltpu.VMEM((MLA_TK, MLA_TQ), F32)],
        compiler_params=_params("parallel", "arbitrary"),
        name="mla",
    )(qt, k, vt)


def _sb_kernel(qt_ref, k_ref, vt_ref, u_ref, o_ref, c_ref, acc_ref, z_ref):
    qi = pl.program_id(1)
    t = SB_T
    n_sub = t // SB_SUBK
    c_ref[...] = jnp.zeros(c_ref.shape, F32)
    acc_ref[...] = jnp.zeros(acc_ref.shape, F32)

    def qk(h, j):
        k0 = pl.multiple_of(j * t, t)
        return jnp.concatenate(
            [_dot(k_ref[0, h, pl.ds(k0 + s * SB_SUBK, SB_SUBK), :], qt_ref[0, h]) for s in range(n_sub)],
            axis=0)

    z_ref[...] = qk(0, qi)

    def step(j, masked):
        k0 = pl.multiple_of(j * t, t)
        if masked:
            strict = (lax.broadcasted_iota(jnp.int32, (t, t), 0)
                      < lax.broadcasted_iota(jnp.int32, (t, t), 1))

        def logits(h, _):
            if h == 0:
                return z_ref[...]
            if h == SB_HEADS:
                return qk(0, jnp.maximum(j - 1, 0))
            return qk(h, j)

        def suffix_sums(h, z):
            if h == SB_HEADS:
                z_ref[...] = z
                return None
            z2 = z * LOG2E
            neg_abs = pltpu.bitcast(pltpu.bitcast(z2, jnp.int32) | INT_MIN, F32)
            nl = jnp.maximum(z2, 0.0) + jnp.log2(1.0 + jnp.exp2(neg_abs))
            if masked:
                nl = jnp.where(strict, nl, 0.0)
            hi = nl.astype(BF16)
            lo = (nl - hi.astype(F32)).astype(BF16)
            sums = []
            for s in range(n_sub):
                rows = slice(s * SB_SUBK, (s + 1) * SB_SUBK)
                sums.append(_dot(u_ref[...], jnp.concatenate([hi[rows], lo[rows]], axis=0)))
            return z2, sums

        def weights_pv(h, carry):
            if h == SB_HEADS:
                return
            z2, sums = carry
            run = jnp.zeros((1, t), F32)
            blk = None
            for s in reversed(range(n_sub)):
                rows = slice(s * SB_SUBK, (s + 1) * SB_SUBK)
                w = jnp.exp2(z2[rows] - sums[s][0:SB_SUBK])
                if masked:
                    w = jnp.where(strict[rows], w, 0.0)
                part = _dot(vt_ref[0, h, :, pl.ds(k0 + s * SB_SUBK, SB_SUBK)], w.astype(BF16))
                blk = part if blk is None else blk + jnp.exp2(-run) * part
                run = run + sums[s][SB_SUBK:SB_SUBK + 1]
            acc_ref[h] += jnp.exp2(-c_ref[h]) * blk
            c_ref[h] += run

        _pipeline(SB_HEADS + 1, [logits, suffix_sums, weights_pv])

    step(qi, True)

    def body(i, c):
        step(qi - 1 - i, False)
        return c

    lax.fori_loop(0, qi, body, 0)
    outs = [acc_ref[h] for h in range(SB_HEADS)]
    outs.append(jnp.zeros((OUT_W - SB_HEADS * HEAD, t), F32))
    o_ref[0] = jnp.concatenate(outs, axis=0).T.astype(BF16)


def _sb_call(qt, k, vt, u):
    B, H, _, S = qt.shape
    res = lambda shape: pl.BlockSpec(shape, lambda b, i: (b, 0, 0, 0), pipeline_mode=pl.Buffered(1))
    return pl.pallas_call(
        _sb_kernel,
        grid=(B, S // SB_T),
        in_specs=[pl.BlockSpec((1, H, HEAD, SB_T), lambda b, i: (b, 0, 0, i)),
                  res((1, H, S, HEAD)), res((1, H, HEAD, S)), _const_spec(u.shape)],
        out_specs=pl.BlockSpec((1, SB_T, OUT_W), lambda b, i: (b, i, 0)),
        out_shape=jax.ShapeDtypeStruct((B, S, OUT_W), BF16),
        scratch_shapes=[pltpu.VMEM((H, 1, SB_T), F32), pltpu.VMEM((H, HEAD, SB_T), F32),
                        pltpu.VMEM((SB_T, SB_T), F32)],
        compiler_params=_params("parallel", "arbitrary"),
        name="stickbreak",
    )(qt, k, vt, u)


def _bias_kernel(rb_ref, o_ref):
    v = pl.program_id(0)
    shape = (DSA_TK, DSA_TQ)
    off = jnp.where(v < DSA_SUB, v * DSA_TQ, DSA_TK)
    dist = off + lax.broadcasted_iota(jnp.int32, shape, 1) - lax.broadcasted_iota(jnp.int32, shape, 0)
    max_exact = REL_BUCKETS // 2
    d = jnp.maximum(dist, 1).astype(F32)
    large = max_exact + (jnp.log(d / max_exact) / math.log(REL_MAX_DIST / max_exact)
                         * (REL_BUCKETS - max_exact)).astype(jnp.int32)
    large = jnp.minimum(large, REL_BUCKETS - 1)
    bucket = jnp.where(dist < max_exact, dist, large)
    for h in range(DSA_HEADS):
        t = jnp.zeros(shape, F32)
        for b in range(REL_BUCKETS):
            t = jnp.where(bucket == b, rb_ref[b, h] - rb_ref[REL_BUCKETS - 1, h], t)
        o_ref[0, h] = jnp.where(dist >= 0, t, NEG)


def _bias_call(rel_bias):
    nv = DSA_SUB + 1
    return pl.pallas_call(
        _bias_kernel,
        grid=(nv,),
        in_specs=[pl.BlockSpec(memory_space=pltpu.SMEM)],
        out_specs=pl.BlockSpec((1, DSA_HEADS, DSA_TK, DSA_TQ), lambda v: (v, 0, 0, 0)),
        out_shape=jax.ShapeDtypeStruct((nv, DSA_HEADS, DSA_TK, DSA_TQ), F32),
        compiler_params=_params("arbitrary"),
        name="bias_tiles",
    )(rel_bias)


def _dsa_kernel(iqt_ref, ik_ref, iwt_ref, dqt_ref, dk_ref, dvt_ref, tab_ref, o_ref,
                keys_ref, m_ref, acc_ref, x_ref, *, topk):
    qi = pl.program_id(1)
    n_blk = qi // DSA_SUB + 1
    sub = qi % DSA_SUB
    tq, tk = DSA_TQ, DSA_TK

    iqt = jnp.concatenate([iqt_ref[0, h] for h in range(IDX_HEADS)], axis=1)
    iw = iwt_ref[0]

    def pair_scores(k_ref_, qt_all, g, j, n_heads):
        k0 = pl.multiple_of(j * tk, tk)
        cols = qt_all[:, 2 * g * tq:min(2 * (g + 1), n_heads) * tq]
        half = tk // 2
        return jnp.concatenate(
            [_dot(k_ref_[0, pl.ds(k0 + i * half, half), :], cols) for i in range(2)], axis=0)

    x_ref[...] = pair_scores(ik_ref, iqt, 0, 0, IDX_HEADS)

    def idx_step(j, masked):
        k0 = pl.multiple_of(j * tk, tk)
        n_pairs = IDX_HEADS // 2
        total = []

        def scores(g, _):
            if g == 0:
                return x_ref[...]
            if g == n_pairs:
                return pair_scores(ik_ref, iqt, 0, jnp.minimum(j + 1, n_blk - 1), IDX_HEADS)
            return pair_scores(ik_ref, iqt, g, j, IDX_HEADS)

        def weigh(g, x):
            if g == n_pairs:
                x_ref[...] = x
                return
            part = (jnp.maximum(x[:, 0:tq], 0.0) * iw[2 * g:2 * g + 1]
                    + jnp.maximum(x[:, tq:2 * tq], 0.0) * iw[2 * g + 1:2 * g + 2])
            total[:] = [part if not total else total[0] + part]

        _pipeline(n_pairs + 1, [scores, weigh])
        isc = total[0]
        isc = jnp.where(isc == 0.0, 0.0, isc)
        if masked:
            kpos = k0 + lax.broadcasted_iota(jnp.int32, isc.shape, 0)
            qpos = qi * tq + lax.broadcasted_iota(jnp.int32, isc.shape, 1)
            isc = jnp.where(kpos <= qpos, isc, -jnp.inf)
        bits = pltpu.bitcast(isc, jnp.int32)
        keys_ref[j] = bits ^ (lax.shift_right_arithmetic(bits, 31) & 0x7FFFFFFF)

    def idx_body(j, c):
        idx_step(j, False)
        return c

    lax.fori_loop(0, n_blk - 1, idx_body, 0)
    idx_step(n_blk - 1, True)

    @pl.when(n_blk % 2 == 1)
    def _():
        keys_ref[n_blk] = jnp.full((tk, tq), INT_MIN, jnp.int32)

    def count_ge(cand):
        def body(i, acc):
            for j in (2 * i, 2 * i + 1):
                c = jnp.where(keys_ref[j] >= cand, 1.0, 0.0)
                acc = acc + jnp.sum(c.reshape(COUNT_ACC, tk // COUNT_ACC, tq), axis=0)
            return acc
        acc = lax.fori_loop(0, (n_blk + 1) // 2, body, jnp.zeros((tk // COUNT_ACC, tq), F32))
        return jnp.sum(acc, axis=0, keepdims=True)

    def bit_body(i, state):
        t_u, done, thr_hit = state
        cand_u = t_u | lax.shift_left(jnp.int32(1), 31 - i)
        cand = cand_u ^ INT_MIN
        cnt = count_ge(cand)
        hit = cnt == topk
        thr_hit = jnp.where(hit, jnp.where(done > 0.0, thr_hit, cand), thr_hit)
        done = jnp.where(hit, 1.0, done)
        return jnp.where(cnt >= topk, cand_u, t_u), done, thr_hit

    zero = jnp.zeros((1, tq), jnp.int32)
    state = (zero, jnp.zeros((1, tq), F32), zero)
    first = 0
    for last in DESCENT_CHECKS + (32,):
        stop = last if first == 0 else jnp.where(jnp.min(state[1]) < 1.0, last, first)
        state = lax.fori_loop(first, stop, bit_body, state)
        first = last
    t_u, done, thr_hit = state
    thr = jnp.where(done > 0.0, thr_hit, t_u ^ INT_MIN)

    tie = jnp.where(done > 0.0, 0.0, jnp.where(thr > NEG_INF_KEY, 1.0, 0.0))

    @pl.when(jnp.max(tie) > 0.0)
    def _():
        need = jnp.where(tie > 0.0, topk - count_ge(thr + 1), float(tk * keys_ref.shape[0] + 1))
        row = lax.broadcasted_iota(jnp.int32, (tk, tk), 0)
        col = lax.broadcasted_iota(jnp.int32, (tk, tk), 1)
        before = jnp.where(col < row, 1.0, 0.0).astype(BF16)

        def body(j, run):
            kb = keys_ref[j]
            eq = jnp.where(kb == thr, 1.0, 0.0)
            rank = _dot(before, eq.astype(BF16)) + run
            keys_ref[j] = jnp.where(kb == thr, jnp.where(rank >= need, thr - 1, kb), kb)
            return run + jnp.sum(eq, axis=0, keepdims=True)

        lax.fori_loop(0, n_blk, body, jnp.zeros((1, tq), F32))

    dqt = jnp.concatenate([dqt_ref[0, h] for h in range(DSA_HEADS)], axis=1)
    m_ref[...] = jnp.full(m_ref.shape, NEG, F32)
    acc_ref[...] = jnp.zeros(acc_ref.shape, F32)

    x_ref[...] = pair_scores(dk_ref, dqt, 0, 0, DSA_HEADS)

    def att_step(j, variant):
        k0 = pl.multiple_of(j * tk, tk)
        vt = dvt_ref[0, :, pl.ds(k0, tk)]
        sel = keys_ref[j] >= thr
        n_pairs = (DSA_HEADS + 1) // 2

        def scores(g, _):
            if g == 0:
                return x_ref[...]
            if g == n_pairs:
                return pair_scores(dk_ref, dqt, 0, jnp.minimum(j + 1, n_blk - 1), DSA_HEADS)
            return pair_scores(dk_ref, dqt, g, j, DSA_HEADS)

        def running_max(g, s):
            if g == n_pairs:
                x_ref[...] = s
                return None
            out = []
            for h in range(2 * g, min(2 * (g + 1), DSA_HEADS)):
                a = s[:, (h - 2 * g) * tq:(h - 2 * g + 1) * tq]
                if variant is not None:
                    a = a + tab_ref[variant, h]
                a = jnp.where(sel, a, NEG)
                m_prev = m_ref[h]
                m_new = jnp.maximum(m_prev, jnp.max(a, axis=0, keepdims=True))
                m_ref[h] = m_new
                out.append((h, a, m_prev, m_new))
            return out

        def weights_pv(g, carry):
            if g == n_pairs:
                return
            for h, a, m_prev, m_new in carry:
                p = jnp.exp(a - m_new)
                acc_ref[h] = jnp.exp(m_prev - m_new) * acc_ref[h] + _dot(vt, p.astype(BF16))

        _pipeline(n_pairs + 1, [scores, running_max, weights_pv])

    def far_body(j, c):
        att_step(j, None)
        return c

    lax.fori_loop(0, n_blk - 2, far_body, 0)

    @pl.when(jnp.logical_and(n_blk >= 2, sub == 0))
    def _():
        att_step(n_blk - 2, DSA_SUB)

    @pl.when(jnp.logical_and(n_blk >= 2, sub != 0))
    def _():
        att_step(n_blk - 2, None)

    att_step(n_blk - 1, sub)
    _finish(acc_ref, DSA_HEADS, o_ref)


def _dsa_call(iqt, ik, iwt, dqt, dk, dvt, tab, topk):
    B, _, _, S = iqt.shape
    tq = DSA_TQ
    qh = lambda n: pl.BlockSpec((1, n, HEAD, tq), lambda b, i: (b, 0, 0, i))
    res = lambda shape: pl.BlockSpec(shape, lambda b, i: (b, 0, 0), pipeline_mode=pl.Buffered(1))
    return pl.pallas_call(
        functools.partial(_dsa_kernel, topk=topk),
        grid=(B, S // tq),
        in_specs=[qh(IDX_HEADS), res((1, S, HEAD)),
                  pl.BlockSpec((1, IDX_HEADS, tq), lambda b, i: (b, 0, i)),
                  qh(DSA_HEADS), res((1, S, HEAD)), res((1, VROWS, S)), _const_spec(tab.shape)],
        out_specs=pl.BlockSpec((1, tq, OUT_W), lambda b, i: (b, i, 0)),
        out_shape=jax.ShapeDtypeStruct((B, S, OUT_W), BF16),
        scratch_shapes=[pltpu.VMEM((S // DSA_TK + (S // DSA_TK) % 2, DSA_TK, tq), jnp.int32),
                        pltpu.VMEM((DSA_HEADS, 1, tq), F32),
                        pltpu.VMEM((DSA_HEADS, VROWS, tq), F32),
                        pltpu.VMEM((DSA_TK, 2 * tq), F32)],
        compiler_params=_params("parallel", "arbitrary"),
        name="dsa",
    )(iqt, ik, iwt, dqt, dk, dvt, tab)


def _post_kernel(h_ref, om_ref, od_ref, os_ref, p_ref, wo_ref, gf_ref, wg_ref, wu_ref, wd_ref,
                 gp_ref, wpg_ref, wpp_ref, gfin_ref, out_ref, *, final):
    h = (h_ref[0] + _dot(om_ref[0], wo_ref[0]) + _dot(od_ref[0], wo_ref[1])
         + _dot(os_ref[0], wo_ref[2]))

    hf = _rms(h, gf_ref[...]).astype(BF16)
    d_ff = wg_ref.shape[1]
    ffn = jnp.zeros(h.shape, F32)
    for c0 in range(0, d_ff, FF_CHUNK):
        g = _dot(hf, wg_ref[:, c0:c0 + FF_CHUNK])
        u = _dot(hf, wu_ref[:, c0:c0 + FF_CHUNK])
        ffn = ffn + _dot((g * jax.nn.sigmoid(g) * u).astype(BF16), wd_ref[c0:c0 + FF_CHUNK, :])
    h = h + ffn

    gate = jax.nn.sigmoid(_dot(_rms(h, gp_ref[...]).astype(BF16), wpg_ref[...]))
    h = h + gate * _dot(p_ref[0, 0].astype(BF16), wpp_ref[...])
    if final:
        h = _rms(h, gfin_ref[...])
    out_ref[0] = h


def _post_call(h, om, od, os_, p, layer, wo, gf, wg, wu, wd, gp, wpg, wpp, gfin, final):
    B, S, D = h.shape
    mix = pl.BlockSpec((1, TM, OUT_W), lambda b, t: (b, t, 0))
    tok = pl.BlockSpec((1, TM, D), lambda b, t: (b, t, 0))
    return pl.pallas_call(
        functools.partial(_post_kernel, final=final),
        grid=(B, S // TM),
        in_specs=[tok, mix, mix, mix,
                  pl.BlockSpec((1, 1, TM, p.shape[-1]), lambda b, t: (layer, b, t, 0)),
                  _const_spec(wo.shape), _const_spec(gf.shape), _const_spec(wg.shape),
                  _const_spec(wu.shape), _const_spec(wd.shape), _const_spec(gp.shape),
                  _const_spec(wpg.shape), _const_spec(wpp.shape), _const_spec(gfin.shape)],
        out_specs=tok,
        out_shape=jax.ShapeDtypeStruct((B, S, D), F32),
        compiler_params=_params("parallel", "parallel"),
        name="post",
    )(h, om, od, os_, p, wo, gf, wg, wu, wd, gp, wpg, wpp, gfin)


def _swap_halves(w):
    half = w.shape[-1] // 2
    return jnp.concatenate([w[..., half:], w[..., :half]], axis=-1)


def _pack_w_in(w_in):
    L, D, _ = w_in.shape
    z = lambda n: jnp.zeros((L, D, n), F32)
    o = 0
    cols = {}
    for name, n in (("cq", MLA_Q_LORA), ("ckv", MLA_KV_LORA), ("kr", MLA_ROPE),
                    ("dq", DSA_HEADS * DSA_DIM), ("dk", DSA_DIM), ("dv", DSA_DIM),
                    ("iq", IDX_HEADS * IDX_DIM), ("ik", IDX_DIM), ("iw", IDX_HEADS),
                    ("sq", SB_HEADS * SB_DIM), ("sk", SB_HEADS * SB_DIM), ("sv", SB_HEADS * SB_DIM)):
        cols[name] = w_in[:, :, o:o + n]
        o += n
    pad_r = LANES - MLA_NOPE - MLA_ROPE
    std = jnp.concatenate([
        cols["cq"], cols["ckv"],
        z(MLA_NOPE), cols["kr"], z(pad_r),
        z(MLA_NOPE), _swap_halves(cols["kr"]), z(pad_r),
        cols["sk"], cols["ik"],
        cols["dk"], z(LANES - DSA_DIM)], axis=-1)
    tr = jnp.concatenate([
        cols["dq"] * DSA_DIM ** -0.5, cols["iq"] * IDX_DIM ** -0.5, cols["sq"] * SB_DIM ** -0.5,
        cols["sv"], cols["dv"], cols["iw"], z(ROW_IW[1] - ROW_IW[0] - IDX_HEADS)], axis=-1)
    assert std.shape[-1] == W_STD_COLS and tr.shape[-1] == W_T_ROWS
    return std.astype(BF16), jnp.swapaxes(tr, 1, 2).astype(BF16)


def _pack_mla(w_uq, w_ukv):
    L = w_uq.shape[0]
    dq = MLA_NOPE + MLA_ROPE
    pad_r = LANES - dq
    uq = w_uq.reshape(L, MLA_Q_LORA, MLA_HEADS, dq)
    zq = lambda n: jnp.zeros((L, MLA_Q_LORA, MLA_HEADS, n), F32)
    wqa = jnp.concatenate([uq, zq(pad_r)], axis=-1)
    wqb = jnp.concatenate([zq(MLA_NOPE), _swap_halves(uq[..., MLA_NOPE:]), zq(pad_r)], axis=-1)
    ukv = w_ukv.reshape(L, MLA_KV_LORA, MLA_HEADS, MLA_NOPE + MLA_V)
    wk = jnp.concatenate([ukv[..., :MLA_NOPE],
                          jnp.zeros((L, MLA_KV_LORA, MLA_HEADS, LANES - MLA_NOPE), F32)], axis=-1)
    wv = ukv[..., MLA_NOPE:]
    flat = lambda a: a.reshape(L, a.shape[1], -1)
    tr = lambda a: jnp.swapaxes(flat(a), 1, 2).astype(BF16)
    return tr(wqa), tr(wqb), flat(wk).astype(BF16), tr(wv)


def _rope_tables(S):
    half = MLA_ROPE // 2
    inv = ROPE_THETA ** (-jnp.arange(half, dtype=F32) / half)
    ang = jnp.arange(S, dtype=jnp.int32).astype(F32)[:, None] * inv[None, :]
    cos, sin = jnp.cos(ang), jnp.sin(ang)
    pad_r = LANES - MLA_NOPE - MLA_ROPE
    cos_t = jnp.concatenate([jnp.ones((S, MLA_NOPE), F32), cos, cos, jnp.zeros((S, pad_r), F32)], axis=-1)
    sin_t = jnp.concatenate([jnp.zeros((S, MLA_NOPE), F32), -sin, sin, jnp.zeros((S, pad_r), F32)], axis=-1)
    return cos_t, sin_t


def _pack_w_o(w_o):
    L, _, D = w_o.shape
    a = MLA_HEADS * MLA_V
    b = a + DSA_HEADS * DSA_DIM
    pad = jnp.zeros((L, OUT_W - DSA_HEADS * DSA_DIM, D), F32)
    blocks = [w_o[:, :a], jnp.concatenate([w_o[:, a:b], pad], axis=1),
              jnp.concatenate([w_o[:, b:], pad], axis=1)]
    return jnp.stack(blocks, axis=1).astype(BF16)


def kernel(x, p, w_in, attn_norm, mla_q_norm, mla_w_uq, mla_kv_norm, mla_w_ukv, rel_bias, w_o,
           ffn_norm, w_gate, w_up, w_down, ple_norm, w_ple_gate, w_ple_proj, final_norm):
    B, S, D = x.shape
    depth = w_in.shape[0]
    assert S % TM == 0 and S % DSA_TK == 0 and S % MLA_TK == 0
    assert w_gate.shape[-1] % FF_CHUNK == 0
    topk = min(DSA_TOPK, S // 4)

    w_std, w_tr = _pack_w_in(w_in)
    wqa, wqb, wk, wv = _pack_mla(mla_w_uq, mla_w_ukv)
    cos_s, sin_s = _rope_tables(S)
    cos_t, sin_t = cos_s.T, sin_s.T
    row = lambda g: g.reshape(depth, 1, -1)
    g_attn, g_q, g_kv, g_ffn, g_ple = map(row, (attn_norm, mla_q_norm, mla_kv_norm, ffn_norm, ple_norm))
    g_fin = final_norm.reshape(1, -1)
    wo = _pack_w_o(w_o)
    wg, wu, wd, wpg, wpp = (a.astype(BF16) for a in (w_gate, w_up, w_down, w_ple_gate, w_ple_proj))
    tri = jnp.triu(jnp.ones((SB_SUBK, SB_SUBK), F32), 0)
    tail = jnp.zeros((16, SB_SUBK), F32).at[0].set(1.0)
    u = jnp.concatenate([tri, tail], axis=0)
    u = jnp.concatenate([u, u], axis=1).astype(BF16)

    tab = _bias_call(rel_bias)

    h = x
    for i in range(depth):
        (mqt, mk, mvt, dqt, dk, dvt, iqt, ik, iwt, sqt, sk, svt) = _proj_call(
            h, g_attn[i], w_std[i], w_tr[i], g_q[i], wqa[i], wqb[i], g_kv[i], wk[i], wv[i],
            cos_s, sin_s, cos_t, sin_t)
        o_mla = _mla_call(mqt, mk, mvt)
        o_dsa = _dsa_call(iqt, ik, iwt, dqt, dk, dvt, tab, topk)
        o_sb = _sb_call(sqt, sk, svt, u)
        h = _post_call(h, o_mla, o_dsa, o_sb, p, i, wo[i], g_ffn[i], wg[i], wu[i], wd[i],
                       g_ple[i], wpg[i], wpp[i], g_fin, final=(i == depth - 1))
    return h
```

```python
import functools
import math

import jax
import jax.numpy as jnp
from jax import lax
from jax.experimental import pallas as pl
from jax.experimental.pallas import tpu as pltpu

EPS = 1e-6
MLA_HEADS = 6
MLA_Q_LORA = 384
MLA_KV_LORA = 256
MLA_NOPE = 64
MLA_ROPE = 32
MLA_V = 64
ROPE_THETA = 10000.0
DSA_HEADS = 5
DSA_DIM = 64
IDX_HEADS = 8
IDX_DIM = 64
DSA_TOPK = 256
SB_HEADS = 5
SB_DIM = 64
REL_BUCKETS = 32
REL_MAX_DIST = 128

LANES = 128
HEAD = 64
VROWS = 80
OUT_W = 384
VMEM_LIMIT = 56 * 1024 * 1024

TM = 512
MLA_TQ, MLA_TK = 256, 512
MLA_UNROLL = 4
SB_T = 256
SB_UNROLL = 4
DSA_TQ, DSA_TK = 128, 512
DSA_SUB = DSA_TK // DSA_TQ
DSA_UNROLL = 4
COUNT_ACC = 8
DESCENT_CHECKS = (24, 28)
FF_CHUNK = 1408

NEG = -1e30
INT_MIN = -2 ** 31
NEG_INF_KEY = -2139095041
LOG2E = math.log2(math.e)

F32 = jnp.float32
BF16 = jnp.bfloat16

SEG_CQ = (0, 384)
SEG_CKV = (384, 640)
SEG_KRA = (640, 768)
SEG_KRB = (768, 896)
SEG_SK_IK = (896, 1280)
SEG_DK = (1280, 1408)
W_STD_COLS = 1408
ROW_DQ = (0, 320)
ROW_IQ = (320, 832)
ROW_SQ = (832, 1152)
ROW_SV = (1152, 1472)
ROW_DV = (1472, 1536)
ROW_IW = (1536, 1552)
W_T_ROWS = 1552


def _rms(x, g):
    return x * lax.rsqrt(jnp.mean(x * x, axis=-1, keepdims=True) + EPS) * g


def _dot(a, b):
    return jnp.dot(a, b, preferred_element_type=F32)


def _const_spec(shape):
    n = len(shape)
    return pl.BlockSpec(shape, lambda *_: (0,) * n, pipeline_mode=pl.Buffered(1))


def _params(*sem):
    return pltpu.CompilerParams(dimension_semantics=sem, vmem_limit_bytes=VMEM_LIMIT)


def _pipeline(n, stages):
    held = [dict() for _ in stages]
    for t in range(n + len(stages) - 1):
        for s, stage in enumerate(stages):
            i = t - s
            if 0 <= i < n:
                held[s][i] = stage(i, held[s - 1].pop(i) if s else None)


def _for_blocks(n, unroll, step):
    def body(i, c):
        step([unroll * i + d for d in range(unroll)])
        return c

    lax.fori_loop(0, n // unroll, body, 0)
    rem = n % unroll
    group = unroll // 2
    while group:
        @pl.when(rem & group != 0)
        def _(group=group):
            base = n - (rem & (2 * group - 1))
            step([base + d for d in range(group)])
        group //= 2


def _ones_rows(n):
    r = lax.broadcasted_iota(jnp.int32, (VROWS - HEAD, n), 0)
    return jnp.where(r == 0, 1.0, 0.0).astype(BF16)


def _proj_kernel(h_ref, g_ref, w_ref, wt_ref, gq_ref, wqa_ref, wqb_ref, gkv_ref, wk_ref, wv_ref,
                 cos_ref, sin_ref, cost_ref, sint_ref,
                 mqt_ref, mk_ref, mvt_ref, dqt_ref, dk_ref, dvt_ref, iqt_ref, ik_ref, iwt_ref,
                 sqt_ref, sk_ref, svt_ref):
    hn = _rms(h_ref[0], g_ref[...])
    hnt = hn.T.astype(BF16)
    hn = hn.astype(BF16)
    tm = hn.shape[0]
    ones = _ones_rows(tm)

    def seg(ab):
        return _dot(hn, w_ref[:, ab[0]:ab[1]])

    def seg_t(ab):
        return _dot(wt_ref[ab[0]:ab[1], :], hnt)

    cqn_t = _rms(seg(SEG_CQ), gq_ref[...]).T.astype(BF16)
    qa = _dot(wqa_ref[...], cqn_t)
    qb = _dot(wqb_ref[...], cqn_t)
    cos_t = cost_ref[...]
    sin_t = sint_ref[...]
    for h in range(MLA_HEADS):
        sl = slice(h * LANES, (h + 1) * LANES)
        mqt_ref[0, h] = (qa[sl] * cos_t + qb[sl] * sin_t).astype(BF16)

    ckvn = _rms(seg(SEG_CKV), gkv_ref[...])
    ckvn_t = ckvn.T.astype(BF16)
    ckvn = ckvn.astype(BF16)
    kn = _dot(ckvn, wk_ref[...])
    kr = seg(SEG_KRA) * cos_ref[...] + seg(SEG_KRB) * sin_ref[...]
    for h in range(MLA_HEADS):
        mk_ref[0, h] = (kn[:, h * LANES:(h + 1) * LANES] + kr).astype(BF16)
    vt = _dot(wv_ref[...], ckvn_t)
    for h in range(MLA_HEADS):
        mvt_ref[0, h, 0:HEAD, :] = vt[h * HEAD:(h + 1) * HEAD].astype(BF16)
        mvt_ref[0, h, HEAD:VROWS, :] = ones

    a = seg(SEG_SK_IK)
    for h in range(SB_HEADS):
        sk_ref[0, h] = a[:, h * HEAD:(h + 1) * HEAD].astype(BF16)
    ik_ref[0] = a[:, 320:384].astype(BF16)
    dk_ref[0] = seg(SEG_DK)[:, 0:HEAD].astype(BF16)

    a = seg_t(ROW_DQ)
    for h in range(DSA_HEADS):
        dqt_ref[0, h] = a[h * HEAD:(h + 1) * HEAD].astype(BF16)
    a = seg_t(ROW_IQ)
    for h in range(IDX_HEADS):
        iqt_ref[0, h] = a[h * HEAD:(h + 1) * HEAD].astype(BF16)
    a = seg_t(ROW_SQ)
    for h in range(SB_HEADS):
        sqt_ref[0, h] = a[h * HEAD:(h + 1) * HEAD].astype(BF16)
    a = seg_t(ROW_SV)
    for h in range(SB_HEADS):
        svt_ref[0, h] = a[h * HEAD:(h + 1) * HEAD].astype(BF16)
    dvt_ref[0, 0:HEAD, :] = seg_t(ROW_DV).astype(BF16)
    dvt_ref[0, HEAD:VROWS, :] = ones
    iwt_ref[0] = seg_t(ROW_IW)[0:IDX_HEADS] * IDX_HEADS ** -0.5


def _proj_call(h, g, w, wt, gq, wqa, wqb, gkv, wk, wv, cos_s, sin_s, cos_t, sin_t):
    B, S, D = h.shape
    nt = S // TM
    tok = lambda w_: pl.BlockSpec((1, TM, w_), lambda b, t: (b, t, 0))
    heads = lambda n, w_: pl.BlockSpec((1, n, TM, w_), lambda b, t: (b, 0, t, 0))
    heads_t = lambda n, r: pl.BlockSpec((1, n, r, TM), lambda b, t: (b, 0, 0, t))
    rows_t = lambda r: pl.BlockSpec((1, r, TM), lambda b, t: (b, 0, t))
    hs = lambda n, w_: jax.ShapeDtypeStruct((B, n, S, w_), BF16)
    hts = lambda n, r: jax.ShapeDtypeStruct((B, n, r, S), BF16)
    return pl.pallas_call(
        _proj_kernel,
        grid=(B, nt),
        in_specs=[tok(D), _const_spec(g.shape), _const_spec(w.shape), _const_spec(wt.shape),
                  _const_spec(gq.shape), _const_spec(wqa.shape), _const_spec(wqb.shape),
                  _const_spec(gkv.shape), _const_spec(wk.shape), _const_spec(wv.shape),
                  pl.BlockSpec((TM, LANES), lambda b, t: (t, 0)),
                  pl.BlockSpec((TM, LANES), lambda b, t: (t, 0)),
                  pl.BlockSpec((LANES, TM), lambda b, t: (0, t)),
                  pl.BlockSpec((LANES, TM), lambda b, t: (0, t))],
        out_specs=[heads_t(MLA_HEADS, LANES), heads(MLA_HEADS, LANES), heads_t(MLA_HEADS, VROWS),
                   heads_t(DSA_HEADS, HEAD), tok(HEAD), rows_t(VROWS),
                   heads_t(IDX_HEADS, HEAD), tok(HEAD), rows_t(IDX_HEADS),
                   heads_t(SB_HEADS, HEAD), heads(SB_HEADS, HEAD), heads_t(SB_HEADS, HEAD)],
        out_shape=[hts(MLA_HEADS, LANES), hs(MLA_HEADS, LANES), hts(MLA_HEADS, VROWS),
                   hts(DSA_HEADS, HEAD), jax.ShapeDtypeStruct((B, S, HEAD), BF16),
                   jax.ShapeDtypeStruct((B, VROWS, S), BF16),
                   hts(IDX_HEADS, HEAD), jax.ShapeDtypeStruct((B, S, HEAD), BF16),
                   jax.ShapeDtypeStruct((B, IDX_HEADS, S), F32),
                   hts(SB_HEADS, HEAD), hs(SB_HEADS, HEAD), hts(SB_HEADS, HEAD)],
        compiler_params=_params("parallel", "parallel"),
        name="proj",
    )(h, g, w, wt, gq, wqa, wqb, gkv, wk, wv, cos_s, sin_s, cos_t, sin_t)


def _finish(acc_ref, n_heads, o_ref):
    outs = [acc_ref[h, 0:HEAD, :] / acc_ref[h, HEAD:HEAD + 1, :] for h in range(n_heads)]
    pad = OUT_W - n_heads * HEAD
    if pad:
        outs.append(jnp.zeros((pad, outs[0].shape[1]), F32))
    o_ref[0] = jnp.concatenate(outs, axis=0).T.astype(BF16)


def _mla_kernel(qt_ref, k_ref, vt_ref, o_ref, m_ref, acc_ref, s_ref):
    qi = pl.program_id(1)
    tq, tk = MLA_TQ, MLA_TK
    m_ref[...] = jnp.full(m_ref.shape, NEG, F32)
    acc_ref[...] = jnp.zeros(acc_ref.shape, F32)
    scale2 = (MLA_NOPE + MLA_ROPE) ** -0.5 * LOG2E
    n_full = (qi * tq) // tk
    n_all = ((qi + 1) * tq + tk - 1) // tk

    def qk(h, j):
        k0 = pl.multiple_of(j * tk, tk)
        half = tk // 4
        return jnp.concatenate(
            [_dot(k_ref[0, h, pl.ds(k0 + i * half, half), :], qt_ref[0, h]) for i in range(4)],
            axis=0)

    s_ref[...] = qk(0, 0)

    def step(js, masked):
        items = [(j, h) for j in js for h in range(MLA_HEADS)]
        n = len(items)

        def scores(i, _):
            if i == 0:
                return s_ref[...]
            if i == n:
                return qk(0, jnp.minimum(js[-1] + 1, n_all - 1))
            j, h = items[i]
            return qk(h, j)

        def running_max(i, s):
            if i == n:
                s_ref[...] = s
                return None
            j, h = items[i]
            s = s * scale2
            if masked:
                kpos = j * tk + lax.broadcasted_iota(jnp.int32, (tk, tq), 0)
                qpos = qi * tq + lax.broadcasted_iota(jnp.int32, (tk, tq), 1)
                s = jnp.where(kpos <= qpos, s, NEG)
            m_prev = m_ref[h]
            m_new = jnp.maximum(m_prev, jnp.max(s, axis=0, keepdims=True))
            m_ref[h] = m_new
            return s, m_prev, m_new

        def weights_pv(i, carry):
            if i == n:
                return
            j, h = items[i]
            s, m_prev, m_new = carry
            p = jnp.exp2(s - m_new)
            k0 = pl.multiple_of(j * tk, tk)
            acc_ref[h] = (jnp.exp2(m_prev - m_new) * acc_ref[h]
                          + _dot(vt_ref[0, h, :, pl.ds(k0, tk)], p.astype(BF16)))

        _pipeline(n + 1, [scores, running_max, weights_pv])

    _for_blocks(n_full, MLA_UNROLL, lambda js: step(js, False))
    step([n_full], True)
    _finish(acc_ref, MLA_HEADS, o_ref)


def _mla_call(qt, k, vt):
    B, H, _, S = qt.shape
    res = lambda shape: pl.BlockSpec(shape, lambda b, i: (b, 0, 0, 0), pipeline_mode=pl.Buffered(1))
    return pl.pallas_call(
        _mla_kernel,
        grid=(B, S // MLA_TQ),
        in_specs=[pl.BlockSpec((1, H, LANES, MLA_TQ), lambda b, i: (b, 0, 0, i)),
                  res((1, H, S, LANES)), res((1, H, VROWS, S))],
        out_specs=pl.BlockSpec((1, MLA_TQ, OUT_W), lambda b, i: (b, i, 0)),
        out_shape=jax.ShapeDtypeStruct((B, S, OUT_W), BF16),
        scratch_shapes=[pltpu.VMEM((H, 1, MLA_TQ), F32), pltpu.VMEM((H, VROWS, MLA_TQ), F32),
                        pltpu.VMEM((MLA_TK, MLA_TQ), F32)],
        compiler_params=_params("parallel", "arbitrary"),
        name="mla",
    )(qt, k, vt)


def _sb_kernel(qt_ref, k_ref, vt_ref, u_ref, o_ref, c_ref, acc_ref, z_ref):
    qi = pl.program_id(1)
    t = SB_T
    c_ref[...] = jnp.zeros(c_ref.shape, F32)
    acc_ref[...] = jnp.zeros(acc_ref.shape, F32)

    def qk(h, j):
        k0 = pl.multiple_of(j * t, t)
        half = t // 2
        return jnp.concatenate(
            [_dot(k_ref[0, h, pl.ds(k0 + s * half, half), :], qt_ref[0, h]) for s in range(2)],
            axis=0)

    strict = (lax.broadcasted_iota(jnp.int32, (t, t), 0) < lax.broadcasted_iota(jnp.int32, (t, t), 1))

    def suffix_sums(z, masked):
        z2 = z * LOG2E
        neg_abs = pltpu.bitcast(pltpu.bitcast(z2, jnp.int32) | INT_MIN, F32)
        nl = jnp.maximum(z2, 0.0) + jnp.log2(1.0 + jnp.exp2(neg_abs))
        if masked:
            nl = jnp.where(strict, nl, 0.0)
        nlb = nl.astype(BF16)
        sums = jnp.concatenate([_dot(u_ref[0:t // 2, :], nlb), _dot(u_ref[t // 2:, :], nlb)], axis=0)
        return z2, sums

    z_ref[...] = qk(0, qi)

    def step(js, masked):
        items = [(j, h) for j in js for h in range(SB_HEADS)]
        n = len(items)

        def logits(i, _):
            if i == 0:
                return z_ref[...]
            if i == n:
                return qk(0, jnp.maximum(js[-1] - 1, 0))
            j, h = items[i]
            return qk(h, j)

        def sums_stage(i, z):
            if i == n:
                z_ref[...] = z
                return None
            return suffix_sums(z, masked)

        def weights_pv(i, carry):
            if i == n:
                return
            j, h = items[i]
            z2, sums = carry
            w = jnp.exp2(z2 - sums[0:t])
            if masked:
                w = jnp.where(strict, w, 0.0)
            k0 = pl.multiple_of(j * t, t)
            acc_ref[h] += jnp.exp2(-c_ref[h]) * _dot(vt_ref[0, h, :, pl.ds(k0, t)], w.astype(BF16))
            c_ref[h] += sums[t:t + 1]

        _pipeline(n + 1, [logits, sums_stage, weights_pv])

    step([qi], True)

    _for_blocks(qi, SB_UNROLL, lambda idx: step([qi - 1 - i for i in idx], False))

    outs = [acc_ref[h] for h in range(SB_HEADS)]
    outs.append(jnp.zeros((OUT_W - SB_HEADS * HEAD, t), F32))
    o_ref[0] = jnp.concatenate(outs, axis=0).T.astype(BF16)


def _sb_call(qt, k, vt, u):
    B, H, _, S = qt.shape
    res = lambda shape: pl.BlockSpec(shape, lambda b, i: (b, 0, 0, 0), pipeline_mode=pl.Buffered(1))
    return pl.pallas_call(
        _sb_kernel,
        grid=(B, S // SB_T),
        in_specs=[pl.BlockSpec((1, H, HEAD, SB_T), lambda b, i: (b, 0, 0, i)),
                  res((1, H, S, HEAD)), res((1, H, HEAD, S)), _const_spec(u.shape)],
        out_specs=pl.BlockSpec((1, SB_T, OUT_W), lambda b, i: (b, i, 0)),
        out_shape=jax.ShapeDtypeStruct((B, S, OUT_W), BF16),
        scratch_shapes=[pltpu.VMEM((H, 1, SB_T), F32), pltpu.VMEM((H, HEAD, SB_T), F32),
                        pltpu.VMEM((SB_T, SB_T), F32)],
        compiler_params=_params("parallel", "arbitrary"),
        name="stickbreak",
    )(qt, k, vt, u)


def _bias_kernel(rb_ref, o_ref):
    v = pl.program_id(0)
    shape = (DSA_TK, DSA_TQ)
    off = jnp.where(v < DSA_SUB, v * DSA_TQ, DSA_TK)
    dist = off + lax.broadcasted_iota(jnp.int32, shape, 1) - lax.broadcasted_iota(jnp.int32, shape, 0)
    max_exact = REL_BUCKETS // 2
    d = jnp.maximum(dist, 1).astype(F32)
    large = max_exact + (jnp.log(d / max_exact) / math.log(REL_MAX_DIST / max_exact)
                         * (REL_BUCKETS - max_exact)).astype(jnp.int32)
    large = jnp.minimum(large, REL_BUCKETS - 1)
    bucket = jnp.where(dist < max_exact, dist, large)
    for h in range(DSA_HEADS):
        t = jnp.zeros(shape, F32)
        for b in range(REL_BUCKETS):
            t = jnp.where(bucket == b, rb_ref[b, h] - rb_ref[REL_BUCKETS - 1, h], t)
        o_ref[0, h] = jnp.where(dist >= 0, t, NEG)


def _bias_call(rel_bias):
    nv = DSA_SUB + 1
    return pl.pallas_call(
        _bias_kernel,
        grid=(nv,),
        in_specs=[pl.BlockSpec(memory_space=pltpu.SMEM)],
        out_specs=pl.BlockSpec((1, DSA_HEADS, DSA_TK, DSA_TQ), lambda v: (v, 0, 0, 0)),
        out_shape=jax.ShapeDtypeStruct((nv, DSA_HEADS, DSA_TK, DSA_TQ), F32),
        compiler_params=_params("arbitrary"),
        name="bias_tiles",
    )(rel_bias)


def _dsa_kernel(iqt_ref, ik_ref, iwt_ref, dqt_ref, dk_ref, dvt_ref, tab_ref, o_ref,
                keys_ref, m_ref, acc_ref, x_ref, *, topk):
    qi = pl.program_id(1)
    n_blk = qi // DSA_SUB + 1
    sub = qi % DSA_SUB
    tq, tk = DSA_TQ, DSA_TK

    iqt = jnp.concatenate([iqt_ref[0, h] for h in range(IDX_HEADS)], axis=1)
    iw = iwt_ref[0]

    def pair_scores(k_ref_, qt_all, g, j, n_heads):
        k0 = pl.multiple_of(j * tk, tk)
        cols = qt_all[:, 2 * g * tq:min(2 * (g + 1), n_heads) * tq]
        half = tk // 2
        return jnp.concatenate(
            [_dot(k_ref_[0, pl.ds(k0 + i * half, half), :], cols) for i in range(2)], axis=0)

    x_ref[...] = pair_scores(ik_ref, iqt, 0, 0, IDX_HEADS)

    def idx_step(js, masked):
        n_pairs = IDX_HEADS // 2
        items = [(j, g) for j in js for g in range(n_pairs)]
        n = len(items)
        total = {}

        def scores(i, _):
            if i == 0:
                return x_ref[...]
            if i == n:
                return pair_scores(ik_ref, iqt, 0, jnp.minimum(js[-1] + 1, n_blk - 1), IDX_HEADS)
            j, g = items[i]
            return pair_scores(ik_ref, iqt, g, j, IDX_HEADS)

        def weigh(i, x):
            if i == n:
                x_ref[...] = x
                return
            j, g = items[i]
            part = (jnp.maximum(x[:, 0:tq], 0.0) * iw[2 * g:2 * g + 1]
                    + jnp.maximum(x[:, tq:2 * tq], 0.0) * iw[2 * g + 1:2 * g + 2])
            blk = i // n_pairs
            total[blk] = part if g == 0 else total[blk] + part
            if g < n_pairs - 1:
                return
            isc = total.pop(blk)
            isc = jnp.where(isc == 0.0, 0.0, isc)
            if masked:
                kpos = j * tk + lax.broadcasted_iota(jnp.int32, isc.shape, 0)
                qpos = qi * tq + lax.broadcasted_iota(jnp.int32, isc.shape, 1)
                isc = jnp.where(kpos <= qpos, isc, -jnp.inf)
            bits = pltpu.bitcast(isc, jnp.int32)
            keys_ref[j] = bits ^ (lax.shift_right_arithmetic(bits, 31) & 0x7FFFFFFF)

        _pipeline(n + 1, [scores, weigh])

    _for_blocks(n_blk - 1, DSA_UNROLL, lambda js: idx_step(js, False))
    idx_step([n_blk - 1], True)

    @pl.when(n_blk % 2 == 1)
    def _():
        keys_ref[n_blk] = jnp.full((tk, tq), INT_MIN, jnp.int32)

    def count_ge(cand):
        def body(i, acc):
            for j in (2 * i, 2 * i + 1):
                c = jnp.where(keys_ref[j] >= cand, 1.0, 0.0)
                acc = acc + jnp.sum(c.reshape(COUNT_ACC, tk // COUNT_ACC, tq), axis=0)
            return acc
        acc = lax.fori_loop(0, (n_blk + 1) // 2, body, jnp.zeros((tk // COUNT_ACC, tq), F32))
        return jnp.sum(acc, axis=0, keepdims=True)

    def bit_body(i, state):
        t_u, done, thr_hit = state
        cand_u = t_u | lax.shift_left(jnp.int32(1), 31 - i)
        cand = cand_u ^ INT_MIN
        cnt = count_ge(cand)
        hit = cnt == topk
        thr_hit = jnp.where(hit, jnp.where(done > 0.0, thr_hit, cand), thr_hit)
        done = jnp.where(hit, 1.0, done)
        return jnp.where(cnt >= topk, cand_u, t_u), done, thr_hit

    zero = jnp.zeros((1, tq), jnp.int32)
    state = (zero, jnp.zeros((1, tq), F32), zero)
    first = 0
    for last in DESCENT_CHECKS + (32,):
        stop = last if first == 0 else jnp.where(jnp.min(state[1]) < 1.0, last, first)
        state = lax.fori_loop(first, stop, bit_body, state)
        first = last
    t_u, done, thr_hit = state
    thr = jnp.where(done > 0.0, thr_hit, t_u ^ INT_MIN)

    tie = jnp.where(done > 0.0, 0.0, jnp.where(thr > NEG_INF_KEY, 1.0, 0.0))

    @pl.when(jnp.max(tie) > 0.0)
    def _():
        need = jnp.where(tie > 0.0, topk - count_ge(thr + 1), float(tk * keys_ref.shape[0] + 1))
        row = lax.broadcasted_iota(jnp.int32, (tk, tk), 0)
        col = lax.broadcasted_iota(jnp.int32, (tk, tk), 1)
        before = jnp.where(col < row, 1.0, 0.0).astype(BF16)

        def body(j, run):
            kb = keys_ref[j]
            eq = jnp.where(kb == thr, 1.0, 0.0)
            rank = _dot(before, eq.astype(BF16)) + run
            keys_ref[j] = jnp.where(kb == thr, jnp.where(rank >= need, thr - 1, kb), kb)
            return run + jnp.sum(eq, axis=0, keepdims=True)

        lax.fori_loop(0, n_blk, body, jnp.zeros((1, tq), F32))

    dqt = jnp.concatenate([dqt_ref[0, h] for h in range(DSA_HEADS)], axis=1)
    m_ref[...] = jnp.full(m_ref.shape, NEG, F32)
    acc_ref[...] = jnp.zeros(acc_ref.shape, F32)

    x_ref[...] = pair_scores(dk_ref, dqt, 0, 0, DSA_HEADS)

    def att_step(js, variant):
        n_pairs = (DSA_HEADS + 1) // 2
        items = [(j, g) for j in js for g in range(n_pairs)]
        n = len(items)

        def scores(i, _):
            if i == 0:
                return x_ref[...]
            if i == n:
                return pair_scores(dk_ref, dqt, 0, jnp.minimum(js[-1] + 1, n_blk - 1), DSA_HEADS)
            j, g = items[i]
            return pair_scores(dk_ref, dqt, g, j, DSA_HEADS)

        def running_max(i, s):
            if i == n:
                x_ref[...] = s
                return None
            j, g = items[i]
            sel = keys_ref[j] >= thr
            out = []
            for h in range(2 * g, min(2 * (g + 1), DSA_HEADS)):
                a = s[:, (h - 2 * g) * tq:(h - 2 * g + 1) * tq]
                if variant is not None:
                    a = a + tab_ref[variant, h]
                a = jnp.where(sel, a, NEG)
                m_prev = m_ref[h]
                m_new = jnp.maximum(m_prev, jnp.max(a, axis=0, keepdims=True))
                m_ref[h] = m_new
                out.append((h, a, m_prev, m_new))
            return out

        def weights_pv(i, carry):
            if i == n:
                return
            j, _ = items[i]
            vt = dvt_ref[0, :, pl.ds(pl.multiple_of(j * tk, tk), tk)]
            for h, a, m_prev, m_new in carry:
                p = jnp.exp(a - m_new)
                acc_ref[h] = jnp.exp(m_prev - m_new) * acc_ref[h] + _dot(vt, p.astype(BF16))

        _pipeline(n + 1, [scores, running_max, weights_pv])

    _for_blocks(jnp.maximum(n_blk - 2, 0), DSA_UNROLL, lambda js: att_step(js, None))

    @pl.when(jnp.logical_and(n_blk >= 2, sub == 0))
    def _():
        att_step([n_blk - 2], DSA_SUB)

    @pl.when(jnp.logical_and(n_blk >= 2, sub != 0))
    def _():
        att_step([n_blk - 2], None)

    att_step([n_blk - 1], sub)
    _finish(acc_ref, DSA_HEADS, o_ref)


def _dsa_call(iqt, ik, iwt, dqt, dk, dvt, tab, topk):
    B, _, _, S = iqt.shape
    tq = DSA_TQ
    qh = lambda n: pl.BlockSpec((1, n, HEAD, tq), lambda b, i: (b, 0, 0, i))
    res = lambda shape: pl.BlockSpec(shape, lambda b, i: (b, 0, 0), pipeline_mode=pl.Buffered(1))
    return pl.pallas_call(
        functools.partial(_dsa_kernel, topk=topk),
        grid=(B, S // tq),
        in_specs=[qh(IDX_HEADS), res((1, S, HEAD)),
                  pl.BlockSpec((1, IDX_HEADS, tq), lambda b, i: (b, 0, i)),
                  qh(DSA_HEADS), res((1, S, HEAD)), res((1, VROWS, S)), _const_spec(tab.shape)],
        out_specs=pl.BlockSpec((1, tq, OUT_W), lambda b, i: (b, i, 0)),
        out_shape=jax.ShapeDtypeStruct((B, S, OUT_W), BF16),
        scratch_shapes=[pltpu.VMEM((S // DSA_TK + (S // DSA_TK) % 2, DSA_TK, tq), jnp.int32),
                        pltpu.VMEM((DSA_HEADS, 1, tq), F32),
                        pltpu.VMEM((DSA_HEADS, VROWS, tq), F32),
                        pltpu.VMEM((DSA_TK, 2 * tq), F32)],
        compiler_params=_params("parallel", "arbitrary"),
        name="dsa",
    )(iqt, ik, iwt, dqt, dk, dvt, tab)


def _post_kernel(h_ref, om_ref, od_ref, os_ref, p_ref, wo_ref, gf_ref, wg_ref, wu_ref, wd_ref,
                 gp_ref, wpg_ref, wpp_ref, gfin_ref, out_ref, *, final):
    h = (h_ref[0] + _dot(om_ref[0], wo_ref[0]) + _dot(od_ref[0], wo_ref[1])
         + _dot(os_ref[0], wo_ref[2]))

    hf = _rms(h, gf_ref[...]).astype(BF16)
    d_ff = wg_ref.shape[1]
    ffn = jnp.zeros(h.shape, F32)
    for c0 in range(0, d_ff, FF_CHUNK):
        g = _dot(hf, wg_ref[:, c0:c0 + FF_CHUNK])
        u = _dot(hf, wu_ref[:, c0:c0 + FF_CHUNK])
        ffn = ffn + _dot((g * jax.nn.sigmoid(g) * u).astype(BF16), wd_ref[c0:c0 + FF_CHUNK, :])
    h = h + ffn

    gate = jax.nn.sigmoid(_dot(_rms(h, gp_ref[...]).astype(BF16), wpg_ref[...]))
    h = h + gate * _dot(p_ref[0, 0].astype(BF16), wpp_ref[...])
    if final:
        h = _rms(h, gfin_ref[...])
    out_ref[0] = h


def _post_call(h, om, od, os_, p, layer, wo, gf, wg, wu, wd, gp, wpg, wpp, gfin, final):
    B, S, D = h.shape
    mix = pl.BlockSpec((1, TM, OUT_W), lambda b, t: (b, t, 0))
    tok = pl.BlockSpec((1, TM, D), lambda b, t: (b, t, 0))
    return pl.pallas_call(
        functools.partial(_post_kernel, final=final),
        grid=(B, S // TM),
        in_specs=[tok, mix, mix, mix,
                  pl.BlockSpec((1, 1, TM, p.shape[-1]), lambda b, t: (layer, b, t, 0)),
                  _const_spec(wo.shape), _const_spec(gf.shape), _const_spec(wg.shape),
                  _const_spec(wu.shape), _const_spec(wd.shape), _const_spec(gp.shape),
                  _const_spec(wpg.shape), _const_spec(wpp.shape), _const_spec(gfin.shape)],
        out_specs=tok,
        out_shape=jax.ShapeDtypeStruct((B, S, D), F32),
        compiler_params=_params("parallel", "parallel"),
        name="post",
    )(h, om, od, os_, p, wo, gf, wg, wu, wd, gp, wpg, wpp, gfin)


def _swap_halves(w):
    half = w.shape[-1] // 2
    return jnp.concatenate([w[..., half:], w[..., :half]], axis=-1)


def _pack_w_in(w_in):
    L, D, _ = w_in.shape
    z = lambda n: jnp.zeros((L, D, n), F32)
    o = 0
    cols = {}
    for name, n in (("cq", MLA_Q_LORA), ("ckv", MLA_KV_LORA), ("kr", MLA_ROPE),
                    ("dq", DSA_HEADS * DSA_DIM), ("dk", DSA_DIM), ("dv", DSA_DIM),
                    ("iq", IDX_HEADS * IDX_DIM), ("ik", IDX_DIM), ("iw", IDX_HEADS),
                    ("sq", SB_HEADS * SB_DIM), ("sk", SB_HEADS * SB_DIM), ("sv", SB_HEADS * SB_DIM)):
        cols[name] = w_in[:, :, o:o + n]
        o += n
    pad_r = LANES - MLA_NOPE - MLA_ROPE
    std = jnp.concatenate([
        cols["cq"], cols["ckv"],
        z(MLA_NOPE), cols["kr"], z(pad_r),
        z(MLA_NOPE), _swap_halves(cols["kr"]), z(pad_r),
        cols["sk"], cols["ik"],
        cols["dk"], z(LANES - DSA_DIM)], axis=-1)
    tr = jnp.concatenate([
        cols["dq"] * DSA_DIM ** -0.5, cols["iq"] * IDX_DIM ** -0.5, cols["sq"] * SB_DIM ** -0.5,
        cols["sv"], cols["dv"], cols["iw"], z(ROW_IW[1] - ROW_IW[0] - IDX_HEADS)], axis=-1)
    assert std.shape[-1] == W_STD_COLS and tr.shape[-1] == W_T_ROWS
    return std.astype(BF16), jnp.swapaxes(tr, 1, 2).astype(BF16)


def _pack_mla(w_uq, w_ukv):
    L = w_uq.shape[0]
    dq = MLA_NOPE + MLA_ROPE
    pad_r = LANES - dq
    uq = w_uq.reshape(L, MLA_Q_LORA, MLA_HEADS, dq)
    zq = lambda n: jnp.zeros((L, MLA_Q_LORA, MLA_HEADS, n), F32)
    wqa = jnp.concatenate([uq, zq(pad_r)], axis=-1)
    wqb = jnp.concatenate([zq(MLA_NOPE), _swap_halves(uq[..., MLA_NOPE:]), zq(pad_r)], axis=-1)
    ukv = w_ukv.reshape(L, MLA_KV_LORA, MLA_HEADS, MLA_NOPE + MLA_V)
    wk = jnp.concatenate([ukv[..., :MLA_NOPE],
                          jnp.zeros((L, MLA_KV_LORA, MLA_HEADS, LANES - MLA_NOPE), F32)], axis=-1)
    wv = ukv[..., MLA_NOPE:]
    flat = lambda a: a.reshape(L, a.shape[1], -1)
    tr = lambda a: jnp.swapaxes(flat(a), 1, 2).astype(BF16)
    return tr(wqa), tr(wqb), flat(wk).astype(BF16), tr(wv)


def _rope_tables(S):
    half = MLA_ROPE // 2
    inv = ROPE_THETA ** (-jnp.arange(half, dtype=F32) / half)
    ang = jnp.arange(S, dtype=jnp.int32).astype(F32)[:, None] * inv[None, :]
    cos, sin = jnp.cos(ang), jnp.sin(ang)
    pad_r = LANES - MLA_NOPE - MLA_ROPE
    cos_t = jnp.concatenate([jnp.ones((S, MLA_NOPE), F32), cos, cos, jnp.zeros((S, pad_r), F32)], axis=-1)
    sin_t = jnp.concatenate([jnp.zeros((S, MLA_NOPE), F32), -sin, sin, jnp.zeros((S, pad_r), F32)], axis=-1)
    return cos_t, sin_t


def _pack_w_o(w_o):
    L, _, D = w_o.shape
    a = MLA_HEADS * MLA_V
    b = a + DSA_HEADS * DSA_DIM
    pad = jnp.zeros((L, OUT_W - DSA_HEADS * DSA_DIM, D), F32)
    blocks = [w_o[:, :a], jnp.concatenate([w_o[:, a:b], pad], axis=1),
              jnp.concatenate([w_o[:, b:], pad], axis=1)]
    return jnp.stack(blocks, axis=1).astype(BF16)


def kernel(x, p, w_in, attn_norm, mla_q_norm, mla_w_uq, mla_kv_norm, mla_w_ukv, rel_bias, w_o,
           ffn_norm, w_gate, w_up, w_down, ple_norm, w_ple_gate, w_ple_proj, final_norm):
    B, S, D = x.shape
    depth = w_in.shape[0]
    assert S % TM == 0 and S % DSA_TK == 0 and S % MLA_TK == 0
    assert w_gate.shape[-1] % FF_CHUNK == 0
    topk = min(DSA_TOPK, S // 4)

    w_std, w_tr = _pack_w_in(w_in)
    wqa, wqb, wk, wv = _pack_mla(mla_w_uq, mla_w_ukv)
    cos_s, sin_s = _rope_tables(S)
    cos_t, sin_t = cos_s.T, sin_s.T
    row = lambda g: g.reshape(depth, 1, -1)
    g_attn, g_q, g_kv, g_ffn, g_ple = map(row, (attn_norm, mla_q_norm, mla_kv_norm, ffn_norm, ple_norm))
    g_fin = final_norm.reshape(1, -1)
    wo = _pack_w_o(w_o)
    wg, wu, wd, wpg, wpp = (a.astype(BF16) for a in (w_gate, w_up, w_down, w_ple_gate, w_ple_proj))
    tri = jnp.triu(jnp.ones((SB_T, SB_T), F32), 0)
    tail = jnp.zeros((16, SB_T), F32).at[0].set(1.0)
    u = jnp.concatenate([tri, tail], axis=0).astype(BF16)

    tab = _bias_call(rel_bias)

    h = x
    for i in range(depth):
        (mqt, mk, mvt, dqt, dk, dvt, iqt, ik, iwt, sqt, sk, svt) = _proj_call(
            h, g_attn[i], w_std[i], w_tr[i], g_q[i], wqa[i], wqb[i], g_kv[i], wk[i], wv[i],
            cos_s, sin_s, cos_t, sin_t)
        o_mla = _mla_call(mqt, mk, mvt)
        o_dsa = _dsa_call(iqt, ik, iwt, dqt, dk, dvt, tab, topk)
        o_sb = _sb_call(sqt, sk, svt, u)
        h = _post_call(h, o_mla, o_dsa, o_sb, p, i, wo[i], g_ffn[i], wg[i], wu[i], wd[i],
                       g_ple[i], wpg[i], wpp[i], g_fin, final=(i == depth - 1))
    return h
```

```python
import functools
import math

import jax
import jax.numpy as jnp
from jax import lax
from jax.experimental import pallas as pl
from jax.experimental.pallas import tpu as pltpu

EPS = 1e-6
MLA_HEADS = 6
MLA_Q_LORA = 384
MLA_KV_LORA = 256
MLA_NOPE = 64
MLA_ROPE = 32
MLA_V = 64
ROPE_THETA = 10000.0
DSA_HEADS = 5
DSA_DIM = 64
IDX_HEADS = 8
IDX_DIM = 64
DSA_TOPK = 256
SB_HEADS = 5
SB_DIM = 64
REL_BUCKETS = 32
REL_MAX_DIST = 128

LANES = 128
HEAD = 64
VROWS = 80
OUT_W = 384
VMEM_LIMIT = 56 * 1024 * 1024

TM = 512
MLA_TQ, MLA_TK = 256, 512
MLA_UNROLL = 4
SB_T = 256
SB_UNROLL = 4
DSA_TQ, DSA_TK = 128, 512
DSA_SUB = DSA_TK // DSA_TQ
DSA_UNROLL = 4
COUNT_ACC = 8
DESCENT_CHECKS = (24, 28)
SLAB = 8
SORT_GROUP = 7
SORT_NETWORK = ((0, 6), (2, 3), (4, 5), (0, 2), (1, 4), (3, 6), (0, 1), (2, 5), (3, 4),
                (1, 2), (4, 6), (2, 3), (4, 5), (1, 2), (3, 4), (5, 6))
FF_CHUNK = 1408

NEG = -1e30
INT_MIN = -2 ** 31
NEG_INF_KEY = -2139095041
LOG2E = math.log2(math.e)

F32 = jnp.float32
BF16 = jnp.bfloat16

SEG_CQ = (0, 384)
SEG_CKV = (384, 640)
SEG_KRA = (640, 768)
SEG_KRB = (768, 896)
SEG_SK_IK = (896, 1280)
SEG_DK = (1280, 1408)
W_STD_COLS = 1408
ROW_DQ = (0, 320)
ROW_IQ = (320, 832)
ROW_SQ = (832, 1152)
ROW_SV = (1152, 1472)
ROW_DV = (1472, 1536)
ROW_IW = (1536, 1552)
W_T_ROWS = 1552


def _rms(x, g):
    return x * lax.rsqrt(jnp.mean(x * x, axis=-1, keepdims=True) + EPS) * g


def _dot(a, b):
    return jnp.dot(a, b, preferred_element_type=F32)


def _const_spec(shape):
    n = len(shape)
    return pl.BlockSpec(shape, lambda *_: (0,) * n, pipeline_mode=pl.Buffered(1))


def _params(*sem):
    return pltpu.CompilerParams(dimension_semantics=sem, vmem_limit_bytes=VMEM_LIMIT)


def _pipeline(n, stages):
    held = [dict() for _ in stages]
    for t in range(n + len(stages) - 1):
        for s, stage in enumerate(stages):
            i = t - s
            if 0 <= i < n:
                held[s][i] = stage(i, held[s - 1].pop(i) if s else None)


def _for_blocks(n, unroll, step):
    def body(i, c):
        step([unroll * i + d for d in range(unroll)])
        return c

    lax.fori_loop(0, n // unroll, body, 0)
    rem = n % unroll
    group = unroll // 2
    while group:
        @pl.when(rem & group != 0)
        def _(group=group):
            base = n - (rem & (2 * group - 1))
            step([base + d for d in range(group)])
        group //= 2


def _ones_rows(n):
    r = lax.broadcasted_iota(jnp.int32, (VROWS - HEAD, n), 0)
    return jnp.where(r == 0, 1.0, 0.0).astype(BF16)


def _proj_kernel(h_ref, g_ref, w_ref, wt_ref, gq_ref, wqa_ref, wqb_ref, gkv_ref, wk_ref, wv_ref,
                 cos_ref, sin_ref, cost_ref, sint_ref,
                 mqt_ref, mk_ref, mvt_ref, dqt_ref, dk_ref, dvt_ref, iqt_ref, ik_ref, iwt_ref,
                 sqt_ref, sk_ref, svt_ref):
    hn = _rms(h_ref[0], g_ref[...])
    hnt = hn.T.astype(BF16)
    hn = hn.astype(BF16)
    tm = hn.shape[0]
    ones = _ones_rows(tm)

    def seg(ab):
        return _dot(hn, w_ref[:, ab[0]:ab[1]])

    def seg_t(ab):
        return _dot(wt_ref[ab[0]:ab[1], :], hnt)

    cqn_t = _rms(seg(SEG_CQ), gq_ref[...]).T.astype(BF16)
    qa = _dot(wqa_ref[...], cqn_t)
    qb = _dot(wqb_ref[...], cqn_t)
    cos_t = cost_ref[...]
    sin_t = sint_ref[...]
    for h in range(MLA_HEADS):
        sl = slice(h * LANES, (h + 1) * LANES)
        mqt_ref[0, h] = (qa[sl] * cos_t + qb[sl] * sin_t).astype(BF16)

    ckvn = _rms(seg(SEG_CKV), gkv_ref[...])
    ckvn_t = ckvn.T.astype(BF16)
    ckvn = ckvn.astype(BF16)
    kn = _dot(ckvn, wk_ref[...])
    kr = seg(SEG_KRA) * cos_ref[...] + seg(SEG_KRB) * sin_ref[...]
    for h in range(MLA_HEADS):
        mk_ref[0, h] = (kn[:, h * LANES:(h + 1) * LANES] + kr).astype(BF16)
    vt = _dot(wv_ref[...], ckvn_t)
    for h in range(MLA_HEADS):
        mvt_ref[0, h, 0:HEAD, :] = vt[h * HEAD:(h + 1) * HEAD].astype(BF16)
        mvt_ref[0, h, HEAD:VROWS, :] = ones

    a = seg(SEG_SK_IK)
    for h in range(SB_HEADS):
        sk_ref[0, h] = a[:, h * HEAD:(h + 1) * HEAD].astype(BF16)
    ik_ref[0] = a[:, 320:384].astype(BF16)
    dk_ref[0] = seg(SEG_DK)[:, 0:HEAD].astype(BF16)

    a = seg_t(ROW_DQ)
    for h in range(DSA_HEADS):
        dqt_ref[0, h] = a[h * HEAD:(h + 1) * HEAD].astype(BF16)
    a = seg_t(ROW_IQ)
    for h in range(IDX_HEADS):
        iqt_ref[0, h] = a[h * HEAD:(h + 1) * HEAD].astype(BF16)
    a = seg_t(ROW_SQ)
    for h in range(SB_HEADS):
        sqt_ref[0, h] = a[h * HEAD:(h + 1) * HEAD].astype(BF16)
    a = seg_t(ROW_SV)
    for h in range(SB_HEADS):
        svt_ref[0, h] = a[h * HEAD:(h + 1) * HEAD].astype(BF16)
    dvt_ref[0, 0:HEAD, :] = seg_t(ROW_DV).astype(BF16)
    dvt_ref[0, HEAD:VROWS, :] = ones
    iwt_ref[0] = seg_t(ROW_IW)[0:IDX_HEADS] * IDX_HEADS ** -0.5


def _proj_call(h, g, w, wt, gq, wqa, wqb, gkv, wk, wv, cos_s, sin_s, cos_t, sin_t):
    B, S, D = h.shape
    nt = S // TM
    tok = lambda w_: pl.BlockSpec((1, TM, w_), lambda b, t: (b, t, 0))
    heads = lambda n, w_: pl.BlockSpec((1, n, TM, w_), lambda b, t: (b, 0, t, 0))
    heads_t = lambda n, r: pl.BlockSpec((1, n, r, TM), lambda b, t: (b, 0, 0, t))
    rows_t = lambda r: pl.BlockSpec((1, r, TM), lambda b, t: (b, 0, t))
    hs = lambda n, w_: jax.ShapeDtypeStruct((B, n, S, w_), BF16)
    hts = lambda n, r: jax.ShapeDtypeStruct((B, n, r, S), BF16)
    return pl.pallas_call(
        _proj_kernel,
        grid=(B, nt),
        in_specs=[tok(D), _const_spec(g.shape), _const_spec(w.shape), _const_spec(wt.shape),
                  _const_spec(gq.shape), _const_spec(wqa.shape), _const_spec(wqb.shape),
                  _const_spec(gkv.shape), _const_spec(wk.shape), _const_spec(wv.shape),
                  pl.BlockSpec((TM, LANES), lambda b, t: (t, 0)),
                  pl.BlockSpec((TM, LANES), lambda b, t: (t, 0)),
                  pl.BlockSpec((LANES, TM), lambda b, t: (0, t)),
                  pl.BlockSpec((LANES, TM), lambda b, t: (0, t))],
        out_specs=[heads_t(MLA_HEADS, LANES), heads(MLA_HEADS, LANES), heads_t(MLA_HEADS, VROWS),
                   heads_t(DSA_HEADS, HEAD), tok(HEAD), rows_t(VROWS),
                   heads_t(IDX_HEADS, HEAD), tok(HEAD), rows_t(IDX_HEADS),
                   heads_t(SB_HEADS, HEAD), heads(SB_HEADS, HEAD), heads_t(SB_HEADS, HEAD)],
        out_shape=[hts(MLA_HEADS, LANES), hs(MLA_HEADS, LANES), hts(MLA_HEADS, VROWS),
                   hts(DSA_HEADS, HEAD), jax.ShapeDtypeStruct((B, S, HEAD), BF16),
                   jax.ShapeDtypeStruct((B, VROWS, S), BF16),
                   hts(IDX_HEADS, HEAD), jax.ShapeDtypeStruct((B, S, HEAD), BF16),
                   jax.ShapeDtypeStruct((B, IDX_HEADS, S), F32),
                   hts(SB_HEADS, HEAD), hs(SB_HEADS, HEAD), hts(SB_HEADS, HEAD)],
        compiler_params=_params("parallel", "parallel"),
        name="proj",
    )(h, g, w, wt, gq, wqa, wqb, gkv, wk, wv, cos_s, sin_s, cos_t, sin_t)


def _finish(acc_ref, n_heads, o_ref):
    outs = [acc_ref[h, 0:HEAD, :] / acc_ref[h, HEAD:HEAD + 1, :] for h in range(n_heads)]
    pad = OUT_W - n_heads * HEAD
    if pad:
        outs.append(jnp.zeros((pad, outs[0].shape[1]), F32))
    o_ref[0] = jnp.concatenate(outs, axis=0).T.astype(BF16)


def _mla_kernel(qt_ref, k_ref, vt_ref, o_ref, m_ref, acc_ref, s_ref):
    qi = pl.program_id(1)
    tq, tk = MLA_TQ, MLA_TK
    m_ref[...] = jnp.full(m_ref.shape, NEG, F32)
    acc_ref[...] = jnp.zeros(acc_ref.shape, F32)
    scale2 = (MLA_NOPE + MLA_ROPE) ** -0.5 * LOG2E
    n_full = (qi * tq) // tk
    n_all = ((qi + 1) * tq + tk - 1) // tk

    def qk(h, j):
        k0 = pl.multiple_of(j * tk, tk)
        half = tk // 4
        return jnp.concatenate(
            [_dot(k_ref[0, h, pl.ds(k0 + i * half, half), :], qt_ref[0, h]) for i in range(4)],
            axis=0)

    s_ref[...] = qk(0, 0)

    def step(js, masked):
        items = [(j, h) for j in js for h in range(MLA_HEADS)]
        n = len(items)

        def scores(i, _):
            if i == 0:
                return s_ref[...]
            if i == n:
                return qk(0, jnp.minimum(js[-1] + 1, n_all - 1))
            j, h = items[i]
            return qk(h, j)

        def running_max(i, s):
            if i == n:
                s_ref[...] = s
                return None
            j, h = items[i]
            s = s * scale2
            if masked:
                kpos = j * tk + lax.broadcasted_iota(jnp.int32, (tk, tq), 0)
                qpos = qi * tq + lax.broadcasted_iota(jnp.int32, (tk, tq), 1)
                s = jnp.where(kpos <= qpos, s, NEG)
            m_prev = m_ref[h]
            m_new = jnp.maximum(m_prev, jnp.max(s, axis=0, keepdims=True))
            m_ref[h] = m_new
            return s, m_prev, m_new

        def weights_pv(i, carry):
            if i == n:
                return
            j, h = items[i]
            s, m_prev, m_new = carry
            p = jnp.exp2(s - m_new)
            k0 = pl.multiple_of(j * tk, tk)
            acc_ref[h] = (jnp.exp2(m_prev - m_new) * acc_ref[h]
                          + _dot(vt_ref[0, h, :, pl.ds(k0, tk)], p.astype(BF16)))

        _pipeline(n + 1, [scores, running_max, weights_pv])

    _for_blocks(n_full, MLA_UNROLL, lambda js: step(js, False))
    step([n_full], True)
    _finish(acc_ref, MLA_HEADS, o_ref)


def _mla_call(qt, k, vt):
    B, H, _, S = qt.shape
    res = lambda shape: pl.BlockSpec(shape, lambda b, i: (b, 0, 0, 0), pipeline_mode=pl.Buffered(1))
    return pl.pallas_call(
        _mla_kernel,
        grid=(B, S // MLA_TQ),
        in_specs=[pl.BlockSpec((1, H, LANES, MLA_TQ), lambda b, i: (b, 0, 0, i)),
                  res((1, H, S, LANES)), res((1, H, VROWS, S))],
        out_specs=pl.BlockSpec((1, MLA_TQ, OUT_W), lambda b, i: (b, i, 0)),
        out_shape=jax.ShapeDtypeStruct((B, S, OUT_W), BF16),
        scratch_shapes=[pltpu.VMEM((H, 1, MLA_TQ), F32), pltpu.VMEM((H, VROWS, MLA_TQ), F32),
                        pltpu.VMEM((MLA_TK, MLA_TQ), F32)],
        compiler_params=_params("parallel", "arbitrary"),
        name="mla",
    )(qt, k, vt)


def _sb_kernel(qt_ref, k_ref, vt_ref, u_ref, o_ref, c_ref, acc_ref, z_ref):
    qi = pl.program_id(1)
    t = SB_T
    c_ref[...] = jnp.zeros(c_ref.shape, F32)
    acc_ref[...] = jnp.zeros(acc_ref.shape, F32)

    def qk(h, j):
        k0 = pl.multiple_of(j * t, t)
        half = t // 2
        return jnp.concatenate(
            [_dot(k_ref[0, h, pl.ds(k0 + s * half, half), :], qt_ref[0, h]) for s in range(2)],
            axis=0)

    strict = (lax.broadcasted_iota(jnp.int32, (t, t), 0) < lax.broadcasted_iota(jnp.int32, (t, t), 1))

    def suffix_sums(z, masked):
        z2 = z * LOG2E
        neg_abs = pltpu.bitcast(pltpu.bitcast(z2, jnp.int32) | INT_MIN, F32)
        nl = jnp.maximum(z2, 0.0) + jnp.log2(1.0 + jnp.exp2(neg_abs))
        if masked:
            nl = jnp.where(strict, nl, 0.0)
        nlb = nl.astype(BF16)
        sums = jnp.concatenate([_dot(u_ref[0:t // 2, :], nlb), _dot(u_ref[t // 2:, :], nlb)], axis=0)
        return z2, sums

    z_ref[...] = qk(0, qi)

    def step(js, masked):
        items = [(j, h) for j in js for h in range(SB_HEADS)]
        n = len(items)

        def logits(i, _):
            if i == 0:
                return z_ref[...]
            if i == n:
                return qk(0, jnp.maximum(js[-1] - 1, 0))
            j, h = items[i]
            return qk(h, j)

        def sums_stage(i, z):
            if i == n:
                z_ref[...] = z
                return None
            return suffix_sums(z, masked)

        def weights_pv(i, carry):
            if i == n:
                return
            j, h = items[i]
            z2, sums = carry
            w = jnp.exp2(z2 - sums[0:t])
            if masked:
                w = jnp.where(strict, w, 0.0)
            k0 = pl.multiple_of(j * t, t)
            acc_ref[h] += jnp.exp2(-c_ref[h]) * _dot(vt_ref[0, h, :, pl.ds(k0, t)], w.astype(BF16))
            c_ref[h] += sums[t:t + 1]

        _pipeline(n + 1, [logits, sums_stage, weights_pv])

    step([qi], True)

    _for_blocks(qi, SB_UNROLL, lambda idx: step([qi - 1 - i for i in idx], False))

    outs = [acc_ref[h] for h in range(SB_HEADS)]
    outs.append(jnp.zeros((OUT_W - SB_HEADS * HEAD, t), F32))
    o_ref[0] = jnp.concatenate(outs, axis=0).T.astype(BF16)


def _sb_call(qt, k, vt, u):
    B, H, _, S = qt.shape
    res = lambda shape: pl.BlockSpec(shape, lambda b, i: (b, 0, 0, 0), pipeline_mode=pl.Buffered(1))
    return pl.pallas_call(
        _sb_kernel,
        grid=(B, S // SB_T),
        in_specs=[pl.BlockSpec((1, H, HEAD, SB_T), lambda b, i: (b, 0, 0, i)),
                  res((1, H, S, HEAD)), res((1, H, HEAD, S)), _const_spec(u.shape)],
        out_specs=pl.BlockSpec((1, SB_T, OUT_W), lambda b, i: (b, i, 0)),
        out_shape=jax.ShapeDtypeStruct((B, S, OUT_W), BF16),
        scratch_shapes=[pltpu.VMEM((H, 1, SB_T), F32), pltpu.VMEM((H, HEAD, SB_T), F32),
                        pltpu.VMEM((SB_T, SB_T), F32)],
        compiler_params=_params("parallel", "arbitrary"),
        name="stickbreak",
    )(qt, k, vt, u)


def _bias_kernel(rb_ref, o_ref):
    v = pl.program_id(0)
    shape = (DSA_TK, DSA_TQ)
    off = jnp.where(v < DSA_SUB, v * DSA_TQ, DSA_TK)
    dist = off + lax.broadcasted_iota(jnp.int32, shape, 1) - lax.broadcasted_iota(jnp.int32, shape, 0)
    max_exact = REL_BUCKETS // 2
    d = jnp.maximum(dist, 1).astype(F32)
    large = max_exact + (jnp.log(d / max_exact) / math.log(REL_MAX_DIST / max_exact)
                         * (REL_BUCKETS - max_exact)).astype(jnp.int32)
    large = jnp.minimum(large, REL_BUCKETS - 1)
    bucket = jnp.where(dist < max_exact, dist, large)
    for h in range(DSA_HEADS):
        t = jnp.zeros(shape, F32)
        for b in range(REL_BUCKETS):
            t = jnp.where(bucket == b, rb_ref[b, h] - rb_ref[REL_BUCKETS - 1, h], t)
        o_ref[0, h] = jnp.where(dist >= 0, t, NEG)


def _bias_call(rel_bias):
    nv = DSA_SUB + 1
    return pl.pallas_call(
        _bias_kernel,
        grid=(nv,),
        in_specs=[pl.BlockSpec(memory_space=pltpu.SMEM)],
        out_specs=pl.BlockSpec((1, DSA_HEADS, DSA_TK, DSA_TQ), lambda v: (v, 0, 0, 0)),
        out_shape=jax.ShapeDtypeStruct((nv, DSA_HEADS, DSA_TK, DSA_TQ), F32),
        compiler_params=_params("arbitrary"),
        name="bias_tiles",
    )(rel_bias)


def _dsa_kernel(iqt_ref, ik_ref, iwt_ref, dqt_ref, dk_ref, dvt_ref, tab_ref, o_ref,
                keys_ref, sorted_ref, m_ref, acc_ref, x_ref, *, topk):
    qi = pl.program_id(1)
    n_blk = qi // DSA_SUB + 1
    sub = qi % DSA_SUB
    tq, tk = DSA_TQ, DSA_TK

    iqt = jnp.concatenate([iqt_ref[0, h] for h in range(IDX_HEADS)], axis=1)
    iw = iwt_ref[0]

    def pair_scores(k_ref_, qt_all, g, j, n_heads):
        k0 = pl.multiple_of(j * tk, tk)
        cols = qt_all[:, 2 * g * tq:min(2 * (g + 1), n_heads) * tq]
        half = tk // 2
        return jnp.concatenate(
            [_dot(k_ref_[0, pl.ds(k0 + i * half, half), :], cols) for i in range(2)], axis=0)

    n_slab = tk // SLAB
    n_group = n_slab // SORT_GROUP

    def sort_groups(key):
        slabs = [key[SLAB * i:SLAB * (i + 1)] for i in range(n_slab)]
        for g in range(n_group):
            v = slabs[SORT_GROUP * g:SORT_GROUP * (g + 1)]
            for a, b in SORT_NETWORK:
                v[a], v[b] = jnp.maximum(v[a], v[b]), jnp.minimum(v[a], v[b])
            slabs[SORT_GROUP * g:SORT_GROUP * (g + 1)] = v
        return jnp.concatenate(slabs, axis=0)

    x_ref[...] = pair_scores(ik_ref, iqt, 0, 0, IDX_HEADS)

    def idx_step(js, masked):
        n_pairs = IDX_HEADS // 2
        items = [(j, g) for j in js for g in range(n_pairs)]
        n = len(items)
        total = {}

        def scores(i, _):
            if i == 0:
                return x_ref[...]
            if i == n:
                return pair_scores(ik_ref, iqt, 0, jnp.minimum(js[-1] + 1, n_blk - 1), IDX_HEADS)
            j, g = items[i]
            return pair_scores(ik_ref, iqt, g, j, IDX_HEADS)

        def weigh(i, x):
            if i == n:
                x_ref[...] = x
                return
            j, g = items[i]
            part = (jnp.maximum(x[:, 0:tq], 0.0) * iw[2 * g:2 * g + 1]
                    + jnp.maximum(x[:, tq:2 * tq], 0.0) * iw[2 * g + 1:2 * g + 2])
            blk = i // n_pairs
            total[blk] = part if g == 0 else total[blk] + part
            if g < n_pairs - 1:
                return
            isc = total.pop(blk)
            isc = jnp.where(isc == 0.0, 0.0, isc)
            if masked:
                kpos = j * tk + lax.broadcasted_iota(jnp.int32, isc.shape, 0)
                qpos = qi * tq + lax.broadcasted_iota(jnp.int32, isc.shape, 1)
                isc = jnp.where(kpos <= qpos, isc, -jnp.inf)
            bits = pltpu.bitcast(isc, jnp.int32)
            key = bits ^ (lax.shift_right_arithmetic(bits, 31) & 0x7FFFFFFF)
            keys_ref[j] = key
            sorted_ref[j] = sort_groups(key)

        _pipeline(n + 1, [scores, weigh])

    _for_blocks(n_blk - 1, DSA_UNROLL, lambda js: idx_step(js, False))
    idx_step([n_blk - 1], True)

    @pl.when(n_blk % 2 == 1)
    def _():
        keys_ref[n_blk] = jnp.full((tk, tq), INT_MIN, jnp.int32)
        sorted_ref[n_blk] = jnp.full((tk, tq), INT_MIN, jnp.int32)

    def count_ge(cand):
        def body(i, acc):
            for j in (2 * i, 2 * i + 1):
                c = jnp.where(keys_ref[j] >= cand, 1.0, 0.0)
                acc = acc + jnp.sum(c.reshape(COUNT_ACC, tk // COUNT_ACC, tq), axis=0)
            return acc
        acc = lax.fori_loop(0, (n_blk + 1) // 2, body, jnp.zeros((tk // COUNT_ACC, tq), F32))
        return jnp.sum(acc, axis=0, keepdims=True)

    def count_ge_sorted(cand):
        def one(flag):
            return jnp.where(flag, 1.0, 0.0)

        def body(i, accs):
            accs = list(accs)
            for j in (2 * i, 2 * i + 1):
                blk = sorted_ref[j]
                for g in range(n_group):
                    s = [blk[SLAB * (SORT_GROUP * g + k):SLAB * (SORT_GROUP * g + k + 1)]
                         for k in range(SORT_GROUP)]
                    b4 = s[3] >= cand
                    b2 = jnp.where(b4, s[5], s[1]) >= cand
                    b1 = jnp.where(b4, jnp.where(b2, s[6], s[4]), jnp.where(b2, s[2], s[0])) >= cand
                    o = 3 * (g % 2)
                    accs[o] += one(b4)
                    accs[o + 1] += one(b2)
                    accs[o + 2] += one(b1)
                for r in range(SORT_GROUP * n_group, n_slab):
                    accs[2] += one(blk[SLAB * r:SLAB * (r + 1)] >= cand)
            return tuple(accs)

        z = jnp.zeros((SLAB, tq), F32)
        a = lax.fori_loop(0, (n_blk + 1) // 2, body, (z,) * 6)
        total = 4.0 * (a[0] + a[3]) + 2.0 * (a[1] + a[4]) + (a[2] + a[5])
        return jnp.sum(total, axis=0, keepdims=True)

    def bit_body(i, state):
        t_u, done, thr_hit = state
        cand_u = t_u | lax.shift_left(jnp.int32(1), 31 - i)
        cand = cand_u ^ INT_MIN
        cnt = count_ge_sorted(cand)
        hit = cnt == topk
        thr_hit = jnp.where(hit, jnp.where(done > 0.0, thr_hit, cand), thr_hit)
        done = jnp.where(hit, 1.0, done)
        return jnp.where(cnt >= topk, cand_u, t_u), done, thr_hit

    zero = jnp.zeros((1, tq), jnp.int32)
    state = (zero, jnp.zeros((1, tq), F32), zero)
    first = 0
    for last in DESCENT_CHECKS + (32,):
        stop = last if first == 0 else jnp.where(jnp.min(state[1]) < 1.0, last, first)
        state = lax.fori_loop(first, stop, bit_body, state)
        first = last
    t_u, done, thr_hit = state
    thr = jnp.where(done > 0.0, thr_hit, t_u ^ INT_MIN)

    tie = jnp.where(done > 0.0, 0.0, jnp.where(thr > NEG_INF_KEY, 1.0, 0.0))

    @pl.when(jnp.max(tie) > 0.0)
    def _():
        need = jnp.where(tie > 0.0, topk - count_ge(thr + 1), float(tk * keys_ref.shape[0] + 1))
        row = lax.broadcasted_iota(jnp.int32, (tk, tk), 0)
        col = lax.broadcasted_iota(jnp.int32, (tk, tk), 1)
        before = jnp.where(col < row, 1.0, 0.0).astype(BF16)

        def body(j, run):
            kb = keys_ref[j]
            eq = jnp.where(kb == thr, 1.0, 0.0)
            rank = _dot(before, eq.astype(BF16)) + run
            keys_ref[j] = jnp.where(kb == thr, jnp.where(rank >= need, thr - 1, kb), kb)
            return run + jnp.sum(eq, axis=0, keepdims=True)

        lax.fori_loop(0, n_blk, body, jnp.zeros((1, tq), F32))

    dqt = jnp.concatenate([dqt_ref[0, h] for h in range(DSA_HEADS)], axis=1)
    m_ref[...] = jnp.full(m_ref.shape, NEG, F32)
    acc_ref[...] = jnp.zeros(acc_ref.shape, F32)

    x_ref[...] = pair_scores(dk_ref, dqt, 0, 0, DSA_HEADS)

    def att_step(js, variant):
        n_pairs = (DSA_HEADS + 1) // 2
        items = [(j, g) for j in js for g in range(n_pairs)]
        n = len(items)

        def scores(i, _):
            if i == 0:
                return x_ref[...]
            if i == n:
                return pair_scores(dk_ref, dqt, 0, jnp.minimum(js[-1] + 1, n_blk - 1), DSA_HEADS)
            j, g = items[i]
            return pair_scores(dk_ref, dqt, g, j, DSA_HEADS)

        def running_max(i, s):
            if i == n:
                x_ref[...] = s
                return None
            j, g = items[i]
            sel = keys_ref[j] >= thr
            out = []
            for h in range(2 * g, min(2 * (g + 1), DSA_HEADS)):
                a = s[:, (h - 2 * g) * tq:(h - 2 * g + 1) * tq]
                if variant is not None:
                    a = a + tab_ref[variant, h]
                a = jnp.where(sel, a, NEG)
                m_prev = m_ref[h]
                m_new = jnp.maximum(m_prev, jnp.max(a, axis=0, keepdims=True))
                m_ref[h] = m_new
                out.append((h, a, m_prev, m_new))
            return out

        def weights_pv(i, carry):
            if i == n:
                return
            j, _ = items[i]
            vt = dvt_ref[0, :, pl.ds(pl.multiple_of(j * tk, tk), tk)]
            for h, a, m_prev, m_new in carry:
                p = jnp.exp(a - m_new)
                acc_ref[h] = jnp.exp(m_prev - m_new) * acc_ref[h] + _dot(vt, p.astype(BF16))

        _pipeline(n + 1, [scores, running_max, weights_pv])

    _for_blocks(jnp.maximum(n_blk - 2, 0), DSA_UNROLL, lambda js: att_step(js, None))

    @pl.when(jnp.logical_and(n_blk >= 2, sub == 0))
    def _():
        att_step([n_blk - 2], DSA_SUB)

    @pl.when(jnp.logical_and(n_blk >= 2, sub != 0))
    def _():
        att_step([n_blk - 2], None)

    att_step([n_blk - 1], sub)
    _finish(acc_ref, DSA_HEADS, o_ref)


def _dsa_call(iqt, ik, iwt, dqt, dk, dvt, tab, topk):
    B, _, _, S = iqt.shape
    tq = DSA_TQ
    qh = lambda n: pl.BlockSpec((1, n, HEAD, tq), lambda b, i: (b, 0, 0, i))
    res = lambda shape: pl.BlockSpec(shape, lambda b, i: (b, 0, 0), pipeline_mode=pl.Buffered(1))
    return pl.pallas_call(
        functools.partial(_dsa_kernel, topk=topk),
        grid=(B, S // tq),
        in_specs=[qh(IDX_HEADS), res((1, S, HEAD)),
                  pl.BlockSpec((1, IDX_HEADS, tq), lambda b, i: (b, 0, i)),
                  qh(DSA_HEADS), res((1, S, HEAD)), res((1, VROWS, S)), _const_spec(tab.shape)],
        out_specs=pl.BlockSpec((1, tq, OUT_W), lambda b, i: (b, i, 0)),
        out_shape=jax.ShapeDtypeStruct((B, S, OUT_W), BF16),
        scratch_shapes=[pltpu.VMEM((S // DSA_TK + (S // DSA_TK) % 2, DSA_TK, tq), jnp.int32),
                        pltpu.VMEM((S // DSA_TK + (S // DSA_TK) % 2, DSA_TK, tq), jnp.int32),
                        pltpu.VMEM((DSA_HEADS, 1, tq), F32),
                        pltpu.VMEM((DSA_HEADS, VROWS, tq), F32),
                        pltpu.VMEM((DSA_TK, 2 * tq), F32)],
        compiler_params=_params("parallel", "arbitrary"),
        name="dsa",
    )(iqt, ik, iwt, dqt, dk, dvt, tab)


def _post_kernel(h_ref, om_ref, od_ref, os_ref, p_ref, wo_ref, gf_ref, wg_ref, wu_ref, wd_ref,
                 gp_ref, wpg_ref, wpp_ref, gfin_ref, out_ref, *, final):
    h = (h_ref[0] + _dot(om_ref[0], wo_ref[0]) + _dot(od_ref[0], wo_ref[1])
         + _dot(os_ref[0], wo_ref[2]))

    hf = _rms(h, gf_ref[...]).astype(BF16)
    d_ff = wg_ref.shape[1]
    ffn = jnp.zeros(h.shape, F32)
    for c0 in range(0, d_ff, FF_CHUNK):
        g = _dot(hf, wg_ref[:, c0:c0 + FF_CHUNK])
        u = _dot(hf, wu_ref[:, c0:c0 + FF_CHUNK])
        ffn = ffn + _dot((g * jax.nn.sigmoid(g) * u).astype(BF16), wd_ref[c0:c0 + FF_CHUNK, :])
    h = h + ffn

    gate = jax.nn.sigmoid(_dot(_rms(h, gp_ref[...]).astype(BF16), wpg_ref[...]))
    h = h + gate * _dot(p_ref[0, 0].astype(BF16), wpp_ref[...])
    if final:
        h = _rms(h, gfin_ref[...])
    out_ref[0] = h


def _post_call(h, om, od, os_, p, layer, wo, gf, wg, wu, wd, gp, wpg, wpp, gfin, final):
    B, S, D = h.shape
    mix = pl.BlockSpec((1, TM, OUT_W), lambda b, t: (b, t, 0))
    tok = pl.BlockSpec((1, TM, D), lambda b, t: (b, t, 0))
    return pl.pallas_call(
        functools.partial(_post_kernel, final=final),
        grid=(B, S // TM),
        in_specs=[tok, mix, mix, mix,
                  pl.BlockSpec((1, 1, TM, p.shape[-1]), lambda b, t: (layer, b, t, 0)),
                  _const_spec(wo.shape), _const_spec(gf.shape), _const_spec(wg.shape),
                  _const_spec(wu.shape), _const_spec(wd.shape), _const_spec(gp.shape),
                  _const_spec(wpg.shape), _const_spec(wpp.shape), _const_spec(gfin.shape)],
        out_specs=tok,
        out_shape=jax.ShapeDtypeStruct((B, S, D), F32),
        compiler_params=_params("parallel", "parallel"),
        name="post",
    )(h, om, od, os_, p, wo, gf, wg, wu, wd, gp, wpg, wpp, gfin)


def _swap_halves(w):
    half = w.shape[-1] // 2
    return jnp.concatenate([w[..., half:], w[..., :half]], axis=-1)


def _pack_w_in(w_in):
    L, D, _ = w_in.shape
    z = lambda n: jnp.zeros((L, D, n), F32)
    o = 0
    cols = {}
    for name, n in (("cq", MLA_Q_LORA), ("ckv", MLA_KV_LORA), ("kr", MLA_ROPE),
                    ("dq", DSA_HEADS * DSA_DIM), ("dk", DSA_DIM), ("dv", DSA_DIM),
                    ("iq", IDX_HEADS * IDX_DIM), ("ik", IDX_DIM), ("iw", IDX_HEADS),
                    ("sq", SB_HEADS * SB_DIM), ("sk", SB_HEADS * SB_DIM), ("sv", SB_HEADS * SB_DIM)):
        cols[name] = w_in[:, :, o:o + n]
        o += n
    pad_r = LANES - MLA_NOPE - MLA_ROPE
    std = jnp.concatenate([
        cols["cq"], cols["ckv"],
        z(MLA_NOPE), cols["kr"], z(pad_r),
        z(MLA_NOPE), _swap_halves(cols["kr"]), z(pad_r),
        cols["sk"], cols["ik"],
        cols["dk"], z(LANES - DSA_DIM)], axis=-1)
    tr = jnp.concatenate([
        cols["dq"] * DSA_DIM ** -0.5, cols["iq"] * IDX_DIM ** -0.5, cols["sq"] * SB_DIM ** -0.5,
        cols["sv"], cols["dv"], cols["iw"], z(ROW_IW[1] - ROW_IW[0] - IDX_HEADS)], axis=-1)
    assert std.shape[-1] == W_STD_COLS and tr.shape[-1] == W_T_ROWS
    return std.astype(BF16), jnp.swapaxes(tr, 1, 2).astype(BF16)


def _pack_mla(w_uq, w_ukv):
    L = w_uq.shape[0]
    dq = MLA_NOPE + MLA_ROPE
    pad_r = LANES - dq
    uq = w_uq.reshape(L, MLA_Q_LORA, MLA_HEADS, dq)
    zq = lambda n: jnp.zeros((L, MLA_Q_LORA, MLA_HEADS, n), F32)
    wqa = jnp.concatenate([uq, zq(pad_r)], axis=-1)
    wqb = jnp.concatenate([zq(MLA_NOPE), _swap_halves(uq[..., MLA_NOPE:]), zq(pad_r)], axis=-1)
    ukv = w_ukv.reshape(L, MLA_KV_LORA, MLA_HEADS, MLA_NOPE + MLA_V)
    wk = jnp.concatenate([ukv[..., :MLA_NOPE],
                          jnp.zeros((L, MLA_KV_LORA, MLA_HEADS, LANES - MLA_NOPE), F32)], axis=-1)
    wv = ukv[..., MLA_NOPE:]
    flat = lambda a: a.reshape(L, a.shape[1], -1)
    tr = lambda a: jnp.swapaxes(flat(a), 1, 2).astype(BF16)
    return tr(wqa), tr(wqb), flat(wk).astype(BF16), tr(wv)


def _rope_tables(S):
    half = MLA_ROPE // 2
    inv = ROPE_THETA ** (-jnp.arange(half, dtype=F32) / half)
    ang = jnp.arange(S, dtype=jnp.int32).astype(F32)[:, None] * inv[None, :]
    cos, sin = jnp.cos(ang), jnp.sin(ang)
    pad_r = LANES - MLA_NOPE - MLA_ROPE
    cos_t = jnp.concatenate([jnp.ones((S, MLA_NOPE), F32), cos, cos, jnp.zeros((S, pad_r), F32)], axis=-1)
    sin_t = jnp.concatenate([jnp.zeros((S, MLA_NOPE), F32), -sin, sin, jnp.zeros((S, pad_r), F32)], axis=-1)
    return cos_t, sin_t


def _pack_w_o(w_o):
    L, _, D = w_o.shape
    a = MLA_HEADS * MLA_V
    b = a + DSA_HEADS * DSA_DIM
    pad = jnp.zeros((L, OUT_W - DSA_HEADS * DSA_DIM, D), F32)
    blocks = [w_o[:, :a], jnp.concatenate([w_o[:, a:b], pad], axis=1),
              jnp.concatenate([w_o[:, b:], pad], axis=1)]
    return jnp.stack(blocks, axis=1).astype(BF16)


def kernel(x, p, w_in, attn_norm, mla_q_norm, mla_w_uq, mla_kv_norm, mla_w_ukv, rel_bias, w_o,
           ffn_norm, w_gate, w_up, w_down, ple_norm, w_ple_gate, w_ple_proj, final_norm):
    B, S, D = x.shape
    depth = w_in.shape[0]
    assert S % TM == 0 and S % DSA_TK == 0 and S % MLA_TK == 0
    assert w_gate.shape[-1] % FF_CHUNK == 0
    topk = min(DSA_TOPK, S // 4)

    w_std, w_tr = _pack_w_in(w_in)
    wqa, wqb, wk, wv = _pack_mla(mla_w_uq, mla_w_ukv)
    cos_s, sin_s = _rope_tables(S)
    cos_t, sin_t = cos_s.T, sin_s.T
    row = lambda g: g.reshape(depth, 1, -1)
    g_attn, g_q, g_kv, g_ffn, g_ple = map(row, (attn_norm, mla_q_norm, mla_kv_norm, ffn_norm, ple_norm))
    g_fin = final_norm.reshape(1, -1)
    wo = _pack_w_o(w_o)
    wg, wu, wd, wpg, wpp = (a.astype(BF16) for a in (w_gate, w_up, w_down, w_ple_gate, w_ple_proj))
    tri = jnp.triu(jnp.ones((SB_T, SB_T), F32), 0)
    tail = jnp.zeros((16, SB_T), F32).at[0].set(1.0)
    u = jnp.concatenate([tri, tail], axis=0).astype(BF16)

    tab = _bias_call(rel_bias)

    h = x
    for i in range(depth):
        (mqt, mk, mvt, dqt, dk, dvt, iqt, ik, iwt, sqt, sk, svt) = _proj_call(
            h, g_attn[i], w_std[i], w_tr[i], g_q[i], wqa[i], wqb[i], g_kv[i], wk[i], wv[i],
            cos_s, sin_s, cos_t, sin_t)
        o_mla = _mla_call(mqt, mk, mvt)
        o_dsa = _dsa_call(iqt, ik, iwt, dqt, dk, dvt, tab, topk)
        o_sb = _sb_call(sqt, sk, svt, u)
        h = _post_call(h, o_mla, o_dsa, o_sb, p, i, wo[i], g_ffn[i], wg[i], wu[i], wd[i],
                       g_ple[i], wpg[i], wpp[i], g_fin, final=(i == depth - 1))
    return h
```

```python
import functools
import math

import jax
import jax.numpy as jnp
from jax import lax
from jax.experimental import pallas as pl
from jax.experimental.pallas import tpu as pltpu

EPS = 1e-6
MLA_HEADS = 6
MLA_Q_LORA = 384
MLA_KV_LORA = 256
MLA_NOPE = 64
MLA_ROPE = 32
MLA_V = 64
ROPE_THETA = 10000.0
DSA_HEADS = 5
DSA_DIM = 64
IDX_HEADS = 8
IDX_DIM = 64
DSA_TOPK = 256
SB_HEADS = 5
SB_DIM = 64
REL_BUCKETS = 32
REL_MAX_DIST = 128

LANES = 128
HEAD = 64
VROWS = 80
OUT_W = 384
VMEM_LIMIT = 56 * 1024 * 1024

TM = 512
MLA_TQ, MLA_TK = 256, 512
MLA_UNROLL = 4
SB_T = 256
SB_UNROLL = 4
DSA_TQ, DSA_TK = 128, 512
DSA_SUB = DSA_TK // DSA_TQ
DSA_UNROLL = 4
COUNT_ACC = 8
DESCENT_CHECKS = (24, 28)
SLAB = 8
SORT_GROUP = 7
SORT_NETWORK = ((0, 6), (2, 3), (4, 5), (0, 2), (1, 4), (3, 6), (0, 1), (2, 5), (3, 4),
                (1, 2), (4, 6), (2, 3), (4, 5), (1, 2), (3, 4), (5, 6))
FF_CHUNK = 1408

NEG = -1e30
INT_MIN = -2 ** 31
NEG_INF_KEY = -2139095041
LOG2E = math.log2(math.e)

F32 = jnp.float32
BF16 = jnp.bfloat16

SEG_CQ = (0, 384)
SEG_CKV = (384, 640)
SEG_KRA = (640, 768)
SEG_KRB = (768, 896)
SEG_SK_IK = (896, 1280)
SEG_DK = (1280, 1408)
W_STD_COLS = 1408
ROW_DQ = (0, 320)
ROW_IQ = (320, 832)
ROW_SQ = (832, 1152)
ROW_SV = (1152, 1472)
ROW_DV = (1472, 1536)
ROW_IW = (1536, 1552)
W_T_ROWS = 1552


def _rms(x, g):
    return x * lax.rsqrt(jnp.mean(x * x, axis=-1, keepdims=True) + EPS) * g


def _dot(a, b):
    return jnp.dot(a, b, preferred_element_type=F32)


def _const_spec(shape):
    n = len(shape)
    return pl.BlockSpec(shape, lambda *_: (0,) * n, pipeline_mode=pl.Buffered(1))


def _params(*sem):
    return pltpu.CompilerParams(dimension_semantics=sem, vmem_limit_bytes=VMEM_LIMIT)


def _pipeline(n, stages):
    held = [dict() for _ in stages]
    for t in range(n + len(stages) - 1):
        for s, stage in enumerate(stages):
            i = t - s
            if 0 <= i < n:
                held[s][i] = stage(i, held[s - 1].pop(i) if s else None)


def _for_blocks(n, unroll, step):
    def body(i, c):
        step([unroll * i + d for d in range(unroll)])
        return c

    lax.fori_loop(0, n // unroll, body, 0)
    rem = n % unroll
    group = unroll // 2
    while group:
        @pl.when(rem & group != 0)
        def _(group=group):
            base = n - (rem & (2 * group - 1))
            step([base + d for d in range(group)])
        group //= 2


def _ones_rows(n):
    r = lax.broadcasted_iota(jnp.int32, (VROWS - HEAD, n), 0)
    return jnp.where(r == 0, 1.0, 0.0).astype(BF16)


def _proj_kernel(h_ref, g_ref, w_ref, wt_ref, gq_ref, wqa_ref, wqb_ref, gkv_ref, wk_ref, wv_ref,
                 cos_ref, sin_ref, cost_ref, sint_ref,
                 mqt_ref, mk_ref, mvt_ref, dqt_ref, dk_ref, dvt_ref, iqt_ref, ik_ref, iwt_ref,
                 sqt_ref, sk_ref, svt_ref):
    hn = _rms(h_ref[0], g_ref[...])
    hnt = hn.T.astype(BF16)
    hn = hn.astype(BF16)
    tm = hn.shape[0]
    ones = _ones_rows(tm)

    def seg(ab):
        return _dot(hn, w_ref[:, ab[0]:ab[1]])

    def seg_t(ab):
        return _dot(wt_ref[ab[0]:ab[1], :], hnt)

    cqn_t = _rms(seg(SEG_CQ), gq_ref[...]).T.astype(BF16)
    qa = _dot(wqa_ref[...], cqn_t)
    qb = _dot(wqb_ref[...], cqn_t)
    cos_t = cost_ref[...]
    sin_t = sint_ref[...]
    for h in range(MLA_HEADS):
        sl = slice(h * LANES, (h + 1) * LANES)
        mqt_ref[0, h] = (qa[sl] * cos_t + qb[sl] * sin_t).astype(BF16)

    ckvn = _rms(seg(SEG_CKV), gkv_ref[...])
    ckvn_t = ckvn.T.astype(BF16)
    ckvn = ckvn.astype(BF16)
    kn = _dot(ckvn, wk_ref[...])
    kr = seg(SEG_KRA) * cos_ref[...] + seg(SEG_KRB) * sin_ref[...]
    for h in range(MLA_HEADS):
        mk_ref[0, h] = (kn[:, h * LANES:(h + 1) * LANES] + kr).astype(BF16)
    vt = _dot(wv_ref[...], ckvn_t)
    for h in range(MLA_HEADS):
        mvt_ref[0, h, 0:HEAD, :] = vt[h * HEAD:(h + 1) * HEAD].astype(BF16)
        mvt_ref[0, h, HEAD:VROWS, :] = ones

    a = seg(SEG_SK_IK)
    for h in range(SB_HEADS):
        sk_ref[0, h] = a[:, h * HEAD:(h + 1) * HEAD].astype(BF16)
    ik_ref[0] = a[:, 320:384].astype(BF16)
    dk_ref[0] = seg(SEG_DK)[:, 0:HEAD].astype(BF16)

    a = seg_t(ROW_DQ)
    for h in range(DSA_HEADS):
        dqt_ref[0, h] = a[h * HEAD:(h + 1) * HEAD].astype(BF16)
    a = seg_t(ROW_IQ)
    for h in range(IDX_HEADS):
        iqt_ref[0, h] = a[h * HEAD:(h + 1) * HEAD].astype(BF16)
    a = seg_t(ROW_SQ)
    for h in range(SB_HEADS):
        sqt_ref[0, h] = a[h * HEAD:(h + 1) * HEAD].astype(BF16)
    a = seg_t(ROW_SV)
    for h in range(SB_HEADS):
        svt_ref[0, h] = a[h * HEAD:(h + 1) * HEAD].astype(BF16)
    dvt_ref[0, 0:HEAD, :] = seg_t(ROW_DV).astype(BF16)
    dvt_ref[0, HEAD:VROWS, :] = ones
    iwt_ref[0] = seg_t(ROW_IW)[0:IDX_HEADS] * IDX_HEADS ** -0.5


def _proj_call(h, g, w, wt, gq, wqa, wqb, gkv, wk, wv, cos_s, sin_s, cos_t, sin_t):
    B, S, D = h.shape
    nt = S // TM
    tok = lambda w_: pl.BlockSpec((1, TM, w_), lambda b, t: (b, t, 0))
    heads = lambda n, w_: pl.BlockSpec((1, n, TM, w_), lambda b, t: (b, 0, t, 0))
    heads_t = lambda n, r: pl.BlockSpec((1, n, r, TM), lambda b, t: (b, 0, 0, t))
    rows_t = lambda r: pl.BlockSpec((1, r, TM), lambda b, t: (b, 0, t))
    hs = lambda n, w_: jax.ShapeDtypeStruct((B, n, S, w_), BF16)
    hts = lambda n, r: jax.ShapeDtypeStruct((B, n, r, S), BF16)
    return pl.pallas_call(
        _proj_kernel,
        grid=(B, nt),
        in_specs=[tok(D), _const_spec(g.shape), _const_spec(w.shape), _const_spec(wt.shape),
                  _const_spec(gq.shape), _const_spec(wqa.shape), _const_spec(wqb.shape),
                  _const_spec(gkv.shape), _const_spec(wk.shape), _const_spec(wv.shape),
                  pl.BlockSpec((TM, LANES), lambda b, t: (t, 0)),
                  pl.BlockSpec((TM, LANES), lambda b, t: (t, 0)),
                  pl.BlockSpec((LANES, TM), lambda b, t: (0, t)),
                  pl.BlockSpec((LANES, TM), lambda b, t: (0, t))],
        out_specs=[heads_t(MLA_HEADS, LANES), heads(MLA_HEADS, LANES), heads_t(MLA_HEADS, VROWS),
                   heads_t(DSA_HEADS, HEAD), tok(HEAD), rows_t(VROWS),
                   heads_t(IDX_HEADS, HEAD), tok(HEAD), rows_t(IDX_HEADS),
                   heads_t(SB_HEADS, HEAD), heads(SB_HEADS, HEAD), heads_t(SB_HEADS, HEAD)],
        out_shape=[hts(MLA_HEADS, LANES), hs(MLA_HEADS, LANES), hts(MLA_HEADS, VROWS),
                   hts(DSA_HEADS, HEAD), jax.ShapeDtypeStruct((B, S, HEAD), BF16),
                   jax.ShapeDtypeStruct((B, VROWS, S), BF16),
                   hts(IDX_HEADS, HEAD), jax.ShapeDtypeStruct((B, S, HEAD), BF16),
                   jax.ShapeDtypeStruct((B, IDX_HEADS, S), F32),
                   hts(SB_HEADS, HEAD), hs(SB_HEADS, HEAD), hts(SB_HEADS, HEAD)],
        compiler_params=_params("parallel", "parallel"),
        name="proj",
    )(h, g, w, wt, gq, wqa, wqb, gkv, wk, wv, cos_s, sin_s, cos_t, sin_t)


def _finish(acc_ref, n_heads, o_ref):
    outs = [acc_ref[h, 0:HEAD, :] / acc_ref[h, HEAD:HEAD + 1, :] for h in range(n_heads)]
    pad = OUT_W - n_heads * HEAD
    if pad:
        outs.append(jnp.zeros((pad, outs[0].shape[1]), F32))
    o_ref[0] = jnp.concatenate(outs, axis=0).T.astype(BF16)


def _mla_kernel(qt_ref, k_ref, vt_ref, o_ref, m_ref, acc_ref, s_ref):
    qi = pl.program_id(1)
    tq, tk = MLA_TQ, MLA_TK
    m_ref[...] = jnp.full(m_ref.shape, NEG, F32)
    acc_ref[...] = jnp.zeros(acc_ref.shape, F32)
    scale2 = (MLA_NOPE + MLA_ROPE) ** -0.5 * LOG2E
    n_full = (qi * tq) // tk
    n_all = ((qi + 1) * tq + tk - 1) // tk

    def qk(h, j):
        k0 = pl.multiple_of(j * tk, tk)
        half = tk // 4
        return jnp.concatenate(
            [_dot(k_ref[0, h, pl.ds(k0 + i * half, half), :], qt_ref[0, h]) for i in range(4)],
            axis=0)

    s_ref[...] = qk(0, 0)

    def step(js, masked):
        items = [(j, h) for j in js for h in range(MLA_HEADS)]
        n = len(items)

        def scores(i, _):
            if i == 0:
                return s_ref[...]
            if i == n:
                return qk(0, jnp.minimum(js[-1] + 1, n_all - 1))
            j, h = items[i]
            return qk(h, j)

        def running_max(i, s):
            if i == n:
                s_ref[...] = s
                return None
            j, h = items[i]
            s = s * scale2
            if masked:
                kpos = j * tk + lax.broadcasted_iota(jnp.int32, (tk, tq), 0)
                qpos = qi * tq + lax.broadcasted_iota(jnp.int32, (tk, tq), 1)
                s = jnp.where(kpos <= qpos, s, NEG)
            m_prev = m_ref[h]
            m_new = jnp.maximum(m_prev, jnp.max(s, axis=0, keepdims=True))
            m_ref[h] = m_new
            return s, m_prev, m_new

        def weights_pv(i, carry):
            if i == n:
                return
            j, h = items[i]
            s, m_prev, m_new = carry
            p = jnp.exp2(s - m_new)
            k0 = pl.multiple_of(j * tk, tk)
            acc_ref[h] = (jnp.exp2(m_prev - m_new) * acc_ref[h]
                          + _dot(vt_ref[0, h, :, pl.ds(k0, tk)], p.astype(BF16)))

        _pipeline(n + 1, [scores, running_max, weights_pv])

    _for_blocks(n_full, MLA_UNROLL, lambda js: step(js, False))
    step([n_full], True)
    _finish(acc_ref, MLA_HEADS, o_ref)


def _mla_call(qt, k, vt):
    B, H, _, S = qt.shape
    res = lambda shape: pl.BlockSpec(shape, lambda b, i: (b, 0, 0, 0), pipeline_mode=pl.Buffered(1))
    return pl.pallas_call(
        _mla_kernel,
        grid=(B, S // MLA_TQ),
        in_specs=[pl.BlockSpec((1, H, LANES, MLA_TQ), lambda b, i: (b, 0, 0, i)),
                  res((1, H, S, LANES)), res((1, H, VROWS, S))],
        out_specs=pl.BlockSpec((1, MLA_TQ, OUT_W), lambda b, i: (b, i, 0)),
        out_shape=jax.ShapeDtypeStruct((B, S, OUT_W), BF16),
        scratch_shapes=[pltpu.VMEM((H, 1, MLA_TQ), F32), pltpu.VMEM((H, VROWS, MLA_TQ), F32),
                        pltpu.VMEM((MLA_TK, MLA_TQ), F32)],
        compiler_params=_params("parallel", "arbitrary"),
        name="mla",
    )(qt, k, vt)


def _sb_kernel(qt_ref, k_ref, vt_ref, u_ref, o_ref, c_ref, acc_ref, z_ref):
    qi = pl.program_id(1)
    t = SB_T
    c_ref[...] = jnp.zeros(c_ref.shape, F32)
    acc_ref[...] = jnp.zeros(acc_ref.shape, F32)

    def qk(h, j):
        k0 = pl.multiple_of(j * t, t)
        half = t // 2
        return jnp.concatenate(
            [_dot(k_ref[0, h, pl.ds(k0 + s * half, half), :], qt_ref[0, h]) for s in range(2)],
            axis=0)

    strict = (lax.broadcasted_iota(jnp.int32, (t, t), 0) < lax.broadcasted_iota(jnp.int32, (t, t), 1))

    def suffix_sums(z, masked):
        z2 = z * LOG2E
        neg_abs = pltpu.bitcast(pltpu.bitcast(z2, jnp.int32) | INT_MIN, F32)
        nl = jnp.maximum(z2, 0.0) + jnp.log2(1.0 + jnp.exp2(neg_abs))
        if masked:
            nl = jnp.where(strict, nl, 0.0)
        nlb = nl.astype(BF16)
        sums = jnp.concatenate([_dot(u_ref[0:t // 2, :], nlb), _dot(u_ref[t // 2:, :], nlb)], axis=0)
        return z2, sums

    z_ref[...] = qk(0, qi)

    def step(js, masked):
        items = [(j, h) for j in js for h in range(SB_HEADS)]
        n = len(items)

        def logits(i, _):
            if i == 0:
                return z_ref[...]
            if i == n:
                return qk(0, jnp.maximum(js[-1] - 1, 0))
            j, h = items[i]
            return qk(h, j)

        def sums_stage(i, z):
            if i == n:
                z_ref[...] = z
                return None
            return suffix_sums(z, masked)

        def weights_pv(i, carry):
            if i == n:
                return
            j, h = items[i]
            z2, sums = carry
            w = jnp.exp2(z2 - sums[0:t])
            if masked:
                w = jnp.where(strict, w, 0.0)
            k0 = pl.multiple_of(j * t, t)
            acc_ref[h] += jnp.exp2(-c_ref[h]) * _dot(vt_ref[0, h, :, pl.ds(k0, t)], w.astype(BF16))
            c_ref[h] += sums[t:t + 1]

        _pipeline(n + 1, [logits, sums_stage, weights_pv])

    step([qi], True)

    _for_blocks(qi, SB_UNROLL, lambda idx: step([qi - 1 - i for i in idx], False))

    outs = [acc_ref[h] for h in range(SB_HEADS)]
    outs.append(jnp.zeros((OUT_W - SB_HEADS * HEAD, t), F32))
    o_ref[0] = jnp.concatenate(outs, axis=0).T.astype(BF16)


def _sb_call(qt, k, vt, u):
    B, H, _, S = qt.shape
    res = lambda shape: pl.BlockSpec(shape, lambda b, i: (b, 0, 0, 0), pipeline_mode=pl.Buffered(1))
    return pl.pallas_call(
        _sb_kernel,
        grid=(B, S // SB_T),
        in_specs=[pl.BlockSpec((1, H, HEAD, SB_T), lambda b, i: (b, 0, 0, i)),
                  res((1, H, S, HEAD)), res((1, H, HEAD, S)), _const_spec(u.shape)],
        out_specs=pl.BlockSpec((1, SB_T, OUT_W), lambda b, i: (b, i, 0)),
        out_shape=jax.ShapeDtypeStruct((B, S, OUT_W), BF16),
        scratch_shapes=[pltpu.VMEM((H, 1, SB_T), F32), pltpu.VMEM((H, HEAD, SB_T), F32),
                        pltpu.VMEM((SB_T, SB_T), F32)],
        compiler_params=_params("parallel", "arbitrary"),
        name="stickbreak",
    )(qt, k, vt, u)


def _bias_kernel(rb_ref, o_ref):
    v = pl.program_id(0)
    shape = (DSA_TK, DSA_TQ)
    off = jnp.where(v < DSA_SUB, v * DSA_TQ, DSA_TK)
    dist = off + lax.broadcasted_iota(jnp.int32, shape, 1) - lax.broadcasted_iota(jnp.int32, shape, 0)
    max_exact = REL_BUCKETS // 2
    d = jnp.maximum(dist, 1).astype(F32)
    large = max_exact + (jnp.log(d / max_exact) / math.log(REL_MAX_DIST / max_exact)
                         * (REL_BUCKETS - max_exact)).astype(jnp.int32)
    large = jnp.minimum(large, REL_BUCKETS - 1)
    bucket = jnp.where(dist < max_exact, dist, large)
    for h in range(DSA_HEADS):
        t = jnp.zeros(shape, F32)
        for b in range(REL_BUCKETS):
            t = jnp.where(bucket == b, rb_ref[b, h] - rb_ref[REL_BUCKETS - 1, h], t)
        o_ref[0, h] = jnp.where(dist >= 0, t, NEG)


def _bias_call(rel_bias):
    nv = DSA_SUB + 1
    return pl.pallas_call(
        _bias_kernel,
        grid=(nv,),
        in_specs=[pl.BlockSpec(memory_space=pltpu.SMEM)],
        out_specs=pl.BlockSpec((1, DSA_HEADS, DSA_TK, DSA_TQ), lambda v: (v, 0, 0, 0)),
        out_shape=jax.ShapeDtypeStruct((nv, DSA_HEADS, DSA_TK, DSA_TQ), F32),
        compiler_params=_params("arbitrary"),
        name="bias_tiles",
    )(rel_bias)


def _dsa_kernel(iqt_ref, ik_ref, iwt_ref, dqt_ref, dk_ref, dvt_ref, tab_ref, o_ref,
                keys_ref, sorted_ref, m_ref, acc_ref, x_ref, *, topk):
    qi = pl.program_id(1)
    n_blk = qi // DSA_SUB + 1
    sub = qi % DSA_SUB
    tq, tk = DSA_TQ, DSA_TK

    iqt = jnp.concatenate([iqt_ref[0, h] for h in range(IDX_HEADS)], axis=1)
    iw = iwt_ref[0]

    def pair_scores(k_ref_, qt_all, g, j, n_heads):
        k0 = pl.multiple_of(j * tk, tk)
        cols = qt_all[:, 2 * g * tq:min(2 * (g + 1), n_heads) * tq]
        half = tk // 2
        return jnp.concatenate(
            [_dot(k_ref_[0, pl.ds(k0 + i * half, half), :], cols) for i in range(2)], axis=0)

    n_slab = tk // SLAB
    n_group = n_slab // SORT_GROUP
    zero_shift = 23 - (keys_ref.shape[0] * tk - 1).bit_length()
    assert zero_shift >= 0
    zero_key = -1 - lax.shift_left(lax.broadcasted_iota(jnp.int32, (tk, tq), 0), zero_shift)

    def sort_groups(key):
        slabs = [key[SLAB * i:SLAB * (i + 1)] for i in range(n_slab)]
        for g in range(n_group):
            v = slabs[SORT_GROUP * g:SORT_GROUP * (g + 1)]
            for a, b in SORT_NETWORK:
                v[a], v[b] = jnp.maximum(v[a], v[b]), jnp.minimum(v[a], v[b])
            slabs[SORT_GROUP * g:SORT_GROUP * (g + 1)] = v
        return jnp.concatenate(slabs, axis=0)

    x_ref[...] = pair_scores(ik_ref, iqt, 0, 0, IDX_HEADS)

    def idx_step(js, masked):
        n_pairs = IDX_HEADS // 2
        items = [(j, g) for j in js for g in range(n_pairs)]
        n = len(items)
        total = {}

        def scores(i, _):
            if i == 0:
                return x_ref[...]
            if i == n:
                return pair_scores(ik_ref, iqt, 0, jnp.minimum(js[-1] + 1, n_blk - 1), IDX_HEADS)
            j, g = items[i]
            return pair_scores(ik_ref, iqt, g, j, IDX_HEADS)

        def weigh(i, x):
            if i == n:
                x_ref[...] = x
                return
            j, g = items[i]
            part = (jnp.maximum(x[:, 0:tq], 0.0) * iw[2 * g:2 * g + 1]
                    + jnp.maximum(x[:, tq:2 * tq], 0.0) * iw[2 * g + 1:2 * g + 2])
            blk = i // n_pairs
            total[blk] = part if g == 0 else total[blk] + part
            if g < n_pairs - 1:
                return
            bits = pltpu.bitcast(total.pop(blk), jnp.int32)
            key = bits ^ (lax.shift_right_arithmetic(bits, 31) & 0x7FFFFFFF)
            key = jnp.where((bits & 0x7F800000) == 0, zero_key - lax.shift_left(j * tk, zero_shift), key)
            if masked:
                kpos = j * tk + lax.broadcasted_iota(jnp.int32, key.shape, 0)
                qpos = qi * tq + lax.broadcasted_iota(jnp.int32, key.shape, 1)
                key = jnp.where(kpos <= qpos, key, NEG_INF_KEY)
            keys_ref[j] = key
            sorted_ref[j] = sort_groups(key)

        _pipeline(n + 1, [scores, weigh])

    _for_blocks(n_blk - 1, DSA_UNROLL, lambda js: idx_step(js, False))
    idx_step([n_blk - 1], True)

    @pl.when(n_blk % 2 == 1)
    def _():
        keys_ref[n_blk] = jnp.full((tk, tq), INT_MIN, jnp.int32)
        sorted_ref[n_blk] = jnp.full((tk, tq), INT_MIN, jnp.int32)

    def count_ge(cand):
        def body(i, acc):
            for j in (2 * i, 2 * i + 1):
                c = jnp.where(keys_ref[j] >= cand, 1.0, 0.0)
                acc = acc + jnp.sum(c.reshape(COUNT_ACC, tk // COUNT_ACC, tq), axis=0)
            return acc
        acc = lax.fori_loop(0, (n_blk + 1) // 2, body, jnp.zeros((tk // COUNT_ACC, tq), F32))
        return jnp.sum(acc, axis=0, keepdims=True)

    def count_ge_sorted(cand):
        def one(flag):
            return jnp.where(flag, 1.0, 0.0)

        def body(i, accs):
            accs = list(accs)
            for j in (2 * i, 2 * i + 1):
                blk = sorted_ref[j]
                for g in range(n_group):
                    s = [blk[SLAB * (SORT_GROUP * g + k):SLAB * (SORT_GROUP * g + k + 1)]
                         for k in range(SORT_GROUP)]
                    b4 = s[3] >= cand
                    b2 = jnp.where(b4, s[5], s[1]) >= cand
                    b1 = jnp.where(b4, jnp.where(b2, s[6], s[4]), jnp.where(b2, s[2], s[0])) >= cand
                    o = 3 * (g % 2)
                    accs[o] += one(b4)
                    accs[o + 1] += one(b2)
                    accs[o + 2] += one(b1)
                for r in range(SORT_GROUP * n_group, n_slab):
                    accs[2] += one(blk[SLAB * r:SLAB * (r + 1)] >= cand)
            return tuple(accs)

        z = jnp.zeros((SLAB, tq), F32)
        a = lax.fori_loop(0, (n_blk + 1) // 2, body, (z,) * 6)
        total = 4.0 * (a[0] + a[3]) + 2.0 * (a[1] + a[4]) + (a[2] + a[5])
        return jnp.sum(total, axis=0, keepdims=True)

    def bit_body(i, state):
        t_u, done, thr_hit = state
        cand_u = t_u | lax.shift_left(jnp.int32(1), 31 - i)
        cand = cand_u ^ INT_MIN
        cnt = count_ge_sorted(cand)
        hit = cnt == topk
        thr_hit = jnp.where(hit, jnp.where(done > 0.0, thr_hit, cand), thr_hit)
        done = jnp.where(hit, 1.0, done)
        return jnp.where(cnt >= topk, cand_u, t_u), done, thr_hit

    zero = jnp.zeros((1, tq), jnp.int32)
    state = (zero, jnp.zeros((1, tq), F32), zero)
    first = 0
    for last in DESCENT_CHECKS + (32,):
        stop = last if first == 0 else jnp.where(jnp.min(state[1]) < 1.0, last, first)
        state = lax.fori_loop(first, stop, bit_body, state)
        first = last
    t_u, done, thr_hit = state
    thr = jnp.where(done > 0.0, thr_hit, t_u ^ INT_MIN)

    tie = jnp.where(done > 0.0, 0.0, jnp.where(thr > NEG_INF_KEY, 1.0, 0.0))

    @pl.when(jnp.max(tie) > 0.0)
    def _():
        need = jnp.where(tie > 0.0, topk - count_ge(thr + 1), float(tk * keys_ref.shape[0] + 1))
        row = lax.broadcasted_iota(jnp.int32, (tk, tk), 0)
        col = lax.broadcasted_iota(jnp.int32, (tk, tk), 1)
        before = jnp.where(col < row, 1.0, 0.0).astype(BF16)

        def body(j, run):
            kb = keys_ref[j]
            eq = jnp.where(kb == thr, 1.0, 0.0)
            rank = _dot(before, eq.astype(BF16)) + run
            keys_ref[j] = jnp.where(kb == thr, jnp.where(rank >= need, thr - 1, kb), kb)
            return run + jnp.sum(eq, axis=0, keepdims=True)

        lax.fori_loop(0, n_blk, body, jnp.zeros((1, tq), F32))

    dqt = jnp.concatenate([dqt_ref[0, h] for h in range(DSA_HEADS)], axis=1)
    m_ref[...] = jnp.full(m_ref.shape, NEG, F32)
    acc_ref[...] = jnp.zeros(acc_ref.shape, F32)

    x_ref[...] = pair_scores(dk_ref, dqt, 0, 0, DSA_HEADS)

    def att_step(js, variant):
        n_pairs = (DSA_HEADS + 1) // 2
        items = [(j, g) for j in js for g in range(n_pairs)]
        n = len(items)

        def scores(i, _):
            if i == 0:
                return x_ref[...]
            if i == n:
                return pair_scores(dk_ref, dqt, 0, jnp.minimum(js[-1] + 1, n_blk - 1), DSA_HEADS)
            j, g = items[i]
            return pair_scores(dk_ref, dqt, g, j, DSA_HEADS)

        def running_max(i, s):
            if i == n:
                x_ref[...] = s
                return None
            j, g = items[i]
            sel = keys_ref[j] >= thr
            out = []
            for h in range(2 * g, min(2 * (g + 1), DSA_HEADS)):
                a = s[:, (h - 2 * g) * tq:(h - 2 * g + 1) * tq]
                if variant is not None:
                    a = a + tab_ref[variant, h]
                a = jnp.where(sel, a, NEG)
                m_prev = m_ref[h]
                m_new = jnp.maximum(m_prev, jnp.max(a, axis=0, keepdims=True))
                m_ref[h] = m_new
                out.append((h, a, m_prev, m_new))
            return out

        def weights_pv(i, carry):
            if i == n:
                return
            j, _ = items[i]
            vt = dvt_ref[0, :, pl.ds(pl.multiple_of(j * tk, tk), tk)]
            for h, a, m_prev, m_new in carry:
                p = jnp.exp(a - m_new)
                acc_ref[h] = jnp.exp(m_prev - m_new) * acc_ref[h] + _dot(vt, p.astype(BF16))

        _pipeline(n + 1, [scores, running_max, weights_pv])

    _for_blocks(jnp.maximum(n_blk - 2, 0), DSA_UNROLL, lambda js: att_step(js, None))

    @pl.when(jnp.logical_and(n_blk >= 2, sub == 0))
    def _():
        att_step([n_blk - 2], DSA_SUB)

    @pl.when(jnp.logical_and(n_blk >= 2, sub != 0))
    def _():
        att_step([n_blk - 2], None)

    att_step([n_blk - 1], sub)
    _finish(acc_ref, DSA_HEADS, o_ref)


def _dsa_call(iqt, ik, iwt, dqt, dk, dvt, tab, topk):
    B, _, _, S = iqt.shape
    tq = DSA_TQ
    qh = lambda n: pl.BlockSpec((1, n, HEAD, tq), lambda b, i: (b, 0, 0, i))
    res = lambda shape: pl.BlockSpec(shape, lambda b, i: (b, 0, 0), pipeline_mode=pl.Buffered(1))
    return pl.pallas_call(
        functools.partial(_dsa_kernel, topk=topk),
        grid=(B, S // tq),
        in_specs=[qh(IDX_HEADS), res((1, S, HEAD)),
                  pl.BlockSpec((1, IDX_HEADS, tq), lambda b, i: (b, 0, i)),
                  qh(DSA_HEADS), res((1, S, HEAD)), res((1, VROWS, S)), _const_spec(tab.shape)],
        out_specs=pl.BlockSpec((1, tq, OUT_W), lambda b, i: (b, i, 0)),
        out_shape=jax.ShapeDtypeStruct((B, S, OUT_W), BF16),
        scratch_shapes=[pltpu.VMEM((S // DSA_TK + (S // DSA_TK) % 2, DSA_TK, tq), jnp.int32),
                        pltpu.VMEM((S // DSA_TK + (S // DSA_TK) % 2, DSA_TK, tq), jnp.int32),
                        pltpu.VMEM((DSA_HEADS, 1, tq), F32),
                        pltpu.VMEM((DSA_HEADS, VROWS, tq), F32),
                        pltpu.VMEM((DSA_TK, 2 * tq), F32)],
        compiler_params=_params("parallel", "arbitrary"),
        name="dsa",
    )(iqt, ik, iwt, dqt, dk, dvt, tab)


def _post_kernel(h_ref, om_ref, od_ref, os_ref, p_ref, wo_ref, gf_ref, wg_ref, wu_ref, wd_ref,
                 gp_ref, wpg_ref, wpp_ref, gfin_ref, out_ref, *, final):
    h = (h_ref[0] + _dot(om_ref[0], wo_ref[0]) + _dot(od_ref[0], wo_ref[1])
         + _dot(os_ref[0], wo_ref[2]))

    hf = _rms(h, gf_ref[...]).astype(BF16)
    d_ff = wg_ref.shape[1]
    ffn = jnp.zeros(h.shape, F32)
    for c0 in range(0, d_ff, FF_CHUNK):
        g = _dot(hf, wg_ref[:, c0:c0 + FF_CHUNK])
        u = _dot(hf, wu_ref[:, c0:c0 + FF_CHUNK])
        ffn = ffn + _dot((g * jax.nn.sigmoid(g) * u).astype(BF16), wd_ref[c0:c0 + FF_CHUNK, :])
    h = h + ffn

    gate = jax.nn.sigmoid(_dot(_rms(h, gp_ref[...]).astype(BF16), wpg_ref[...]))
    h = h + gate * _dot(p_ref[0, 0].astype(BF16), wpp_ref[...])
    if final:
        h = _rms(h, gfin_ref[...])
    out_ref[0] = h


def _post_call(h, om, od, os_, p, layer, wo, gf, wg, wu, wd, gp, wpg, wpp, gfin, final):
    B, S, D = h.shape
    mix = pl.BlockSpec((1, TM, OUT_W), lambda b, t: (b, t, 0))
    tok = pl.BlockSpec((1, TM, D), lambda b, t: (b, t, 0))
    return pl.pallas_call(
        functools.partial(_post_kernel, final=final),
        grid=(B, S // TM),
        in_specs=[tok, mix, mix, mix,
                  pl.BlockSpec((1, 1, TM, p.shape[-1]), lambda b, t: (layer, b, t, 0)),
                  _const_spec(wo.shape), _const_spec(gf.shape), _const_spec(wg.shape),
                  _const_spec(wu.shape), _const_spec(wd.shape), _const_spec(gp.shape),
                  _const_spec(wpg.shape), _const_spec(wpp.shape), _const_spec(gfin.shape)],
        out_specs=tok,
        out_shape=jax.ShapeDtypeStruct((B, S, D), F32),
        compiler_params=_params("parallel", "parallel"),
        name="post",
    )(h, om, od, os_, p, wo, gf, wg, wu, wd, gp, wpg, wpp, gfin)


def _swap_halves(w):
    half = w.shape[-1] // 2
    return jnp.concatenate([w[..., half:], w[..., :half]], axis=-1)


def _pack_w_in(w_in):
    L, D, _ = w_in.shape
    z = lambda n: jnp.zeros((L, D, n), F32)
    o = 0
    cols = {}
    for name, n in (("cq", MLA_Q_LORA), ("ckv", MLA_KV_LORA), ("kr", MLA_ROPE),
                    ("dq", DSA_HEADS * DSA_DIM), ("dk", DSA_DIM), ("dv", DSA_DIM),
                    ("iq", IDX_HEADS * IDX_DIM), ("ik", IDX_DIM), ("iw", IDX_HEADS),
                    ("sq", SB_HEADS * SB_DIM), ("sk", SB_HEADS * SB_DIM), ("sv", SB_HEADS * SB_DIM)):
        cols[name] = w_in[:, :, o:o + n]
        o += n
    pad_r = LANES - MLA_NOPE - MLA_ROPE
    std = jnp.concatenate([
        cols["cq"], cols["ckv"],
        z(MLA_NOPE), cols["kr"], z(pad_r),
        z(MLA_NOPE), _swap_halves(cols["kr"]), z(pad_r),
        cols["sk"], cols["ik"],
        cols["dk"], z(LANES - DSA_DIM)], axis=-1)
    tr = jnp.concatenate([
        cols["dq"] * DSA_DIM ** -0.5, cols["iq"] * IDX_DIM ** -0.5, cols["sq"] * SB_DIM ** -0.5,
        cols["sv"], cols["dv"], cols["iw"], z(ROW_IW[1] - ROW_IW[0] - IDX_HEADS)], axis=-1)
    assert std.shape[-1] == W_STD_COLS and tr.shape[-1] == W_T_ROWS
    return std.astype(BF16), jnp.swapaxes(tr, 1, 2).astype(BF16)


def _pack_mla(w_uq, w_ukv):
    L = w_uq.shape[0]
    dq = MLA_NOPE + MLA_ROPE
    pad_r = LANES - dq
    uq = w_uq.reshape(L, MLA_Q_LORA, MLA_HEADS, dq)
    zq = lambda n: jnp.zeros((L, MLA_Q_LORA, MLA_HEADS, n), F32)
    wqa = jnp.concatenate([uq, zq(pad_r)], axis=-1)
    wqb = jnp.concatenate([zq(MLA_NOPE), _swap_halves(uq[..., MLA_NOPE:]), zq(pad_r)], axis=-1)
    ukv = w_ukv.reshape(L, MLA_KV_LORA, MLA_HEADS, MLA_NOPE + MLA_V)
    wk = jnp.concatenate([ukv[..., :MLA_NOPE],
                          jnp.zeros((L, MLA_KV_LORA, MLA_HEADS, LANES - MLA_NOPE), F32)], axis=-1)
    wv = ukv[..., MLA_NOPE:]
    flat = lambda a: a.reshape(L, a.shape[1], -1)
    tr = lambda a: jnp.swapaxes(flat(a), 1, 2).astype(BF16)
    return tr(wqa), tr(wqb), flat(wk).astype(BF16), tr(wv)


def _rope_tables(S):
    half = MLA_ROPE // 2
    inv = ROPE_THETA ** (-jnp.arange(half, dtype=F32) / half)
    ang = jnp.arange(S, dtype=jnp.int32).astype(F32)[:, None] * inv[None, :]
    cos, sin = jnp.cos(ang), jnp.sin(ang)
    pad_r = LANES - MLA_NOPE - MLA_ROPE
    cos_t = jnp.concatenate([jnp.ones((S, MLA_NOPE), F32), cos, cos, jnp.zeros((S, pad_r), F32)], axis=-1)
    sin_t = jnp.concatenate([jnp.zeros((S, MLA_NOPE), F32), -sin, sin, jnp.zeros((S, pad_r), F32)], axis=-1)
    return cos_t, sin_t


def _pack_w_o(w_o):
    L, _, D = w_o.shape
    a = MLA_HEADS * MLA_V
    b = a + DSA_HEADS * DSA_DIM
    pad = jnp.zeros((L, OUT_W - DSA_HEADS * DSA_DIM, D), F32)
    blocks = [w_o[:, :a], jnp.concatenate([w_o[:, a:b], pad], axis=1),
              jnp.concatenate([w_o[:, b:], pad], axis=1)]
    return jnp.stack(blocks, axis=1).astype(BF16)


def kernel(x, p, w_in, attn_norm, mla_q_norm, mla_w_uq, mla_kv_norm, mla_w_ukv, rel_bias, w_o,
           ffn_norm, w_gate, w_up, w_down, ple_norm, w_ple_gate, w_ple_proj, final_norm):
    B, S, D = x.shape
    depth = w_in.shape[0]
    assert S % TM == 0 and S % DSA_TK == 0 and S % MLA_TK == 0
    assert w_gate.shape[-1] % FF_CHUNK == 0
    topk = min(DSA_TOPK, S // 4)

    w_std, w_tr = _pack_w_in(w_in)
    wqa, wqb, wk, wv = _pack_mla(mla_w_uq, mla_w_ukv)
    cos_s, sin_s = _rope_tables(S)
    cos_t, sin_t = cos_s.T, sin_s.T
    row = lambda g: g.reshape(depth, 1, -1)
    g_attn, g_q, g_kv, g_ffn, g_ple = map(row, (attn_norm, mla_q_norm, mla_kv_norm, ffn_norm, ple_norm))
    g_fin = final_norm.reshape(1, -1)
    wo = _pack_w_o(w_o)
    wg, wu, wd, wpg, wpp = (a.astype(BF16) for a in (w_gate, w_up, w_down, w_ple_gate, w_ple_proj))
    tri = jnp.triu(jnp.ones((SB_T, SB_T), F32), 0)
    tail = jnp.zeros((16, SB_T), F32).at[0].set(1.0)
    u = jnp.concatenate([tri, tail], axis=0).astype(BF16)

    tab = _bias_call(rel_bias)

    h = x
    for i in range(depth):
        (mqt, mk, mvt, dqt, dk, dvt, iqt, ik, iwt, sqt, sk, svt) = _proj_call(
            h, g_attn[i], w_std[i], w_tr[i], g_q[i], wqa[i], wqb[i], g_kv[i], wk[i], wv[i],
            cos_s, sin_s, cos_t, sin_t)
        o_mla = _mla_call(mqt, mk, mvt)
        o_dsa = _dsa_call(iqt, ik, iwt, dqt, dk, dvt, tab, topk)
        o_sb = _sb_call(sqt, sk, svt, u)
        h = _post_call(h, o_mla, o_dsa, o_sb, p, i, wo[i], g_ffn[i], wg[i], wu[i], wd[i],
                       g_ple[i], wpg[i], wpp[i], g_fin, final=(i == depth - 1))
    return h
```

```python
import functools
import math

import jax
import jax.numpy as jnp
from jax import lax
from jax.experimental import pallas as pl
from jax.experimental.pallas import tpu as pltpu

EPS = 1e-6
MLA_HEADS = 6
MLA_Q_LORA = 384
MLA_KV_LORA = 256
MLA_NOPE = 64
MLA_ROPE = 32
MLA_V = 64
ROPE_THETA = 10000.0
DSA_HEADS = 5
DSA_DIM = 64
IDX_HEADS = 8
IDX_DIM = 64
DSA_TOPK = 256
SB_HEADS = 5
SB_DIM = 64
REL_BUCKETS = 32
REL_MAX_DIST = 128

LANES = 128
HEAD = 64
VROWS = 80
OUT_W = 384
VMEM_LIMIT = 56 * 1024 * 1024

TM = 512
MLA_TQ, MLA_TK = 256, 512
MLA_UNROLL = 4
SB_T = 256
SB_UNROLL = 4
DSA_TQ, DSA_TK = 128, 512
DSA_SUB = DSA_TK // DSA_TQ
DSA_UNROLL = 4
COUNT_ACC = 8
DESCENT_CHECKS = (24, 28)
SLAB = 8
SORT_GROUP = 7
SORT_NETWORK = ((0, 6), (2, 3), (4, 5), (0, 2), (1, 4), (3, 6), (0, 1), (2, 5), (3, 4),
                (1, 2), (4, 6), (2, 3), (4, 5), (1, 2), (3, 4), (5, 6))
FF_CHUNK = 1408

NEG = -1e30
INT_MIN = -2 ** 31
NEG_INF_KEY = -2139095041
LOG2E = math.log2(math.e)

F32 = jnp.float32
BF16 = jnp.bfloat16

SEG_CQ = (0, 384)
SEG_CKV = (384, 640)
SEG_KRA = (640, 768)
SEG_KRB = (768, 896)
SEG_SK_IK = (896, 1280)
SEG_DK = (1280, 1408)
W_STD_COLS = 1408
ROW_DQ = (0, 320)
ROW_IQ = (320, 832)
ROW_SQ = (832, 1152)
ROW_SV = (1152, 1472)
ROW_DV = (1472, 1536)
ROW_IW = (1536, 1552)
W_T_ROWS = 1552


def _rms(x, g):
    return x * lax.rsqrt(jnp.mean(x * x, axis=-1, keepdims=True) + EPS) * g


def _dot(a, b):
    return jnp.dot(a, b, preferred_element_type=F32)


def _const_spec(shape):
    n = len(shape)
    return pl.BlockSpec(shape, lambda *_: (0,) * n, pipeline_mode=pl.Buffered(1))


def _params(*sem):
    return pltpu.CompilerParams(dimension_semantics=sem, vmem_limit_bytes=VMEM_LIMIT)


def _pipeline(n, stages):
    held = [dict() for _ in stages]
    for t in range(n + len(stages) - 1):
        for s, stage in enumerate(stages):
            i = t - s
            if 0 <= i < n:
                held[s][i] = stage(i, held[s - 1].pop(i) if s else None)


def _for_blocks(n, unroll, step):
    def body(i, c):
        step([unroll * i + d for d in range(unroll)])
        return c

    lax.fori_loop(0, n // unroll, body, 0)
    rem = n % unroll
    group = unroll // 2
    while group:
        @pl.when(rem & group != 0)
        def _(group=group):
            base = n - (rem & (2 * group - 1))
            step([base + d for d in range(group)])
        group //= 2


def _ones_rows(n):
    r = lax.broadcasted_iota(jnp.int32, (VROWS - HEAD, n), 0)
    return jnp.where(r == 0, 1.0, 0.0).astype(BF16)


def _proj_kernel(h_ref, g_ref, w_ref, wt_ref, gq_ref, wqa_ref, wqb_ref, gkv_ref, wk_ref, wv_ref,
                 cos_ref, sin_ref, cost_ref, sint_ref,
                 mqt_ref, mk_ref, mvt_ref, dqt_ref, dk_ref, dvt_ref, iqt_ref, ik_ref, iwt_ref,
                 sqt_ref, sk_ref, svt_ref):
    hn = _rms(h_ref[0], g_ref[...])
    hnt = hn.T.astype(BF16)
    hn = hn.astype(BF16)
    tm = hn.shape[0]
    ones = _ones_rows(tm)

    def seg(ab):
        return _dot(hn, w_ref[:, ab[0]:ab[1]])

    def seg_t(ab):
        return _dot(wt_ref[ab[0]:ab[1], :], hnt)

    cqn_t = _rms(seg(SEG_CQ), gq_ref[...]).T.astype(BF16)
    qa = _dot(wqa_ref[...], cqn_t)
    qb = _dot(wqb_ref[...], cqn_t)
    cos_t = cost_ref[...]
    sin_t = sint_ref[...]
    for h in range(MLA_HEADS):
        sl = slice(h * LANES, (h + 1) * LANES)
        mqt_ref[0, h] = (qa[sl] * cos_t + qb[sl] * sin_t).astype(BF16)

    ckvn = _rms(seg(SEG_CKV), gkv_ref[...])
    ckvn_t = ckvn.T.astype(BF16)
    ckvn = ckvn.astype(BF16)
    kn = _dot(ckvn, wk_ref[...])
    kr = seg(SEG_KRA) * cos_ref[...] + seg(SEG_KRB) * sin_ref[...]
    for h in range(MLA_HEADS):
        mk_ref[0, h] = (kn[:, h * LANES:(h + 1) * LANES] + kr).astype(BF16)
    vt = _dot(wv_ref[...], ckvn_t)
    for h in range(MLA_HEADS):
        mvt_ref[0, h, 0:HEAD, :] = vt[h * HEAD:(h + 1) * HEAD].astype(BF16)
        mvt_ref[0, h, HEAD:VROWS, :] = ones

    a = seg(SEG_SK_IK)
    for h in range(SB_HEADS):
        sk_ref[0, h] = a[:, h * HEAD:(h + 1) * HEAD].astype(BF16)
    ik_ref[0] = a[:, 320:384].astype(BF16)
    dk_ref[0] = seg(SEG_DK)[:, 0:HEAD].astype(BF16)

    a = seg_t(ROW_DQ) * LOG2E
    for h in range(DSA_HEADS):
        dqt_ref[0, h] = a[h * HEAD:(h + 1) * HEAD].astype(BF16)
    a = seg_t(ROW_IQ)
    for h in range(IDX_HEADS):
        iqt_ref[0, h] = a[h * HEAD:(h + 1) * HEAD].astype(BF16)
    a = seg_t(ROW_SQ) * LOG2E
    for h in range(SB_HEADS):
        sqt_ref[0, h] = a[h * HEAD:(h + 1) * HEAD].astype(BF16)
    a = seg_t(ROW_SV)
    for h in range(SB_HEADS):
        svt_ref[0, h] = a[h * HEAD:(h + 1) * HEAD].astype(BF16)
    dvt_ref[0, 0:HEAD, :] = seg_t(ROW_DV).astype(BF16)
    dvt_ref[0, HEAD:VROWS, :] = ones
    iwt_ref[0] = seg_t(ROW_IW)[0:IDX_HEADS] * IDX_HEADS ** -0.5


def _proj_call(h, g, w, wt, gq, wqa, wqb, gkv, wk, wv, cos_s, sin_s, cos_t, sin_t):
    B, S, D = h.shape
    nt = S // TM
    tok = lambda w_: pl.BlockSpec((1, TM, w_), lambda b, t: (b, t, 0))
    heads = lambda n, w_: pl.BlockSpec((1, n, TM, w_), lambda b, t: (b, 0, t, 0))
    heads_t = lambda n, r: pl.BlockSpec((1, n, r, TM), lambda b, t: (b, 0, 0, t))
    rows_t = lambda r: pl.BlockSpec((1, r, TM), lambda b, t: (b, 0, t))
    hs = lambda n, w_: jax.ShapeDtypeStruct((B, n, S, w_), BF16)
    hts = lambda n, r: jax.ShapeDtypeStruct((B, n, r, S), BF16)
    return pl.pallas_call(
        _proj_kernel,
        grid=(B, nt),
        in_specs=[tok(D), _const_spec(g.shape), _const_spec(w.shape), _const_spec(wt.shape),
                  _const_spec(gq.shape), _const_spec(wqa.shape), _const_spec(wqb.shape),
                  _const_spec(gkv.shape), _const_spec(wk.shape), _const_spec(wv.shape),
                  pl.BlockSpec((TM, LANES), lambda b, t: (t, 0)),
                  pl.BlockSpec((TM, LANES), lambda b, t: (t, 0)),
                  pl.BlockSpec((LANES, TM), lambda b, t: (0, t)),
                  pl.BlockSpec((LANES, TM), lambda b, t: (0, t))],
        out_specs=[heads_t(MLA_HEADS, LANES), heads(MLA_HEADS, LANES), heads_t(MLA_HEADS, VROWS),
                   heads_t(DSA_HEADS, HEAD), tok(HEAD), rows_t(VROWS),
                   heads_t(IDX_HEADS, HEAD), tok(HEAD), rows_t(IDX_HEADS),
                   heads_t(SB_HEADS, HEAD), heads(SB_HEADS, HEAD), heads_t(SB_HEADS, HEAD)],
        out_shape=[hts(MLA_HEADS, LANES), hs(MLA_HEADS, LANES), hts(MLA_HEADS, VROWS),
                   hts(DSA_HEADS, HEAD), jax.ShapeDtypeStruct((B, S, HEAD), BF16),
                   jax.ShapeDtypeStruct((B, VROWS, S), BF16),
                   hts(IDX_HEADS, HEAD), jax.ShapeDtypeStruct((B, S, HEAD), BF16),
                   jax.ShapeDtypeStruct((B, IDX_HEADS, S), F32),
                   hts(SB_HEADS, HEAD), hs(SB_HEADS, HEAD), hts(SB_HEADS, HEAD)],
        compiler_params=_params("parallel", "parallel"),
        name="proj",
    )(h, g, w, wt, gq, wqa, wqb, gkv, wk, wv, cos_s, sin_s, cos_t, sin_t)


def _finish(acc_ref, n_heads, o_ref):
    outs = [acc_ref[h, 0:HEAD, :] / acc_ref[h, HEAD:HEAD + 1, :] for h in range(n_heads)]
    pad = OUT_W - n_heads * HEAD
    if pad:
        outs.append(jnp.zeros((pad, outs[0].shape[1]), F32))
    o_ref[0] = jnp.concatenate(outs, axis=0).T.astype(BF16)


def _mla_kernel(qt_ref, k_ref, vt_ref, o_ref, m_ref, acc_ref, s_ref):
    qi = pl.program_id(1)
    tq, tk = MLA_TQ, MLA_TK
    m_ref[...] = jnp.full(m_ref.shape, NEG, F32)
    acc_ref[...] = jnp.zeros(acc_ref.shape, F32)
    n_full = (qi * tq) // tk
    n_all = ((qi + 1) * tq + tk - 1) // tk

    def qk(h, j):
        k0 = pl.multiple_of(j * tk, tk)
        half = tk // 4
        return jnp.concatenate(
            [_dot(k_ref[0, h, pl.ds(k0 + i * half, half), :], qt_ref[0, h]) for i in range(4)],
            axis=0)

    s_ref[...] = qk(0, 0)

    def step(js, masked):
        items = [(j, h) for j in js for h in range(MLA_HEADS)]
        n = len(items)

        def scores(i, _):
            if i == 0:
                return s_ref[...]
            if i == n:
                return qk(0, jnp.minimum(js[-1] + 1, n_all - 1))
            j, h = items[i]
            return qk(h, j)

        def running_max(i, s):
            if i == n:
                s_ref[...] = s
                return None
            j, h = items[i]
            if masked[i // MLA_HEADS]:
                kpos = j * tk + lax.broadcasted_iota(jnp.int32, (tk, tq), 0)
                qpos = qi * tq + lax.broadcasted_iota(jnp.int32, (tk, tq), 1)
                s = jnp.where(kpos <= qpos, s, NEG)
            m_prev = m_ref[h]
            m_new = jnp.maximum(m_prev, jnp.max(s, axis=0, keepdims=True))
            m_ref[h] = m_new
            return s, m_prev, m_new

        def weights_pv(i, carry):
            if i == n:
                return
            j, h = items[i]
            s, m_prev, m_new = carry
            p = jnp.exp2(s - m_new)
            k0 = pl.multiple_of(j * tk, tk)
            acc_ref[h] = (jnp.exp2(m_prev - m_new) * acc_ref[h]
                          + _dot(vt_ref[0, h, :, pl.ds(k0, tk)], p.astype(BF16)))

        _pipeline(n + 1, [scores, running_max, weights_pv])

    _for_blocks(jnp.maximum(n_full - 1, 0), MLA_UNROLL, lambda js: step(js, [False] * len(js)))

    @pl.when(n_full >= 1)
    def _():
        step([n_full - 1, n_full], [False, True])

    @pl.when(n_full == 0)
    def _():
        step([0], [True])

    _finish(acc_ref, MLA_HEADS, o_ref)


def _mla_call(qt, k, vt):
    B, H, _, S = qt.shape
    res = lambda shape: pl.BlockSpec(shape, lambda b, i: (b, 0, 0, 0), pipeline_mode=pl.Buffered(1))
    return pl.pallas_call(
        _mla_kernel,
        grid=(B, S // MLA_TQ),
        in_specs=[pl.BlockSpec((1, H, LANES, MLA_TQ), lambda b, i: (b, 0, 0, i)),
                  res((1, H, S, LANES)), res((1, H, VROWS, S))],
        out_specs=pl.BlockSpec((1, MLA_TQ, OUT_W), lambda b, i: (b, i, 0)),
        out_shape=jax.ShapeDtypeStruct((B, S, OUT_W), BF16),
        scratch_shapes=[pltpu.VMEM((H, 1, MLA_TQ), F32), pltpu.VMEM((H, VROWS, MLA_TQ), F32),
                        pltpu.VMEM((MLA_TK, MLA_TQ), F32)],
        compiler_params=_params("parallel", "arbitrary"),
        name="mla",
    )(qt, k, vt)


def _sb_kernel(qt_ref, k_ref, vt_ref, u_ref, o_ref, c_ref, acc_ref, z_ref):
    qi = pl.program_id(1)
    t = SB_T
    c_ref[...] = jnp.zeros(c_ref.shape, F32)
    acc_ref[...] = jnp.zeros(acc_ref.shape, F32)

    def qk(h, j):
        k0 = pl.multiple_of(j * t, t)
        half = t // 2
        return jnp.concatenate(
            [_dot(k_ref[0, h, pl.ds(k0 + s * half, half), :], qt_ref[0, h]) for s in range(2)],
            axis=0)

    strict = (lax.broadcasted_iota(jnp.int32, (t, t), 0) < lax.broadcasted_iota(jnp.int32, (t, t), 1))

    def suffix_sums(z2, masked):
        neg_abs = pltpu.bitcast(pltpu.bitcast(z2, jnp.int32) | INT_MIN, F32)
        nl = jnp.maximum(z2, 0.0) + jnp.log2(1.0 + jnp.exp2(neg_abs))
        if masked:
            nl = jnp.where(strict, nl, 0.0)
        nlb = nl.astype(BF16)
        sums = jnp.concatenate([_dot(u_ref[0:t // 2, :], nlb), _dot(u_ref[t // 2:, :], nlb)], axis=0)
        return z2, sums

    z_ref[...] = qk(0, qi)

    def step(js, masked):
        items = [(j, h) for j in js for h in range(SB_HEADS)]
        n = len(items)

        def logits(i, _):
            if i == 0:
                return z_ref[...]
            if i == n:
                return qk(0, jnp.maximum(js[-1] - 1, 0))
            j, h = items[i]
            return qk(h, j)

        def sums_stage(i, z):
            if i == n:
                z_ref[...] = z
                return None
            return suffix_sums(z, masked[i // SB_HEADS])

        def weights_pv(i, carry):
            if i == n:
                return
            j, h = items[i]
            z2, sums = carry
            w = jnp.exp2(z2 - sums[0:t])
            if masked[i // SB_HEADS]:
                w = jnp.where(strict, w, 0.0)
            k0 = pl.multiple_of(j * t, t)
            acc_ref[h] += jnp.exp2(-c_ref[h]) * _dot(vt_ref[0, h, :, pl.ds(k0, t)], w.astype(BF16))
            c_ref[h] += sums[t:t + 1]

        _pipeline(n + 1, [logits, sums_stage, weights_pv])

    @pl.when(qi >= 1)
    def _():
        step([qi, qi - 1], [True, False])

    @pl.when(qi == 0)
    def _():
        step([0], [True])

    _for_blocks(jnp.maximum(qi - 1, 0), SB_UNROLL,
                lambda idx: step([qi - 2 - i for i in idx], [False] * len(idx)))

    outs = [acc_ref[h] for h in range(SB_HEADS)]
    outs.append(jnp.zeros((OUT_W - SB_HEADS * HEAD, t), F32))
    o_ref[0] = jnp.concatenate(outs, axis=0).T.astype(BF16)


def _sb_call(qt, k, vt, u):
    B, H, _, S = qt.shape
    res = lambda shape: pl.BlockSpec(shape, lambda b, i: (b, 0, 0, 0), pipeline_mode=pl.Buffered(1))
    return pl.pallas_call(
        _sb_kernel,
        grid=(B, S // SB_T),
        in_specs=[pl.BlockSpec((1, H, HEAD, SB_T), lambda b, i: (b, 0, 0, i)),
                  res((1, H, S, HEAD)), res((1, H, HEAD, S)), _const_spec(u.shape)],
        out_specs=pl.BlockSpec((1, SB_T, OUT_W), lambda b, i: (b, i, 0)),
        out_shape=jax.ShapeDtypeStruct((B, S, OUT_W), BF16),
        scratch_shapes=[pltpu.VMEM((H, 1, SB_T), F32), pltpu.VMEM((H, HEAD, SB_T), F32),
                        pltpu.VMEM((SB_T, SB_T), F32)],
        compiler_params=_params("parallel", "arbitrary"),
        name="stickbreak",
    )(qt, k, vt, u)


def _bias_kernel(rb_ref, o_ref):
    v = pl.program_id(0)
    shape = (DSA_TK, DSA_TQ)
    off = jnp.where(v < DSA_SUB, v * DSA_TQ, DSA_TK)
    dist = off + lax.broadcasted_iota(jnp.int32, shape, 1) - lax.broadcasted_iota(jnp.int32, shape, 0)
    max_exact = REL_BUCKETS // 2
    d = jnp.maximum(dist, 1).astype(F32)
    large = max_exact + (jnp.log(d / max_exact) / math.log(REL_MAX_DIST / max_exact)
                         * (REL_BUCKETS - max_exact)).astype(jnp.int32)
    large = jnp.minimum(large, REL_BUCKETS - 1)
    bucket = jnp.where(dist < max_exact, dist, large)
    for h in range(DSA_HEADS):
        t = jnp.zeros(shape, F32)
        for b in range(REL_BUCKETS):
            t = jnp.where(bucket == b, (rb_ref[b, h] - rb_ref[REL_BUCKETS - 1, h]) * LOG2E, t)
        o_ref[0, h] = jnp.where(dist >= 0, t, NEG)


def _bias_call(rel_bias):
    nv = DSA_SUB + 1
    return pl.pallas_call(
        _bias_kernel,
        grid=(nv,),
        in_specs=[pl.BlockSpec(memory_space=pltpu.SMEM)],
        out_specs=pl.BlockSpec((1, DSA_HEADS, DSA_TK, DSA_TQ), lambda v: (v, 0, 0, 0)),
        out_shape=jax.ShapeDtypeStruct((nv, DSA_HEADS, DSA_TK, DSA_TQ), F32),
        compiler_params=_params("arbitrary"),
        name="bias_tiles",
    )(rel_bias)


def _dsa_kernel(iqt_ref, ik_ref, iwt_ref, dqt_ref, dk_ref, dvt_ref, tab_ref, o_ref,
                keys_ref, sorted_ref, m_ref, acc_ref, x_ref, *, topk):
    qi = pl.program_id(1)
    n_blk = qi // DSA_SUB + 1
    sub = qi % DSA_SUB
    tq, tk = DSA_TQ, DSA_TK

    iqt = jnp.concatenate([iqt_ref[0, h] for h in range(IDX_HEADS)], axis=1)
    iw = iwt_ref[0]

    def pair_scores(k_ref_, qt_all, g, j, n_heads):
        k0 = pl.multiple_of(j * tk, tk)
        cols = qt_all[:, 2 * g * tq:min(2 * (g + 1), n_heads) * tq]
        half = tk // 2
        return jnp.concatenate(
            [_dot(k_ref_[0, pl.ds(k0 + i * half, half), :], cols) for i in range(2)], axis=0)

    n_slab = tk // SLAB
    n_group = n_slab // SORT_GROUP
    zero_shift = 23 - (keys_ref.shape[0] * tk - 1).bit_length()
    assert zero_shift >= 0
    zero_key = -1 - lax.shift_left(lax.broadcasted_iota(jnp.int32, (tk, tq), 0), zero_shift)

    def sort_groups(key):
        slabs = [key[SLAB * i:SLAB * (i + 1)] for i in range(n_slab)]
        for g in range(n_group):
            v = slabs[SORT_GROUP * g:SORT_GROUP * (g + 1)]
            for a, b in SORT_NETWORK:
                v[a], v[b] = jnp.maximum(v[a], v[b]), jnp.minimum(v[a], v[b])
            slabs[SORT_GROUP * g:SORT_GROUP * (g + 1)] = v
        return jnp.concatenate(slabs, axis=0)

    x_ref[...] = pair_scores(ik_ref, iqt, 0, 0, IDX_HEADS)

    def idx_step(js, masked):
        n_pairs = IDX_HEADS // 2
        items = [(j, g) for j in js for g in range(n_pairs)]
        n = len(items)
        total = {}

        def scores(i, _):
            if i == 0:
                return x_ref[...]
            if i == n:
                return pair_scores(ik_ref, iqt, 0, jnp.minimum(js[-1] + 1, n_blk - 1), IDX_HEADS)
            j, g = items[i]
            return pair_scores(ik_ref, iqt, g, j, IDX_HEADS)

        def weigh(i, x):
            if i == n:
                x_ref[...] = x
                return
            j, g = items[i]
            part = (jnp.maximum(x[:, 0:tq], 0.0) * iw[2 * g:2 * g + 1]
                    + jnp.maximum(x[:, tq:2 * tq], 0.0) * iw[2 * g + 1:2 * g + 2])
            blk = i // n_pairs
            total[blk] = part if g == 0 else total[blk] + part
            if g < n_pairs - 1:
                return
            bits = pltpu.bitcast(total.pop(blk), jnp.int32)
            key = bits ^ (lax.shift_right_arithmetic(bits, 31) & 0x7FFFFFFF)
            key = jnp.where((bits & 0x7F800000) == 0, zero_key - lax.shift_left(j * tk, zero_shift), key)
            if masked:
                kpos = j * tk + lax.broadcasted_iota(jnp.int32, key.shape, 0)
                qpos = qi * tq + lax.broadcasted_iota(jnp.int32, key.shape, 1)
                key = jnp.where(kpos <= qpos, key, NEG_INF_KEY)
            keys_ref[j] = key
            sorted_ref[j] = sort_groups(key)

        _pipeline(n + 1, [scores, weigh])

    _for_blocks(n_blk - 1, DSA_UNROLL, lambda js: idx_step(js, False))
    idx_step([n_blk - 1], True)

    @pl.when(n_blk % 2 == 1)
    def _():
        keys_ref[n_blk] = jnp.full((tk, tq), INT_MIN, jnp.int32)
        sorted_ref[n_blk] = jnp.full((tk, tq), INT_MIN, jnp.int32)

    def count_ge(cand):
        def body(i, acc):
            for j in (2 * i, 2 * i + 1):
                c = jnp.where(keys_ref[j] >= cand, 1.0, 0.0)
                acc = acc + jnp.sum(c.reshape(COUNT_ACC, tk // COUNT_ACC, tq), axis=0)
            return acc
        acc = lax.fori_loop(0, (n_blk + 1) // 2, body, jnp.zeros((tk // COUNT_ACC, tq), F32))
        return jnp.sum(acc, axis=0, keepdims=True)

    def count_ge_sorted(cand):
        def one(flag):
            return jnp.where(flag, 1.0, 0.0)

        def body(i, accs):
            accs = list(accs)
            for j in (2 * i, 2 * i + 1):
                blk = sorted_ref[j]
                for g in range(n_group):
                    s = [blk[SLAB * (SORT_GROUP * g + k):SLAB * (SORT_GROUP * g + k + 1)]
                         for k in range(SORT_GROUP)]
                    b4 = s[3] >= cand
                    b2 = jnp.where(b4, s[5], s[1]) >= cand
                    b1 = jnp.where(b4, jnp.where(b2, s[6], s[4]), jnp.where(b2, s[2], s[0])) >= cand
                    o = 3 * (g % 2)
                    accs[o] += one(b4)
                    accs[o + 1] += one(b2)
                    accs[o + 2] += one(b1)
                for r in range(SORT_GROUP * n_group, n_slab):
                    accs[2] += one(blk[SLAB * r:SLAB * (r + 1)] >= cand)
            return tuple(accs)

        z = jnp.zeros((SLAB, tq), F32)
        a = lax.fori_loop(0, (n_blk + 1) // 2, body, (z,) * 6)
        total = 4.0 * (a[0] + a[3]) + 2.0 * (a[1] + a[4]) + (a[2] + a[5])
        return jnp.sum(total, axis=0, keepdims=True)

    def bit_body(i, state):
        t_u, done, thr_hit = state
        cand_u = t_u | lax.shift_left(jnp.int32(1), 31 - i)
        cand = cand_u ^ INT_MIN
        cnt = count_ge_sorted(cand)
        hit = cnt == topk
        thr_hit = jnp.where(hit, jnp.where(done > 0.0, thr_hit, cand), thr_hit)
        done = jnp.where(hit, 1.0, done)
        return jnp.where(cnt >= topk, cand_u, t_u), done, thr_hit

    zero = jnp.zeros((1, tq), jnp.int32)
    state = (zero, jnp.zeros((1, tq), F32), zero)
    first = 0
    for last in DESCENT_CHECKS + (32,):
        stop = last if first == 0 else jnp.where(jnp.min(state[1]) < 1.0, last, first)
        state = lax.fori_loop(first, stop, bit_body, state)
        first = last
    t_u, done, thr_hit = state
    thr = jnp.where(done > 0.0, thr_hit, t_u ^ INT_MIN)

    tie = jnp.where(done > 0.0, 0.0, jnp.where(thr > NEG_INF_KEY, 1.0, 0.0))

    @pl.when(jnp.max(tie) > 0.0)
    def _():
        need = jnp.where(tie > 0.0, topk - count_ge(thr + 1), float(tk * keys_ref.shape[0] + 1))
        row = lax.broadcasted_iota(jnp.int32, (tk, tk), 0)
        col = lax.broadcasted_iota(jnp.int32, (tk, tk), 1)
        before = jnp.where(col < row, 1.0, 0.0).astype(BF16)

        def body(j, run):
            kb = keys_ref[j]
            eq = jnp.where(kb == thr, 1.0, 0.0)
            rank = _dot(before, eq.astype(BF16)) + run
            keys_ref[j] = jnp.where(kb == thr, jnp.where(rank >= need, thr - 1, kb), kb)
            return run + jnp.sum(eq, axis=0, keepdims=True)

        lax.fori_loop(0, n_blk, body, jnp.zeros((1, tq), F32))

    dqt = jnp.concatenate([dqt_ref[0, h] for h in range(DSA_HEADS)], axis=1)
    m_ref[...] = jnp.full(m_ref.shape, NEG, F32)
    acc_ref[...] = jnp.zeros(acc_ref.shape, F32)

    x_ref[...] = pair_scores(dk_ref, dqt, 0, 0, DSA_HEADS)

    def att_step(js, variants):
        n_pairs = (DSA_HEADS + 1) // 2
        items = [(j, g) for j in js for g in range(n_pairs)]
        n = len(items)

        def scores(i, _):
            if i == 0:
                return x_ref[...]
            if i == n:
                return pair_scores(dk_ref, dqt, 0, jnp.minimum(js[-1] + 1, n_blk - 1), DSA_HEADS)
            j, g = items[i]
            return pair_scores(dk_ref, dqt, g, j, DSA_HEADS)

        def running_max(i, s):
            if i == n:
                x_ref[...] = s
                return None
            j, g = items[i]
            variant = variants[i // n_pairs]
            sel = keys_ref[j] >= thr
            out = []
            for h in range(2 * g, min(2 * (g + 1), DSA_HEADS)):
                a = s[:, (h - 2 * g) * tq:(h - 2 * g + 1) * tq]
                if variant is not None:
                    a = a + tab_ref[variant, h]
                a = jnp.where(sel, a, NEG)
                m_prev = m_ref[h]
                m_new = jnp.maximum(m_prev, jnp.max(a, axis=0, keepdims=True))
                m_ref[h] = m_new
                out.append((h, a, m_prev, m_new))
            return out

        def weights_pv(i, carry):
            if i == n:
                return
            j, _ = items[i]
            vt = dvt_ref[0, :, pl.ds(pl.multiple_of(j * tk, tk), tk)]
            for h, a, m_prev, m_new in carry:
                p = jnp.exp2(a - m_new)
                acc_ref[h] = jnp.exp2(m_prev - m_new) * acc_ref[h] + _dot(vt, p.astype(BF16))

        _pipeline(n + 1, [scores, running_max, weights_pv])

    _for_blocks(jnp.maximum(n_blk - 2, 0), DSA_UNROLL, lambda js: att_step(js, [None] * len(js)))

    @pl.when(jnp.logical_and(n_blk >= 2, sub == 0))
    def _():
        att_step([n_blk - 2, n_blk - 1], [DSA_SUB, 0])

    @pl.when(jnp.logical_and(n_blk >= 2, sub != 0))
    def _():
        att_step([n_blk - 2, n_blk - 1], [None, sub])

    @pl.when(n_blk == 1)
    def _():
        att_step([0], [sub])
    _finish(acc_ref, DSA_HEADS, o_ref)


def _dsa_call(iqt, ik, iwt, dqt, dk, dvt, tab, topk):
    B, _, _, S = iqt.shape
    tq = DSA_TQ
    qh = lambda n: pl.BlockSpec((1, n, HEAD, tq), lambda b, i: (b, 0, 0, i))
    res = lambda shape: pl.BlockSpec(shape, lambda b, i: (b, 0, 0), pipeline_mode=pl.Buffered(1))
    return pl.pallas_call(
        functools.partial(_dsa_kernel, topk=topk),
        grid=(B, S // tq),
        in_specs=[qh(IDX_HEADS), res((1, S, HEAD)),
                  pl.BlockSpec((1, IDX_HEADS, tq), lambda b, i: (b, 0, i)),
                  qh(DSA_HEADS), res((1, S, HEAD)), res((1, VROWS, S)), _const_spec(tab.shape)],
        out_specs=pl.BlockSpec((1, tq, OUT_W), lambda b, i: (b, i, 0)),
        out_shape=jax.ShapeDtypeStruct((B, S, OUT_W), BF16),
        scratch_shapes=[pltpu.VMEM((S // DSA_TK + (S // DSA_TK) % 2, DSA_TK, tq), jnp.int32),
                        pltpu.VMEM((S // DSA_TK + (S // DSA_TK) % 2, DSA_TK, tq), jnp.int32),
                        pltpu.VMEM((DSA_HEADS, 1, tq), F32),
                        pltpu.VMEM((DSA_HEADS, VROWS, tq), F32),
                        pltpu.VMEM((DSA_TK, 2 * tq), F32)],
        compiler_params=_params("parallel", "arbitrary"),
        name="dsa",
    )(iqt, ik, iwt, dqt, dk, dvt, tab)


def _post_kernel(h_ref, om_ref, od_ref, os_ref, p_ref, wo_ref, gf_ref, wg_ref, wu_ref, wd_ref,
                 gp_ref, wpg_ref, wpp_ref, gfin_ref, out_ref, *, final):
    h = (h_ref[0] + _dot(om_ref[0], wo_ref[0]) + _dot(od_ref[0], wo_ref[1])
         + _dot(os_ref[0], wo_ref[2]))

    hf = _rms(h, gf_ref[...]).astype(BF16)
    d_ff = wg_ref.shape[1]
    ffn = jnp.zeros(h.shape, F32)
    for c0 in range(0, d_ff, FF_CHUNK):
        g = _dot(hf, wg_ref[:, c0:c0 + FF_CHUNK])
        u = _dot(hf, wu_ref[:, c0:c0 + FF_CHUNK])
        ffn = ffn + _dot((g * jax.nn.sigmoid(g) * u).astype(BF16), wd_ref[c0:c0 + FF_CHUNK, :])
    h = h + ffn

    gate = jax.nn.sigmoid(_dot(_rms(h, gp_ref[...]).astype(BF16), wpg_ref[...]))
    h = h + gate * _dot(p_ref[0, 0].astype(BF16), wpp_ref[...])
    if final:
        h = _rms(h, gfin_ref[...])
    out_ref[0] = h


def _post_call(h, om, od, os_, p, layer, wo, gf, wg, wu, wd, gp, wpg, wpp, gfin, final):
    B, S, D = h.shape
    mix = pl.BlockSpec((1, TM, OUT_W), lambda b, t: (b, t, 0))
    tok = pl.BlockSpec((1, TM, D), lambda b, t: (b, t, 0))
    return pl.pallas_call(
        functools.partial(_post_kernel, final=final),
        grid=(B, S // TM),
        in_specs=[tok, mix, mix, mix,
                  pl.BlockSpec((1, 1, TM, p.shape[-1]), lambda b, t: (layer, b, t, 0)),
                  _const_spec(wo.shape), _const_spec(gf.shape), _const_spec(wg.shape),
                  _const_spec(wu.shape), _const_spec(wd.shape), _const_spec(gp.shape),
                  _const_spec(wpg.shape), _const_spec(wpp.shape), _const_spec(gfin.shape)],
        out_specs=tok,
        out_shape=jax.ShapeDtypeStruct((B, S, D), F32),
        compiler_params=_params("parallel", "parallel"),
        name="post",
    )(h, om, od, os_, p, wo, gf, wg, wu, wd, gp, wpg, wpp, gfin)


def _swap_halves(w):
    half = w.shape[-1] // 2
    return jnp.concatenate([w[..., half:], w[..., :half]], axis=-1)


def _pack_w_in(w_in):
    L, D, _ = w_in.shape
    z = lambda n: jnp.zeros((L, D, n), F32)
    o = 0
    cols = {}
    for name, n in (("cq", MLA_Q_LORA), ("ckv", MLA_KV_LORA), ("kr", MLA_ROPE),
                    ("dq", DSA_HEADS * DSA_DIM), ("dk", DSA_DIM), ("dv", DSA_DIM),
                    ("iq", IDX_HEADS * IDX_DIM), ("ik", IDX_DIM), ("iw", IDX_HEADS),
                    ("sq", SB_HEADS * SB_DIM), ("sk", SB_HEADS * SB_DIM), ("sv", SB_HEADS * SB_DIM)):
        cols[name] = w_in[:, :, o:o + n]
        o += n
    pad_r = LANES - MLA_NOPE - MLA_ROPE
    std = jnp.concatenate([
        cols["cq"], cols["ckv"],
        z(MLA_NOPE), cols["kr"], z(pad_r),
        z(MLA_NOPE), _swap_halves(cols["kr"]), z(pad_r),
        cols["sk"], cols["ik"],
        cols["dk"], z(LANES - DSA_DIM)], axis=-1)
    tr = jnp.concatenate([
        cols["dq"] * DSA_DIM ** -0.5, cols["iq"] * IDX_DIM ** -0.5, cols["sq"] * SB_DIM ** -0.5,
        cols["sv"], cols["dv"], cols["iw"], z(ROW_IW[1] - ROW_IW[0] - IDX_HEADS)], axis=-1)
    assert std.shape[-1] == W_STD_COLS and tr.shape[-1] == W_T_ROWS
    return std.astype(BF16), jnp.swapaxes(tr, 1, 2).astype(BF16)


def _pack_mla(w_uq, w_ukv):
    L = w_uq.shape[0]
    dq = MLA_NOPE + MLA_ROPE
    pad_r = LANES - dq
    uq = w_uq.reshape(L, MLA_Q_LORA, MLA_HEADS, dq)
    zq = lambda n: jnp.zeros((L, MLA_Q_LORA, MLA_HEADS, n), F32)
    wqa = jnp.concatenate([uq, zq(pad_r)], axis=-1)
    wqb = jnp.concatenate([zq(MLA_NOPE), _swap_halves(uq[..., MLA_NOPE:]), zq(pad_r)], axis=-1)
    ukv = w_ukv.reshape(L, MLA_KV_LORA, MLA_HEADS, MLA_NOPE + MLA_V)
    wk = jnp.concatenate([ukv[..., :MLA_NOPE],
                          jnp.zeros((L, MLA_KV_LORA, MLA_HEADS, LANES - MLA_NOPE), F32)], axis=-1)
    wv = ukv[..., MLA_NOPE:]
    flat = lambda a: a.reshape(L, a.shape[1], -1)
    tr = lambda a: jnp.swapaxes(flat(a), 1, 2).astype(BF16)
    return tr(wqa), tr(wqb), flat(wk).astype(BF16), tr(wv)


def _rope_tables(S):
    half = MLA_ROPE // 2
    inv = ROPE_THETA ** (-jnp.arange(half, dtype=F32) / half)
    ang = jnp.arange(S, dtype=jnp.int32).astype(F32)[:, None] * inv[None, :]
    cos, sin = jnp.cos(ang), jnp.sin(ang)
    pad_r = LANES - MLA_NOPE - MLA_ROPE
    cos_t = jnp.concatenate([jnp.ones((S, MLA_NOPE), F32), cos, cos, jnp.zeros((S, pad_r), F32)], axis=-1)
    sin_t = jnp.concatenate([jnp.zeros((S, MLA_NOPE), F32), -sin, sin, jnp.zeros((S, pad_r), F32)], axis=-1)
    return cos_t, sin_t


def _pack_w_o(w_o):
    L, _, D = w_o.shape
    a = MLA_HEADS * MLA_V
    b = a + DSA_HEADS * DSA_DIM
    pad = jnp.zeros((L, OUT_W - DSA_HEADS * DSA_DIM, D), F32)
    blocks = [w_o[:, :a], jnp.concatenate([w_o[:, a:b], pad], axis=1),
              jnp.concatenate([w_o[:, b:], pad], axis=1)]
    return jnp.stack(blocks, axis=1).astype(BF16)


def kernel(x, p, w_in, attn_norm, mla_q_norm, mla_w_uq, mla_kv_norm, mla_w_ukv, rel_bias, w_o,
           ffn_norm, w_gate, w_up, w_down, ple_norm, w_ple_gate, w_ple_proj, final_norm):
    B, S, D = x.shape
    depth = w_in.shape[0]
    assert S % TM == 0 and S % DSA_TK == 0 and S % MLA_TK == 0
    assert w_gate.shape[-1] % FF_CHUNK == 0
    topk = min(DSA_TOPK, S // 4)

    w_std, w_tr = _pack_w_in(w_in)
    wqa, wqb, wk, wv = _pack_mla(mla_w_uq, mla_w_ukv)
    cos_s, sin_s = _rope_tables(S)
    q_scale = (MLA_NOPE + MLA_ROPE) ** -0.5 * LOG2E
    cos_t, sin_t = cos_s.T * q_scale, sin_s.T * q_scale
    row = lambda g: g.reshape(depth, 1, -1)
    g_attn, g_q, g_kv, g_ffn, g_ple = map(row, (attn_norm, mla_q_norm, mla_kv_norm, ffn_norm, ple_norm))
    g_fin = final_norm.reshape(1, -1)
    wo = _pack_w_o(w_o)
    wg, wu, wd, wpg, wpp = (a.astype(BF16) for a in (w_gate, w_up, w_down, w_ple_gate, w_ple_proj))
    tri = jnp.triu(jnp.ones((SB_T, SB_T), F32), 0)
    tail = jnp.zeros((16, SB_T), F32).at[0].set(1.0)
    u = jnp.concatenate([tri, tail], axis=0).astype(BF16)

    tab = _bias_call(rel_bias)

    h = x
    for i in range(depth):
        (mqt, mk, mvt, dqt, dk, dvt, iqt, ik, iwt, sqt, sk, svt) = _proj_call(
            h, g_attn[i], w_std[i], w_tr[i], g_q[i], wqa[i], wqb[i], g_kv[i], wk[i], wv[i],
            cos_s, sin_s, cos_t, sin_t)
        o_mla = _mla_call(mqt, mk, mvt)
        o_dsa = _dsa_call(iqt, ik, iwt, dqt, dk, dvt, tab, topk)
        o_sb = _sb_call(sqt, sk, svt, u)
        h = _post_call(h, o_mla, o_dsa, o_sb, p, i, wo[i], g_ffn[i], wg[i], wu[i], wd[i],
                       g_ple[i], wpg[i], wpp[i], g_fin, final=(i == depth - 1))
    return h
```

```python
import functools
import math

import jax
import jax.numpy as jnp
from jax import lax
from jax.experimental import pallas as pl
from jax.experimental.pallas import tpu as pltpu

EPS = 1e-6
MLA_HEADS = 6
MLA_Q_LORA = 384
MLA_KV_LORA = 256
MLA_NOPE = 64
MLA_ROPE = 32
MLA_V = 64
ROPE_THETA = 10000.0
DSA_HEADS = 5
DSA_DIM = 64
IDX_HEADS = 8
IDX_DIM = 64
DSA_TOPK = 256
SB_HEADS = 5
SB_DIM = 64
REL_BUCKETS = 32
REL_MAX_DIST = 128

LANES = 128
HEAD = 64
VROWS = 80
OUT_W = 384
VMEM_LIMIT = 56 * 1024 * 1024

TM = 512
MLA_TQ, MLA_TK = 256, 512
MLA_UNROLL = 4
SB_T = 256
SB_UNROLL = 8
DSA_TQ, DSA_TK = 128, 512
DSA_SUB = DSA_TK // DSA_TQ
DSA_UNROLL = 4
COUNT_ACC = 8
DESCENT_CHECKS = (24, 26, 28)
SLAB = 8
SORT_GROUP = 7
SORT_NETWORK = ((0, 6), (2, 3), (4, 5), (0, 2), (1, 4), (3, 6), (0, 1), (2, 5), (3, 4),
                (1, 2), (4, 6), (2, 3), (4, 5), (1, 2), (3, 4), (5, 6))
FF_CHUNK = 1408

NEG = -1e30
INT_MIN = -2 ** 31
NEG_INF_KEY = -2139095041
LOG2E = math.log2(math.e)

F32 = jnp.float32
BF16 = jnp.bfloat16

SEG_CQ = (0, 384)
SEG_CKV = (384, 640)
SEG_KRA = (640, 768)
SEG_KRB = (768, 896)
SEG_SK_IK = (896, 1280)
SEG_DK = (1280, 1408)
W_STD_COLS = 1408
ROW_DQ = (0, 320)
ROW_IQ = (320, 832)
ROW_SQ = (832, 1152)
ROW_SV = (1152, 1472)
ROW_DV = (1472, 1536)
ROW_IW = (1536, 1552)
W_T_ROWS = 1552


def _rms(x, g):
    return x * lax.rsqrt(jnp.mean(x * x, axis=-1, keepdims=True) + EPS) * g


def _dot(a, b):
    return jnp.dot(a, b, preferred_element_type=F32)


def _const_spec(shape):
    n = len(shape)
    return pl.BlockSpec(shape, lambda *_: (0,) * n, pipeline_mode=pl.Buffered(1))


def _params(*sem):
    return pltpu.CompilerParams(dimension_semantics=sem, vmem_limit_bytes=VMEM_LIMIT)


def _pipeline(n, stages):
    held = [dict() for _ in stages]
    for t in range(n + len(stages) - 1):
        for s, stage in enumerate(stages):
            i = t - s
            if 0 <= i < n:
                held[s][i] = stage(i, held[s - 1].pop(i) if s else None)


def _for_blocks(n, unroll, step):
    def body(i, c):
        step([unroll * i + d for d in range(unroll)])
        return c

    lax.fori_loop(0, n // unroll, body, 0)
    rem = n % unroll
    group = unroll // 2
    while group:
        @pl.when(rem & group != 0)
        def _(group=group):
            base = n - (rem & (2 * group - 1))
            step([base + d for d in range(group)])
        group //= 2


def _ones_rows(n):
    r = lax.broadcasted_iota(jnp.int32, (VROWS - HEAD, n), 0)
    return jnp.where(r == 0, 1.0, 0.0).astype(BF16)


def _proj_kernel(h_ref, g_ref, w_ref, wt_ref, gq_ref, wqa_ref, wqb_ref, gkv_ref, wk_ref, wv_ref,
                 cos_ref, sin_ref, cost_ref, sint_ref,
                 mqt_ref, mk_ref, mvt_ref, dqt_ref, dk_ref, dvt_ref, iqt_ref, ik_ref, iwt_ref,
                 sqt_ref, sk_ref, svt_ref):
    hn = _rms(h_ref[0], g_ref[...])
    hnt = hn.T.astype(BF16)
    hn = hn.astype(BF16)
    tm = hn.shape[0]
    ones = _ones_rows(tm)

    def seg(ab):
        return _dot(hn, w_ref[:, ab[0]:ab[1]])

    def seg_t(ab):
        return _dot(wt_ref[ab[0]:ab[1], :], hnt)

    cqn_t = _rms(seg(SEG_CQ), gq_ref[...]).T.astype(BF16)
    qa = _dot(wqa_ref[...], cqn_t)
    qb = _dot(wqb_ref[...], cqn_t)
    cos_t = cost_ref[...]
    sin_t = sint_ref[...]
    for h in range(MLA_HEADS):
        sl = slice(h * LANES, (h + 1) * LANES)
        mqt_ref[0, h] = (qa[sl] * cos_t + qb[sl] * sin_t).astype(BF16)

    ckvn = _rms(seg(SEG_CKV), gkv_ref[...])
    ckvn_t = ckvn.T.astype(BF16)
    ckvn = ckvn.astype(BF16)
    kn = _dot(ckvn, wk_ref[...])
    kr = seg(SEG_KRA) * cos_ref[...] + seg(SEG_KRB) * sin_ref[...]
    for h in range(MLA_HEADS):
        mk_ref[0, h] = (kn[:, h * LANES:(h + 1) * LANES] + kr).astype(BF16)
    vt = _dot(wv_ref[...], ckvn_t)
    for h in range(MLA_HEADS):
        mvt_ref[0, h, 0:HEAD, :] = vt[h * HEAD:(h + 1) * HEAD].astype(BF16)
        mvt_ref[0, h, HEAD:VROWS, :] = ones

    a = seg(SEG_SK_IK)
    for h in range(SB_HEADS):
        sk_ref[0, h] = a[:, h * HEAD:(h + 1) * HEAD].astype(BF16)
    ik_ref[0] = a[:, 320:384].astype(BF16)
    dk_ref[0] = seg(SEG_DK)[:, 0:HEAD].astype(BF16)

    a = seg_t(ROW_DQ) * LOG2E
    for h in range(DSA_HEADS):
        dqt_ref[0, h] = a[h * HEAD:(h + 1) * HEAD].astype(BF16)
    a = seg_t(ROW_IQ)
    for h in range(IDX_HEADS):
        iqt_ref[0, h] = a[h * HEAD:(h + 1) * HEAD].astype(BF16)
    a = seg_t(ROW_SQ) * LOG2E
    for h in range(SB_HEADS):
        sqt_ref[0, h] = a[h * HEAD:(h + 1) * HEAD].astype(BF16)
    a = seg_t(ROW_SV)
    for h in range(SB_HEADS):
        svt_ref[0, h] = a[h * HEAD:(h + 1) * HEAD].astype(BF16)
    dvt_ref[0, 0:HEAD, :] = seg_t(ROW_DV).astype(BF16)
    dvt_ref[0, HEAD:VROWS, :] = ones
    iwt_ref[0] = seg_t(ROW_IW)[0:IDX_HEADS] * IDX_HEADS ** -0.5


def _proj_call(h, g, w, wt, gq, wqa, wqb, gkv, wk, wv, cos_s, sin_s, cos_t, sin_t):
    B, S, D = h.shape
    nt = S // TM
    tok = lambda w_: pl.BlockSpec((1, TM, w_), lambda b, t: (b, t, 0))
    heads = lambda n, w_: pl.BlockSpec((1, n, TM, w_), lambda b, t: (b, 0, t, 0))
    heads_t = lambda n, r: pl.BlockSpec((1, n, r, TM), lambda b, t: (b, 0, 0, t))
    rows_t = lambda r: pl.BlockSpec((1, r, TM), lambda b, t: (b, 0, t))
    hs = lambda n, w_: jax.ShapeDtypeStruct((B, n, S, w_), BF16)
    hts = lambda n, r: jax.ShapeDtypeStruct((B, n, r, S), BF16)
    return pl.pallas_call(
        _proj_kernel,
        grid=(B, nt),
        in_specs=[tok(D), _const_spec(g.shape), _const_spec(w.shape), _const_spec(wt.shape),
                  _const_spec(gq.shape), _const_spec(wqa.shape), _const_spec(wqb.shape),
                  _const_spec(gkv.shape), _const_spec(wk.shape), _const_spec(wv.shape),
                  pl.BlockSpec((TM, LANES), lambda b, t: (t, 0)),
                  pl.BlockSpec((TM, LANES), lambda b, t: (t, 0)),
                  pl.BlockSpec((LANES, TM), lambda b, t: (0, t)),
                  pl.BlockSpec((LANES, TM), lambda b, t: (0, t))],
        out_specs=[heads_t(MLA_HEADS, LANES), heads(MLA_HEADS, LANES), heads_t(MLA_HEADS, VROWS),
                   heads_t(DSA_HEADS, HEAD), tok(HEAD), rows_t(VROWS),
                   heads_t(IDX_HEADS, HEAD), tok(HEAD), rows_t(IDX_HEADS),
                   heads_t(SB_HEADS, HEAD), heads(SB_HEADS, HEAD), heads_t(SB_HEADS, HEAD)],
        out_shape=[hts(MLA_HEADS, LANES), hs(MLA_HEADS, LANES), hts(MLA_HEADS, VROWS),
                   hts(DSA_HEADS, HEAD), jax.ShapeDtypeStruct((B, S, HEAD), BF16),
                   jax.ShapeDtypeStruct((B, VROWS, S), BF16),
                   hts(IDX_HEADS, HEAD), jax.ShapeDtypeStruct((B, S, HEAD), BF16),
                   jax.ShapeDtypeStruct((B, IDX_HEADS, S), F32),
                   hts(SB_HEADS, HEAD), hs(SB_HEADS, HEAD), hts(SB_HEADS, HEAD)],
        compiler_params=_params("parallel", "parallel"),
        name="proj",
    )(h, g, w, wt, gq, wqa, wqb, gkv, wk, wv, cos_s, sin_s, cos_t, sin_t)


def _finish(head_acc, n_heads, o_ref):
    accs = [head_acc(h) for h in range(n_heads)]
    outs = [a[0:HEAD, :] / a[HEAD:HEAD + 1, :] for a in accs]
    pad = OUT_W - n_heads * HEAD
    if pad:
        outs.append(jnp.zeros((pad, outs[0].shape[1]), F32))
    o_ref[0] = jnp.concatenate(outs, axis=0).T.astype(BF16)


def _mla_kernel(qt_ref, k_ref, vt_ref, o_ref, m_ref, acc_ref, s_ref):
    qi = pl.program_id(1)
    tq, tk = MLA_TQ, MLA_TK
    m_ref[...] = jnp.full(m_ref.shape, NEG, F32)
    acc_ref[...] = jnp.zeros(acc_ref.shape, F32)
    n_full = (qi * tq) // tk
    n_all = ((qi + 1) * tq + tk - 1) // tk

    def qk(h, j):
        k0 = pl.multiple_of(j * tk, tk)
        half = tk // 4
        return jnp.concatenate(
            [_dot(k_ref[0, h, pl.ds(k0 + i * half, half), :], qt_ref[0, h]) for i in range(4)],
            axis=0)

    s_ref[...] = qk(0, 0)

    def step(js, masked):
        items = [(j, h) for j in js for h in range(MLA_HEADS)]
        n = len(items)

        def scores(i, _):
            if i == 0:
                return s_ref[...]
            if i == n:
                return qk(0, jnp.minimum(js[-1] + 1, n_all - 1))
            j, h = items[i]
            return qk(h, j)

        def running_max(i, s):
            if i == n:
                s_ref[...] = s
                return None
            j, h = items[i]
            if masked[i // MLA_HEADS]:
                kpos = j * tk + lax.broadcasted_iota(jnp.int32, (tk, tq), 0)
                qpos = qi * tq + lax.broadcasted_iota(jnp.int32, (tk, tq), 1)
                s = jnp.where(kpos <= qpos, s, NEG)
            m_prev = m_ref[h]
            m_new = jnp.maximum(m_prev, jnp.max(s, axis=0, keepdims=True))
            m_ref[h] = m_new
            return s, m_prev, m_new

        def weights_pv(i, carry):
            if i == n:
                return
            j, h = items[i]
            s, m_prev, m_new = carry
            p = jnp.exp2(s - m_new)
            k0 = pl.multiple_of(j * tk, tk)
            acc_ref[h] = (jnp.exp2(m_prev - m_new) * acc_ref[h]
                          + _dot(vt_ref[0, h, :, pl.ds(k0, tk)], p.astype(BF16)))

        _pipeline(n + 1, [scores, running_max, weights_pv])

    _for_blocks(jnp.maximum(n_full - 1, 0), MLA_UNROLL, lambda js: step(js, [False] * len(js)))

    @pl.when(n_full >= 1)
    def _():
        step([n_full - 1, n_full], [False, True])

    @pl.when(n_full == 0)
    def _():
        step([0], [True])

    _finish(lambda h: acc_ref[h], MLA_HEADS, o_ref)


def _mla_call(qt, k, vt):
    B, H, _, S = qt.shape
    res = lambda shape: pl.BlockSpec(shape, lambda b, i: (b, 0, 0, 0), pipeline_mode=pl.Buffered(1))
    return pl.pallas_call(
        _mla_kernel,
        grid=(B, S // MLA_TQ),
        in_specs=[pl.BlockSpec((1, H, LANES, MLA_TQ), lambda b, i: (b, 0, 0, i)),
                  res((1, H, S, LANES)), res((1, H, VROWS, S))],
        out_specs=pl.BlockSpec((1, MLA_TQ, OUT_W), lambda b, i: (b, i, 0)),
        out_shape=jax.ShapeDtypeStruct((B, S, OUT_W), BF16),
        scratch_shapes=[pltpu.VMEM((H, 1, MLA_TQ), F32), pltpu.VMEM((H, VROWS, MLA_TQ), F32),
                        pltpu.VMEM((MLA_TK, MLA_TQ), F32)],
        compiler_params=_params("parallel", "arbitrary"),
        name="mla",
    )(qt, k, vt)


def _sb_kernel(qt_ref, k_ref, vt_ref, u_ref, o_ref, c_ref, acc_ref, z_ref):
    qi = pl.program_id(1)
    t = SB_T
    c_ref[...] = jnp.zeros(c_ref.shape, F32)
    acc_ref[...] = jnp.zeros(acc_ref.shape, F32)

    def qk(h, j):
        k0 = pl.multiple_of(j * t, t)
        half = t // 2
        return jnp.concatenate(
            [_dot(k_ref[0, h, pl.ds(k0 + s * half, half), :], qt_ref[0, h]) for s in range(2)],
            axis=0)

    strict = (lax.broadcasted_iota(jnp.int32, (t, t), 0) < lax.broadcasted_iota(jnp.int32, (t, t), 1))

    def suffix_sums(z2, masked):
        neg_abs = pltpu.bitcast(pltpu.bitcast(z2, jnp.int32) | INT_MIN, F32)
        nl = jnp.maximum(z2, 0.0) + jnp.log2(1.0 + jnp.exp2(neg_abs))
        if masked:
            nl = jnp.where(strict, nl, 0.0)
        nlb = nl.astype(BF16)
        sums = jnp.concatenate([_dot(u_ref[0:t // 2, :], nlb), _dot(u_ref[t // 2:, :], nlb)], axis=0)
        return z2, sums

    z_ref[...] = qk(0, qi)

    def step(js, masked):
        items = [(j, h) for j in js for h in range(SB_HEADS)]
        n = len(items)

        def logits(i, _):
            if i == 0:
                return z_ref[...]
            if i == n:
                return qk(0, jnp.maximum(js[-1] - 1, 0))
            j, h = items[i]
            return qk(h, j)

        def sums_stage(i, z):
            if i == n:
                z_ref[...] = z
                return None
            return suffix_sums(z, masked[i // SB_HEADS])

        def weights_pv(i, carry):
            if i == n:
                return
            j, h = items[i]
            z2, sums = carry
            w = jnp.exp2(z2 - sums[0:t])
            if masked[i // SB_HEADS]:
                w = jnp.where(strict, w, 0.0)
            k0 = pl.multiple_of(j * t, t)
            acc_ref[h] += jnp.exp2(-c_ref[h]) * _dot(vt_ref[0, h, :, pl.ds(k0, t)], w.astype(BF16))
            c_ref[h] += sums[t:t + 1]

        _pipeline(n + 1, [logits, sums_stage, weights_pv])

    @pl.when(qi >= 1)
    def _():
        step([qi, qi - 1], [True, False])

    @pl.when(qi == 0)
    def _():
        step([0], [True])

    _for_blocks(jnp.maximum(qi - 1, 0), SB_UNROLL,
                lambda idx: step([qi - 2 - i for i in idx], [False] * len(idx)))

    outs = [acc_ref[h] for h in range(SB_HEADS)]
    outs.append(jnp.zeros((OUT_W - SB_HEADS * HEAD, t), F32))
    o_ref[0] = jnp.concatenate(outs, axis=0).T.astype(BF16)


def _sb_call(qt, k, vt, u):
    B, H, _, S = qt.shape
    res = lambda shape: pl.BlockSpec(shape, lambda b, i: (b, 0, 0, 0), pipeline_mode=pl.Buffered(1))
    return pl.pallas_call(
        _sb_kernel,
        grid=(B, S // SB_T),
        in_specs=[pl.BlockSpec((1, H, HEAD, SB_T), lambda b, i: (b, 0, 0, i)),
                  res((1, H, S, HEAD)), res((1, H, HEAD, S)), _const_spec(u.shape)],
        out_specs=pl.BlockSpec((1, SB_T, OUT_W), lambda b, i: (b, i, 0)),
        out_shape=jax.ShapeDtypeStruct((B, S, OUT_W), BF16),
        scratch_shapes=[pltpu.VMEM((H, 1, SB_T), F32), pltpu.VMEM((H, HEAD, SB_T), F32),
                        pltpu.VMEM((SB_T, SB_T), F32)],
        compiler_params=_params("parallel", "arbitrary"),
        name="stickbreak",
    )(qt, k, vt, u)


def _bias_kernel(rb_ref, o_ref):
    v = pl.program_id(0)
    shape = (DSA_TK, DSA_TQ)
    off = jnp.where(v < DSA_SUB, v * DSA_TQ, DSA_TK)
    dist = off + lax.broadcasted_iota(jnp.int32, shape, 1) - lax.broadcasted_iota(jnp.int32, shape, 0)
    max_exact = REL_BUCKETS // 2
    d = jnp.maximum(dist, 1).astype(F32)
    large = max_exact + (jnp.log(d / max_exact) / math.log(REL_MAX_DIST / max_exact)
                         * (REL_BUCKETS - max_exact)).astype(jnp.int32)
    large = jnp.minimum(large, REL_BUCKETS - 1)
    bucket = jnp.where(dist < max_exact, dist, large)
    for h in range(DSA_HEADS):
        t = jnp.zeros(shape, F32)
        for b in range(REL_BUCKETS):
            t = jnp.where(bucket == b, (rb_ref[b, h] - rb_ref[REL_BUCKETS - 1, h]) * LOG2E, t)
        o_ref[0, h] = jnp.where(dist >= 0, t, NEG)


def _bias_call(rel_bias):
    nv = DSA_SUB + 1
    return pl.pallas_call(
        _bias_kernel,
        grid=(nv,),
        in_specs=[pl.BlockSpec(memory_space=pltpu.SMEM)],
        out_specs=pl.BlockSpec((1, DSA_HEADS, DSA_TK, DSA_TQ), lambda v: (v, 0, 0, 0)),
        out_shape=jax.ShapeDtypeStruct((nv, DSA_HEADS, DSA_TK, DSA_TQ), F32),
        compiler_params=_params("arbitrary"),
        name="bias_tiles",
    )(rel_bias)


def _dsa_kernel(iqt_ref, ik_ref, iwt_ref, dqt_ref, dk_ref, dvt_ref, tab_ref, o_ref,
                keys_ref, sorted_ref, m_ref, acc_ref, x_ref, *, topk):
    qi = pl.program_id(1)
    n_blk = qi // DSA_SUB + 1
    sub = qi % DSA_SUB
    tq, tk = DSA_TQ, DSA_TK

    iqt = jnp.concatenate([iqt_ref[0, h] for h in range(IDX_HEADS)], axis=1)
    iw = iwt_ref[0]

    def pair_scores(k_ref_, qt_all, g, j, n_heads):
        k0 = pl.multiple_of(j * tk, tk)
        cols = qt_all[:, 2 * g * tq:min(2 * (g + 1), n_heads) * tq]
        half = tk // 2
        return jnp.concatenate(
            [_dot(k_ref_[0, pl.ds(k0 + i * half, half), :], cols) for i in range(2)], axis=0)

    n_slab = tk // SLAB
    n_group = n_slab // SORT_GROUP
    zero_shift = 23 - (keys_ref.shape[0] * tk - 1).bit_length()
    assert zero_shift >= 0
    zero_key = -1 - lax.shift_left(lax.broadcasted_iota(jnp.int32, (tk, tq), 0), zero_shift)

    def sort_groups(key):
        slabs = [key[SLAB * i:SLAB * (i + 1)] for i in range(n_slab)]
        for g in range(n_group):
            v = slabs[SORT_GROUP * g:SORT_GROUP * (g + 1)]
            for a, b in SORT_NETWORK:
                v[a], v[b] = jnp.maximum(v[a], v[b]), jnp.minimum(v[a], v[b])
            slabs[SORT_GROUP * g:SORT_GROUP * (g + 1)] = v
        return jnp.concatenate(slabs, axis=0)

    x_ref[...] = pair_scores(ik_ref, iqt, 0, 0, IDX_HEADS)

    def idx_step(js, masked):
        n_pairs = IDX_HEADS // 2
        items = [(j, g) for j in js for g in range(n_pairs)]
        n = len(items)
        total = {}

        def scores(i, _):
            if i == 0:
                return x_ref[...]
            if i == n:
                return pair_scores(ik_ref, iqt, 0, jnp.minimum(js[-1] + 1, n_blk - 1), IDX_HEADS)
            j, g = items[i]
            return pair_scores(ik_ref, iqt, g, j, IDX_HEADS)

        def weigh(i, x):
            if i == n:
                x_ref[...] = x
                return
            j, g = items[i]
            part = (jnp.maximum(x[:, 0:tq], 0.0) * iw[2 * g:2 * g + 1]
                    + jnp.maximum(x[:, tq:2 * tq], 0.0) * iw[2 * g + 1:2 * g + 2])
            blk = i // n_pairs
            total[blk] = part if g == 0 else total[blk] + part
            if g < n_pairs - 1:
                return
            bits = pltpu.bitcast(total.pop(blk), jnp.int32)
            key = bits ^ (lax.shift_right_arithmetic(bits, 31) & 0x7FFFFFFF)
            key = jnp.where((bits & 0x7F800000) == 0, zero_key - lax.shift_left(j * tk, zero_shift), key)
            if masked:
                kpos = j * tk + lax.broadcasted_iota(jnp.int32, key.shape, 0)
                qpos = qi * tq + lax.broadcasted_iota(jnp.int32, key.shape, 1)
                key = jnp.where(kpos <= qpos, key, NEG_INF_KEY)
            keys_ref[j] = key
            sorted_ref[j] = sort_groups(key)

        _pipeline(n + 1, [scores, weigh])

    _for_blocks(n_blk - 1, DSA_UNROLL, lambda js: idx_step(js, False))
    idx_step([n_blk - 1], True)

    @pl.when(n_blk % 2 == 1)
    def _():
        keys_ref[n_blk] = jnp.full((tk, tq), INT_MIN, jnp.int32)
        sorted_ref[n_blk] = jnp.full((tk, tq), INT_MIN, jnp.int32)

    def count_ge(cand):
        def body(i, acc):
            for j in (2 * i, 2 * i + 1):
                c = jnp.where(keys_ref[j] >= cand, 1.0, 0.0)
                acc = acc + jnp.sum(c.reshape(COUNT_ACC, tk // COUNT_ACC, tq), axis=0)
            return acc
        acc = lax.fori_loop(0, (n_blk + 1) // 2, body, jnp.zeros((tk // COUNT_ACC, tq), F32))
        return jnp.sum(acc, axis=0, keepdims=True)

    def count_ge_sorted(cand):
        def one(flag):
            return jnp.where(flag, 1.0, 0.0)

        def body(i, accs):
            accs = list(accs)
            for j in (2 * i, 2 * i + 1):
                blk = sorted_ref[j]
                for g in range(n_group):
                    s = [blk[SLAB * (SORT_GROUP * g + k):SLAB * (SORT_GROUP * g + k + 1)]
                         for k in range(SORT_GROUP)]
                    b4 = s[3] >= cand
                    b2 = jnp.where(b4, s[5], s[1]) >= cand
                    b1 = jnp.where(b4, jnp.where(b2, s[6], s[4]), jnp.where(b2, s[2], s[0])) >= cand
                    o = 3 * (g % 2)
                    accs[o] += one(b4)
                    accs[o + 1] += one(b2)
                    accs[o + 2] += one(b1)
                for r in range(SORT_GROUP * n_group, n_slab):
                    accs[2] += one(blk[SLAB * r:SLAB * (r + 1)] >= cand)
            return tuple(accs)

        z = jnp.zeros((SLAB, tq), F32)
        a = lax.fori_loop(0, (n_blk + 1) // 2, body, (z,) * 6)
        total = 4.0 * (a[0] + a[3]) + 2.0 * (a[1] + a[4]) + (a[2] + a[5])
        return jnp.sum(total, axis=0, keepdims=True)

    def bit_body(i, state):
        t_u, done, thr_hit = state
        cand_u = t_u | lax.shift_left(jnp.int32(1), 31 - i)
        cand = cand_u ^ INT_MIN
        cnt = count_ge_sorted(cand)
        hit = cnt == topk
        thr_hit = jnp.where(hit, jnp.where(done > 0.0, thr_hit, cand), thr_hit)
        done = jnp.where(hit, 1.0, done)
        return jnp.where(cnt >= topk, cand_u, t_u), done, thr_hit

    zero = jnp.zeros((1, tq), jnp.int32)
    state = (zero, jnp.zeros((1, tq), F32), zero)
    first = 0
    for last in DESCENT_CHECKS + (32,):
        stop = last if first == 0 else jnp.where(jnp.min(state[1]) < 1.0, last, first)
        state = lax.fori_loop(first, stop, bit_body, state)
        first = last
    t_u, done, thr_hit = state
    thr = jnp.where(done > 0.0, thr_hit, t_u ^ INT_MIN)

    tie = jnp.where(done > 0.0, 0.0, jnp.where(thr > NEG_INF_KEY, 1.0, 0.0))

    @pl.when(jnp.max(tie) > 0.0)
    def _():
        need = jnp.where(tie > 0.0, topk - count_ge(thr + 1), float(tk * keys_ref.shape[0] + 1))
        row = lax.broadcasted_iota(jnp.int32, (tk, tk), 0)
        col = lax.broadcasted_iota(jnp.int32, (tk, tk), 1)
        before = jnp.where(col < row, 1.0, 0.0).astype(BF16)

        def body(j, run):
            kb = keys_ref[j]
            eq = jnp.where(kb == thr, 1.0, 0.0)
            rank = _dot(before, eq.astype(BF16)) + run
            keys_ref[j] = jnp.where(kb == thr, jnp.where(rank >= need, thr - 1, kb), kb)
            return run + jnp.sum(eq, axis=0, keepdims=True)

        lax.fori_loop(0, n_blk, body, jnp.zeros((1, tq), F32))

    dqt = jnp.concatenate([dqt_ref[0, h] for h in range(DSA_HEADS)], axis=1)
    m_ref[...] = jnp.full(m_ref.shape, NEG, F32)
    acc_ref[...] = jnp.zeros(acc_ref.shape, F32)

    x_ref[...] = pair_scores(dk_ref, dqt, 0, 0, DSA_HEADS)

    def att_step(js, variants):
        n_pairs = (DSA_HEADS + 1) // 2
        items = [(j, g) for j in js for g in range(n_pairs)]
        n = len(items)

        def scores(i, _):
            if i == 0:
                return x_ref[...]
            if i == n:
                return pair_scores(dk_ref, dqt, 0, jnp.minimum(js[-1] + 1, n_blk - 1), DSA_HEADS)
            j, g = items[i]
            return pair_scores(dk_ref, dqt, g, j, DSA_HEADS)

        def running_max(i, s):
            if i == n:
                x_ref[...] = s
                return None
            j, g = items[i]
            variant = variants[i // n_pairs]
            sel = keys_ref[j] >= thr
            out = []
            for h in range(2 * g, min(2 * (g + 1), DSA_HEADS)):
                a = s[:, (h - 2 * g) * tq:(h - 2 * g + 1) * tq]
                if variant is not None:
                    a = a + tab_ref[variant, h]
                a = jnp.where(sel, a, NEG)
                m_prev = m_ref[h]
                m_new = jnp.maximum(m_prev, jnp.max(a, axis=0, keepdims=True))
                m_ref[h] = m_new
                out.append((h, a, m_prev, m_new))
            return out

        def weights_pv(i, carry):
            if i == n:
                return
            j, _ = items[i]
            vt = dvt_ref[0, :, pl.ds(pl.multiple_of(j * tk, tk), tk)]
            for h, a, m_prev, m_new in carry:
                p = jnp.exp2(a - m_new)
                acc_ref[h] = jnp.exp2(m_prev - m_new) * acc_ref[h] + _dot(vt, p.astype(BF16))

        _pipeline(n + 1, [scores, running_max, weights_pv])

    _for_blocks(jnp.maximum(n_blk - 2, 0), DSA_UNROLL, lambda js: att_step(js, [None] * len(js)))

    @pl.when(jnp.logical_and(n_blk >= 2, sub == 0))
    def _():
        att_step([n_blk - 2, n_blk - 1], [DSA_SUB, 0])

    @pl.when(jnp.logical_and(n_blk >= 2, sub != 0))
    def _():
        att_step([n_blk - 2, n_blk - 1], [None, sub])

    @pl.when(n_blk == 1)
    def _():
        att_step([0], [sub])
    _finish(lambda h: acc_ref[h], DSA_HEADS, o_ref)


def _dsa_call(iqt, ik, iwt, dqt, dk, dvt, tab, topk):
    B, _, _, S = iqt.shape
    tq = DSA_TQ
    qh = lambda n: pl.BlockSpec((1, n, HEAD, tq), lambda b, i: (b, 0, 0, i))
    res = lambda shape: pl.BlockSpec(shape, lambda b, i: (b, 0, 0), pipeline_mode=pl.Buffered(1))
    return pl.pallas_call(
        functools.partial(_dsa_kernel, topk=topk),
        grid=(B, S // tq),
        in_specs=[qh(IDX_HEADS), res((1, S, HEAD)),
                  pl.BlockSpec((1, IDX_HEADS, tq), lambda b, i: (b, 0, i)),
                  qh(DSA_HEADS), res((1, S, HEAD)), res((1, VROWS, S)), _const_spec(tab.shape)],
        out_specs=pl.BlockSpec((1, tq, OUT_W), lambda b, i: (b, i, 0)),
        out_shape=jax.ShapeDtypeStruct((B, S, OUT_W), BF16),
        scratch_shapes=[pltpu.VMEM((S // DSA_TK + (S // DSA_TK) % 2, DSA_TK, tq), jnp.int32),
                        pltpu.VMEM((S // DSA_TK + (S // DSA_TK) % 2, DSA_TK, tq), jnp.int32),
                        pltpu.VMEM((DSA_HEADS, 1, tq), F32),
                        pltpu.VMEM((DSA_HEADS, VROWS, tq), F32),
                        pltpu.VMEM((DSA_TK, 2 * tq), F32)],
        compiler_params=_params("parallel", "arbitrary"),
        name="dsa",
    )(iqt, ik, iwt, dqt, dk, dvt, tab)


def _post_kernel(h_ref, om_ref, od_ref, os_ref, p_ref, wo_ref, gf_ref, wg_ref, wu_ref, wd_ref,
                 gp_ref, wpg_ref, wpp_ref, gfin_ref, out_ref, *, final):
    h = (h_ref[0] + _dot(om_ref[0], wo_ref[0]) + _dot(od_ref[0], wo_ref[1])
         + _dot(os_ref[0], wo_ref[2]))

    hf = _rms(h, gf_ref[...]).astype(BF16)
    d_ff = wg_ref.shape[1]
    ffn = jnp.zeros(h.shape, F32)
    for c0 in range(0, d_ff, FF_CHUNK):
        g = _dot(hf, wg_ref[:, c0:c0 + FF_CHUNK])
        u = _dot(hf, wu_ref[:, c0:c0 + FF_CHUNK])
        ffn = ffn + _dot((g * jax.nn.sigmoid(g) * u).astype(BF16), wd_ref[c0:c0 + FF_CHUNK, :])
    h = h + ffn

    gate = jax.nn.sigmoid(_dot(_rms(h, gp_ref[...]).astype(BF16), wpg_ref[...]))
    h = h + gate * _dot(p_ref[0, 0].astype(BF16), wpp_ref[...])
    if final:
        h = _rms(h, gfin_ref[...])
    out_ref[0] = h


def _post_call(h, om, od, os_, p, layer, wo, gf, wg, wu, wd, gp, wpg, wpp, gfin, final):
    B, S, D = h.shape
    mix = pl.BlockSpec((1, TM, OUT_W), lambda b, t: (b, t, 0))
    tok = pl.BlockSpec((1, TM, D), lambda b, t: (b, t, 0))
    return pl.pallas_call(
        functools.partial(_post_kernel, final=final),
        grid=(B, S // TM),
        in_specs=[tok, mix, mix, mix,
                  pl.BlockSpec((1, 1, TM, p.shape[-1]), lambda b, t: (layer, b, t, 0)),
                  _const_spec(wo.shape), _const_spec(gf.shape), _const_spec(wg.shape),
                  _const_spec(wu.shape), _const_spec(wd.shape), _const_spec(gp.shape),
                  _const_spec(wpg.shape), _const_spec(wpp.shape), _const_spec(gfin.shape)],
        out_specs=tok,
        out_shape=jax.ShapeDtypeStruct((B, S, D), F32),
        compiler_params=_params("parallel", "parallel"),
        name="post",
    )(h, om, od, os_, p, wo, gf, wg, wu, wd, gp, wpg, wpp, gfin)


def _swap_halves(w):
    half = w.shape[-1] // 2
    return jnp.concatenate([w[..., half:], w[..., :half]], axis=-1)


def _pack_w_in(w_in):
    L, D, _ = w_in.shape
    z = lambda n: jnp.zeros((L, D, n), F32)
    o = 0
    cols = {}
    for name, n in (("cq", MLA_Q_LORA), ("ckv", MLA_KV_LORA), ("kr", MLA_ROPE),
                    ("dq", DSA_HEADS * DSA_DIM), ("dk", DSA_DIM), ("dv", DSA_DIM),
                    ("iq", IDX_HEADS * IDX_DIM), ("ik", IDX_DIM), ("iw", IDX_HEADS),
                    ("sq", SB_HEADS * SB_DIM), ("sk", SB_HEADS * SB_DIM), ("sv", SB_HEADS * SB_DIM)):
        cols[name] = w_in[:, :, o:o + n]
        o += n
    pad_r = LANES - MLA_NOPE - MLA_ROPE
    std = jnp.concatenate([
        cols["cq"], cols["ckv"],
        z(MLA_NOPE), cols["kr"], z(pad_r),
        z(MLA_NOPE), _swap_halves(cols["kr"]), z(pad_r),
        cols["sk"], cols["ik"],
        cols["dk"], z(LANES - DSA_DIM)], axis=-1)
    tr = jnp.concatenate([
        cols["dq"] * DSA_DIM ** -0.5, cols["iq"] * IDX_DIM ** -0.5, cols["sq"] * SB_DIM ** -0.5,
        cols["sv"], cols["dv"], cols["iw"], z(ROW_IW[1] - ROW_IW[0] - IDX_HEADS)], axis=-1)
    assert std.shape[-1] == W_STD_COLS and tr.shape[-1] == W_T_ROWS
    return std.astype(BF16), jnp.swapaxes(tr, 1, 2).astype(BF16)


def _pack_mla(w_uq, w_ukv):
    L = w_uq.shape[0]
    dq = MLA_NOPE + MLA_ROPE
    pad_r = LANES - dq
    uq = w_uq.reshape(L, MLA_Q_LORA, MLA_HEADS, dq)
    zq = lambda n: jnp.zeros((L, MLA_Q_LORA, MLA_HEADS, n), F32)
    wqa = jnp.concatenate([uq, zq(pad_r)], axis=-1)
    wqb = jnp.concatenate([zq(MLA_NOPE), _swap_halves(uq[..., MLA_NOPE:]), zq(pad_r)], axis=-1)
    ukv = w_ukv.reshape(L, MLA_KV_LORA, MLA_HEADS, MLA_NOPE + MLA_V)
    wk = jnp.concatenate([ukv[..., :MLA_NOPE],
                          jnp.zeros((L, MLA_KV_LORA, MLA_HEADS, LANES - MLA_NOPE), F32)], axis=-1)
    wv = ukv[..., MLA_NOPE:]
    flat = lambda a: a.reshape(L, a.shape[1], -1)
    tr = lambda a: jnp.swapaxes(flat(a), 1, 2).astype(BF16)
    return tr(wqa), tr(wqb), flat(wk).astype(BF16), tr(wv)


def _rope_tables(S):
    half = MLA_ROPE // 2
    inv = ROPE_THETA ** (-jnp.arange(half, dtype=F32) / half)
    ang = jnp.arange(S, dtype=jnp.int32).astype(F32)[:, None] * inv[None, :]
    cos, sin = jnp.cos(ang), jnp.sin(ang)
    pad_r = LANES - MLA_NOPE - MLA_ROPE
    cos_t = jnp.concatenate([jnp.ones((S, MLA_NOPE), F32), cos, cos, jnp.zeros((S, pad_r), F32)], axis=-1)
    sin_t = jnp.concatenate([jnp.zeros((S, MLA_NOPE), F32), -sin, sin, jnp.zeros((S, pad_r), F32)], axis=-1)
    return cos_t, sin_t


def _pack_w_o(w_o):
    L, _, D = w_o.shape
    a = MLA_HEADS * MLA_V
    b = a + DSA_HEADS * DSA_DIM
    pad = jnp.zeros((L, OUT_W - DSA_HEADS * DSA_DIM, D), F32)
    blocks = [w_o[:, :a], jnp.concatenate([w_o[:, a:b], pad], axis=1),
              jnp.concatenate([w_o[:, b:], pad], axis=1)]
    return jnp.stack(blocks, axis=1).astype(BF16)


def kernel(x, p, w_in, attn_norm, mla_q_norm, mla_w_uq, mla_kv_norm, mla_w_ukv, rel_bias, w_o,
           ffn_norm, w_gate, w_up, w_down, ple_norm, w_ple_gate, w_ple_proj, final_norm):
    B, S, D = x.shape
    depth = w_in.shape[0]
    assert S % TM == 0 and S % DSA_TK == 0 and S % MLA_TK == 0
    assert w_gate.shape[-1] % FF_CHUNK == 0
    topk = min(DSA_TOPK, S // 4)

    w_std, w_tr = _pack_w_in(w_in)
    wqa, wqb, wk, wv = _pack_mla(mla_w_uq, mla_w_ukv)
    cos_s, sin_s = _rope_tables(S)
    q_scale = (MLA_NOPE + MLA_ROPE) ** -0.5 * LOG2E
    cos_t, sin_t = cos_s.T * q_scale, sin_s.T * q_scale
    row = lambda g: g.reshape(depth, 1, -1)
    g_attn, g_q, g_kv, g_ffn, g_ple = map(row, (attn_norm, mla_q_norm, mla_kv_norm, ffn_norm, ple_norm))
    g_fin = final_norm.reshape(1, -1)
    wo = _pack_w_o(w_o)
    wg, wu, wd, wpg, wpp = (a.astype(BF16) for a in (w_gate, w_up, w_down, w_ple_gate, w_ple_proj))
    tri = jnp.triu(jnp.ones((SB_T, SB_T), F32), 0)
    tail = jnp.zeros((16, SB_T), F32).at[0].set(1.0)
    u = jnp.concatenate([tri, tail], axis=0).astype(BF16)

    tab = _bias_call(rel_bias)

    h = x
    for i in range(depth):
        (mqt, mk, mvt, dqt, dk, dvt, iqt, ik, iwt, sqt, sk, svt) = _proj_call(
            h, g_attn[i], w_std[i], w_tr[i], g_q[i], wqa[i], wqb[i], g_kv[i], wk[i], wv[i],
            cos_s, sin_s, cos_t, sin_t)
        o_mla = _mla_call(mqt, mk, mvt)
        o_dsa = _dsa_call(iqt, ik, iwt, dqt, dk, dvt, tab, topk)
        o_sb = _sb_call(sqt, sk, svt, u)
        h = _post_call(h, o_mla, o_dsa, o_sb, p, i, wo[i], g_ffn[i], wg[i], wu[i], wd[i],
                       g_ple[i], wpg[i], wpp[i], g_fin, final=(i == depth - 1))
    return h
```

```python
import functools
import math

import jax
import jax.numpy as jnp
from jax import lax
from jax.experimental import pallas as pl
from jax.experimental.pallas import tpu as pltpu

EPS = 1e-6
MLA_HEADS = 6
MLA_Q_LORA = 384
MLA_KV_LORA = 256
MLA_NOPE = 64
MLA_ROPE = 32
MLA_V = 64
ROPE_THETA = 10000.0
DSA_HEADS = 5
DSA_DIM = 64
IDX_HEADS = 8
IDX_DIM = 64
DSA_TOPK = 256
SB_HEADS = 5
SB_DIM = 64
REL_BUCKETS = 32
REL_MAX_DIST = 128

LANES = 128
HEAD = 64
VROWS = 80
OUT_W = 384
VMEM_LIMIT = 56 * 1024 * 1024

TM = 512
MLA_TQ, MLA_TK = 256, 512
MLA_UNROLL = 4
SB_T = 256
SB_UNROLL = 8
DSA_TQ, DSA_TK = 128, 512
DSA_SUB = DSA_TK // DSA_TQ
DSA_UNROLL = 4
COUNT_ACC = 8
DESCENT_CHECKS = (24, 26, 28)
SLAB = 8
SORT_GROUP = 7
SORT_NETWORK = ((0, 6), (2, 3), (4, 5), (0, 2), (1, 4), (3, 6), (0, 1), (2, 5), (3, 4),
                (1, 2), (4, 6), (2, 3), (4, 5), (1, 2), (3, 4), (5, 6))
FF_CHUNK = 1408

NEG = -1e30
INT_MIN = -2 ** 31
NEG_INF_KEY = -2139095041
LOG2E = math.log2(math.e)

F32 = jnp.float32
BF16 = jnp.bfloat16

SEG_CQ = (0, 384)
SEG_CKV = (384, 640)
SEG_KRA = (640, 768)
SEG_KRB = (768, 896)
SEG_SK_IK = (896, 1280)
SEG_DK = (1280, 1408)
W_STD_COLS = 1408
ROW_DQ = (0, 320)
ROW_IQ = (320, 832)
ROW_SQ = (832, 1152)
ROW_SV = (1152, 1472)
ROW_DV = (1472, 1536)
ROW_IW = (1536, 1552)
W_T_ROWS = 1552


def _rms(x, g):
    return x * lax.rsqrt(jnp.mean(x * x, axis=-1, keepdims=True) + EPS) * g


def _dot(a, b):
    return jnp.dot(a, b, preferred_element_type=F32)


def _const_spec(shape):
    n = len(shape)
    return pl.BlockSpec(shape, lambda *_: (0,) * n, pipeline_mode=pl.Buffered(1))


def _params(*sem):
    return pltpu.CompilerParams(dimension_semantics=sem, vmem_limit_bytes=VMEM_LIMIT)


def _pipeline(n, stages):
    held = [dict() for _ in stages]
    for t in range(n + len(stages) - 1):
        for s, stage in enumerate(stages):
            i = t - s
            if 0 <= i < n:
                held[s][i] = stage(i, held[s - 1].pop(i) if s else None)


def _for_blocks(n, unroll, step):
    def body(i, c):
        step([unroll * i + d for d in range(unroll)])
        return c

    lax.fori_loop(0, n // unroll, body, 0)
    rem = n % unroll
    group = unroll // 2
    while group:
        @pl.when(rem & group != 0)
        def _(group=group):
            base = n - (rem & (2 * group - 1))
            step([base + d for d in range(group)])
        group //= 2


def _ones_rows(n):
    r = lax.broadcasted_iota(jnp.int32, (VROWS - HEAD, n), 0)
    return jnp.where(r == 0, 1.0, 0.0).astype(BF16)


def _proj_kernel(h_ref, g_ref, w_ref, wt_ref, gq_ref, wqa_ref, wqb_ref, gkv_ref, wk_ref, wv_ref,
                 cos_ref, sin_ref, cost_ref, sint_ref,
                 mqt_ref, mk_ref, mvt_ref, dqt_ref, dk_ref, dvt_ref, iqt_ref, ik_ref, iwt_ref,
                 sqt_ref, sk_ref, svt_ref):
    hn = _rms(h_ref[0], g_ref[...])
    hnt = hn.T.astype(BF16)
    hn = hn.astype(BF16)
    tm = hn.shape[0]
    ones = _ones_rows(tm)

    def seg(ab):
        return _dot(hn, w_ref[:, ab[0]:ab[1]])

    def seg_t(ab):
        return _dot(wt_ref[ab[0]:ab[1], :], hnt)

    cqn_t = _rms(seg(SEG_CQ), gq_ref[...]).T.astype(BF16)
    qa = _dot(wqa_ref[...], cqn_t)
    qb = _dot(wqb_ref[...], cqn_t)
    cos_t = cost_ref[...]
    sin_t = sint_ref[...]
    for h in range(MLA_HEADS):
        sl = slice(h * LANES, (h + 1) * LANES)
        mqt_ref[0, h] = (qa[sl] * cos_t + qb[sl] * sin_t).astype(BF16)

    ckvn = _rms(seg(SEG_CKV), gkv_ref[...])
    ckvn_t = ckvn.T.astype(BF16)
    ckvn = ckvn.astype(BF16)
    kn = _dot(ckvn, wk_ref[...])
    kr = seg(SEG_KRA) * cos_ref[...] + seg(SEG_KRB) * sin_ref[...]
    for h in range(MLA_HEADS):
        mk_ref[0, h] = (kn[:, h * LANES:(h + 1) * LANES] + kr).astype(BF16)
    vt = _dot(wv_ref[...], ckvn_t)
    for h in range(MLA_HEADS):
        mvt_ref[0, h, 0:HEAD, :] = vt[h * HEAD:(h + 1) * HEAD].astype(BF16)
        mvt_ref[0, h, HEAD:VROWS, :] = ones

    a = seg(SEG_SK_IK)
    for h in range(SB_HEADS):
        sk_ref[0, h] = a[:, h * HEAD:(h + 1) * HEAD].astype(BF16)
    ik_ref[0] = a[:, 320:384].astype(BF16)
    dk_ref[0] = seg(SEG_DK)[:, 0:HEAD].astype(BF16)

    a = seg_t(ROW_DQ) * LOG2E
    for h in range(DSA_HEADS):
        dqt_ref[0, h] = a[h * HEAD:(h + 1) * HEAD].astype(BF16)
    a = seg_t(ROW_IQ)
    for h in range(IDX_HEADS):
        iqt_ref[0, h] = a[h * HEAD:(h + 1) * HEAD].astype(BF16)
    a = seg_t(ROW_SQ) * LOG2E
    for h in range(SB_HEADS):
        sqt_ref[0, h] = a[h * HEAD:(h + 1) * HEAD].astype(BF16)
    a = seg_t(ROW_SV)
    for h in range(SB_HEADS):
        svt_ref[0, h] = a[h * HEAD:(h + 1) * HEAD].astype(BF16)
    dvt_ref[0, 0:HEAD, :] = seg_t(ROW_DV).astype(BF16)
    dvt_ref[0, HEAD:VROWS, :] = ones
    iwt_ref[0] = seg_t(ROW_IW)[0:IDX_HEADS] * IDX_HEADS ** -0.5


def _proj_call(h, g, w, wt, gq, wqa, wqb, gkv, wk, wv, cos_s, sin_s, cos_t, sin_t):
    B, S, D = h.shape
    nt = S // TM
    tok = lambda w_: pl.BlockSpec((1, TM, w_), lambda b, t: (b, t, 0))
    heads = lambda n, w_: pl.BlockSpec((1, n, TM, w_), lambda b, t: (b, 0, t, 0))
    heads_t = lambda n, r: pl.BlockSpec((1, n, r, TM), lambda b, t: (b, 0, 0, t))
    rows_t = lambda r: pl.BlockSpec((1, r, TM), lambda b, t: (b, 0, t))
    hs = lambda n, w_: jax.ShapeDtypeStruct((B, n, S, w_), BF16)
    hts = lambda n, r: jax.ShapeDtypeStruct((B, n, r, S), BF16)
    return pl.pallas_call(
        _proj_kernel,
        grid=(B, nt),
        in_specs=[tok(D), _const_spec(g.shape), _const_spec(w.shape), _const_spec(wt.shape),
                  _const_spec(gq.shape), _const_spec(wqa.shape), _const_spec(wqb.shape),
                  _const_spec(gkv.shape), _const_spec(wk.shape), _const_spec(wv.shape),
                  pl.BlockSpec((TM, LANES), lambda b, t: (t, 0)),
                  pl.BlockSpec((TM, LANES), lambda b, t: (t, 0)),
                  pl.BlockSpec((LANES, TM), lambda b, t: (0, t)),
                  pl.BlockSpec((LANES, TM), lambda b, t: (0, t))],
        out_specs=[heads_t(MLA_HEADS, LANES), heads(MLA_HEADS, LANES), heads_t(MLA_HEADS, VROWS),
                   heads_t(DSA_HEADS, HEAD), tok(HEAD), rows_t(VROWS),
                   heads_t(IDX_HEADS, HEAD), tok(HEAD), rows_t(IDX_HEADS),
                   heads_t(SB_HEADS, HEAD), heads(SB_HEADS, HEAD), heads_t(SB_HEADS, HEAD)],
        out_shape=[hts(MLA_HEADS, LANES), hs(MLA_HEADS, LANES), hts(MLA_HEADS, VROWS),
                   hts(DSA_HEADS, HEAD), jax.ShapeDtypeStruct((B, S, HEAD), BF16),
                   jax.ShapeDtypeStruct((B, VROWS, S), BF16),
                   hts(IDX_HEADS, HEAD), jax.ShapeDtypeStruct((B, S, HEAD), BF16),
                   jax.ShapeDtypeStruct((B, IDX_HEADS, S), F32),
                   hts(SB_HEADS, HEAD), hs(SB_HEADS, HEAD), hts(SB_HEADS, HEAD)],
        compiler_params=_params("parallel", "parallel"),
        name="proj",
    )(h, g, w, wt, gq, wqa, wqb, gkv, wk, wv, cos_s, sin_s, cos_t, sin_t)


def _finish(head_acc, n_heads, o_ref):
    accs = [head_acc(h) for h in range(n_heads)]
    outs = [a[0:HEAD, :] / a[HEAD:HEAD + 1, :] for a in accs]
    pad = OUT_W - n_heads * HEAD
    if pad:
        outs.append(jnp.zeros((pad, outs[0].shape[1]), F32))
    o_ref[0] = jnp.concatenate(outs, axis=0).T.astype(BF16)


def _mla_kernel(qt_ref, k_ref, vt_ref, o_ref, m_ref, acc_ref, s_ref):
    qi = pl.program_id(1)
    tq, tk = MLA_TQ, MLA_TK
    m_ref[...] = jnp.full(m_ref.shape, NEG, F32)
    acc_ref[...] = jnp.zeros(acc_ref.shape, F32)
    n_full = (qi * tq) // tk
    n_all = ((qi + 1) * tq + tk - 1) // tk

    def qk(h, j):
        k0 = pl.multiple_of(j * tk, tk)
        half = tk // 4
        return jnp.concatenate(
            [_dot(k_ref[0, h, pl.ds(k0 + i * half, half), :], qt_ref[0, h]) for i in range(4)],
            axis=0)

    s_ref[...] = qk(0, 0)

    def step(js, masked):
        items = [(j, h) for j in js for h in range(MLA_HEADS)]
        n = len(items)

        def scores(i, _):
            if i == 0:
                return s_ref[...]
            if i == n:
                return qk(0, jnp.minimum(js[-1] + 1, n_all - 1))
            j, h = items[i]
            return qk(h, j)

        def running_max(i, s):
            if i == n:
                s_ref[...] = s
                return None
            j, h = items[i]
            if masked[i // MLA_HEADS]:
                kpos = j * tk + lax.broadcasted_iota(jnp.int32, (tk, tq), 0)
                qpos = qi * tq + lax.broadcasted_iota(jnp.int32, (tk, tq), 1)
                s = jnp.where(kpos <= qpos, s, NEG)
            m_prev = m_ref[h]
            m_new = jnp.maximum(m_prev, jnp.max(s, axis=0, keepdims=True))
            m_ref[h] = m_new
            return s, m_prev, m_new

        def weights_pv(i, carry):
            if i == n:
                return
            j, h = items[i]
            s, m_prev, m_new = carry
            p = jnp.exp2(s - m_new)
            k0 = pl.multiple_of(j * tk, tk)
            acc_ref[h] = (jnp.exp2(m_prev - m_new) * acc_ref[h]
                          + _dot(vt_ref[0, h, :, pl.ds(k0, tk)], p.astype(BF16)))

        _pipeline(n + 1, [scores, running_max, weights_pv])

    _for_blocks(jnp.maximum(n_full - 1, 0), MLA_UNROLL, lambda js: step(js, [False] * len(js)))

    @pl.when(n_full >= 1)
    def _():
        step([n_full - 1, n_full], [False, True])

    @pl.when(n_full == 0)
    def _():
        step([0], [True])

    _finish(lambda h: acc_ref[h], MLA_HEADS, o_ref)


def _mla_call(qt, k, vt):
    B, H, _, S = qt.shape
    res = lambda shape: pl.BlockSpec(shape, lambda b, i: (b, 0, 0, 0), pipeline_mode=pl.Buffered(1))
    return pl.pallas_call(
        _mla_kernel,
        grid=(B, S // MLA_TQ),
        in_specs=[pl.BlockSpec((1, H, LANES, MLA_TQ), lambda b, i: (b, 0, 0, i)),
                  res((1, H, S, LANES)), res((1, H, VROWS, S))],
        out_specs=pl.BlockSpec((1, MLA_TQ, OUT_W), lambda b, i: (b, i, 0)),
        out_shape=jax.ShapeDtypeStruct((B, S, OUT_W), BF16),
        scratch_shapes=[pltpu.VMEM((H, 1, MLA_TQ), F32), pltpu.VMEM((H, VROWS, MLA_TQ), F32),
                        pltpu.VMEM((MLA_TK, MLA_TQ), F32)],
        compiler_params=_params("parallel", "arbitrary"),
        name="mla",
    )(qt, k, vt)


def _sb_kernel(qt_ref, k_ref, vt_ref, u_ref, o_ref, c_ref, acc_ref, z_ref):
    qi = pl.program_id(1)
    t = SB_T
    c_ref[...] = jnp.zeros(c_ref.shape, F32)
    acc_ref[...] = jnp.zeros(acc_ref.shape, F32)

    def qk(h, j):
        k0 = pl.multiple_of(j * t, t)
        half = t // 2
        return jnp.concatenate(
            [_dot(k_ref[0, h, pl.ds(k0 + s * half, half), :], qt_ref[0, h]) for s in range(2)],
            axis=0)

    strict = (lax.broadcasted_iota(jnp.int32, (t, t), 0) < lax.broadcasted_iota(jnp.int32, (t, t), 1))

    def suffix_sums(z2, masked):
        neg_abs = pltpu.bitcast(pltpu.bitcast(z2, jnp.int32) | INT_MIN, F32)
        nl = jnp.maximum(z2, 0.0) + jnp.log2(1.0 + jnp.exp2(neg_abs))
        if masked:
            nl = jnp.where(strict, nl, 0.0)
        nlb = nl.astype(BF16)
        sums = jnp.concatenate([_dot(u_ref[0:t // 2, :], nlb), _dot(u_ref[t // 2:, :], nlb)], axis=0)
        return z2, sums

    z_ref[...] = qk(0, qi)

    def step(js, masked):
        items = [(j, h) for j in js for h in range(SB_HEADS)]
        n = len(items)

        def logits(i, _):
            if i == 0:
                return z_ref[...]
            if i == n:
                return qk(0, jnp.maximum(js[-1] - 1, 0))
            j, h = items[i]
            return qk(h, j)

        def sums_stage(i, z):
            if i == n:
                z_ref[...] = z
                return None
            return suffix_sums(z, masked[i // SB_HEADS])

        def weights_pv(i, carry):
            if i == n:
                return
            j, h = items[i]
            z2, sums = carry
            w = jnp.exp2(z2 - sums[0:t])
            if masked[i // SB_HEADS]:
                w = jnp.where(strict, w, 0.0)
            k0 = pl.multiple_of(j * t, t)
            acc_ref[h] += jnp.exp2(-c_ref[h]) * _dot(vt_ref[0, h, :, pl.ds(k0, t)], w.astype(BF16))
            c_ref[h] += sums[t:t + 1]

        _pipeline(n + 1, [logits, sums_stage, weights_pv])

    @pl.when(qi >= 1)
    def _():
        step([qi, qi - 1], [True, False])

    @pl.when(qi == 0)
    def _():
        step([0], [True])

    _for_blocks(jnp.maximum(qi - 1, 0), SB_UNROLL,
                lambda idx: step([qi - 2 - i for i in idx], [False] * len(idx)))

    outs = [acc_ref[h] for h in range(SB_HEADS)]
    outs.append(jnp.zeros((OUT_W - SB_HEADS * HEAD, t), F32))
    o_ref[0] = jnp.concatenate(outs, axis=0).T.astype(BF16)


def _sb_call(qt, k, vt, u):
    B, H, _, S = qt.shape
    res = lambda shape: pl.BlockSpec(shape, lambda b, i: (b, 0, 0, 0), pipeline_mode=pl.Buffered(1))
    return pl.pallas_call(
        _sb_kernel,
        grid=(B, S // SB_T),
        in_specs=[pl.BlockSpec((1, H, HEAD, SB_T), lambda b, i: (b, 0, 0, i)),
                  res((1, H, S, HEAD)), res((1, H, HEAD, S)), _const_spec(u.shape)],
        out_specs=pl.BlockSpec((1, SB_T, OUT_W), lambda b, i: (b, i, 0)),
        out_shape=jax.ShapeDtypeStruct((B, S, OUT_W), BF16),
        scratch_shapes=[pltpu.VMEM((H, 1, SB_T), F32), pltpu.VMEM((H, HEAD, SB_T), F32),
                        pltpu.VMEM((SB_T, SB_T), F32)],
        compiler_params=_params("parallel", "arbitrary"),
        name="stickbreak",
    )(qt, k, vt, u)


def _bias_kernel(rb_ref, o_ref):
    v = pl.program_id(0)
    shape = (DSA_TK, DSA_TQ)
    off = jnp.where(v < DSA_SUB, v * DSA_TQ, DSA_TK)
    dist = off + lax.broadcasted_iota(jnp.int32, shape, 1) - lax.broadcasted_iota(jnp.int32, shape, 0)
    max_exact = REL_BUCKETS // 2
    d = jnp.maximum(dist, 1).astype(F32)
    large = max_exact + (jnp.log(d / max_exact) / math.log(REL_MAX_DIST / max_exact)
                         * (REL_BUCKETS - max_exact)).astype(jnp.int32)
    large = jnp.minimum(large, REL_BUCKETS - 1)
    bucket = jnp.where(dist < max_exact, dist, large)
    for h in range(DSA_HEADS):
        t = jnp.zeros(shape, F32)
        for b in range(REL_BUCKETS):
            t = jnp.where(bucket == b, (rb_ref[b, h] - rb_ref[REL_BUCKETS - 1, h]) * LOG2E, t)
        o_ref[0, h] = jnp.where(dist >= 0, t, NEG)


def _bias_call(rel_bias):
    nv = DSA_SUB + 1
    return pl.pallas_call(
        _bias_kernel,
        grid=(nv,),
        in_specs=[pl.BlockSpec(memory_space=pltpu.SMEM)],
        out_specs=pl.BlockSpec((1, DSA_HEADS, DSA_TK, DSA_TQ), lambda v: (v, 0, 0, 0)),
        out_shape=jax.ShapeDtypeStruct((nv, DSA_HEADS, DSA_TK, DSA_TQ), F32),
        compiler_params=_params("arbitrary"),
        name="bias_tiles",
    )(rel_bias)


def _dsa_kernel(iqt_ref, ik_ref, iwt_ref, dqt_ref, dk_ref, dvt_ref, tab_ref, o_ref,
                keys_ref, sorted_ref, m_ref, acc_ref, x_ref, *, topk):
    qi = pl.program_id(1)
    n_blk = qi // DSA_SUB + 1
    sub = qi % DSA_SUB
    tq, tk = DSA_TQ, DSA_TK

    iqt = jnp.concatenate([iqt_ref[0, h] for h in range(IDX_HEADS)], axis=1)
    iw = iwt_ref[0]

    def pair_scores(k_ref_, qt_all, g, j, n_heads):
        k0 = pl.multiple_of(j * tk, tk)
        cols = qt_all[:, 2 * g * tq:min(2 * (g + 1), n_heads) * tq]
        half = tk // 2
        return jnp.concatenate(
            [_dot(k_ref_[0, pl.ds(k0 + i * half, half), :], cols) for i in range(2)], axis=0)

    n_slab = tk // SLAB
    n_group = n_slab // SORT_GROUP
    zero_shift = 23 - (keys_ref.shape[0] * tk - 1).bit_length()
    assert zero_shift >= 0
    zero_key = -1 - lax.shift_left(lax.broadcasted_iota(jnp.int32, (tk, tq), 0), zero_shift)

    def sort_groups(key):
        slabs = [key[SLAB * i:SLAB * (i + 1)] for i in range(n_slab)]
        for g in range(n_group):
            v = slabs[SORT_GROUP * g:SORT_GROUP * (g + 1)]
            for a, b in SORT_NETWORK:
                v[a], v[b] = jnp.maximum(v[a], v[b]), jnp.minimum(v[a], v[b])
            slabs[SORT_GROUP * g:SORT_GROUP * (g + 1)] = v
        return jnp.concatenate(slabs, axis=0)

    x_ref[...] = pair_scores(ik_ref, iqt, 0, 0, IDX_HEADS)

    def idx_step(js, masked):
        n_pairs = IDX_HEADS // 2
        items = [(j, g) for j in js for g in range(n_pairs)]
        n = len(items)
        total = {}

        def scores(i, _):
            if i == 0:
                return x_ref[...]
            if i == n:
                return pair_scores(ik_ref, iqt, 0, jnp.minimum(js[-1] + 1, n_blk - 1), IDX_HEADS)
            j, g = items[i]
            return pair_scores(ik_ref, iqt, g, j, IDX_HEADS)

        def weigh(i, x):
            if i == n:
                x_ref[...] = x
                return
            j, g = items[i]
            part = (jnp.maximum(x[:, 0:tq], 0.0) * iw[2 * g:2 * g + 1]
                    + jnp.maximum(x[:, tq:2 * tq], 0.0) * iw[2 * g + 1:2 * g + 2])
            blk = i // n_pairs
            total[blk] = part if g == 0 else total[blk] + part
            if g < n_pairs - 1:
                return
            bits = pltpu.bitcast(total.pop(blk), jnp.int32)
            key = bits ^ (lax.shift_right_arithmetic(bits, 31) & 0x7FFFFFFF)
            key = jnp.where((bits & 0x7F800000) == 0, zero_key - lax.shift_left(j * tk, zero_shift), key)
            if masked:
                kpos = j * tk + lax.broadcasted_iota(jnp.int32, key.shape, 0)
                qpos = qi * tq + lax.broadcasted_iota(jnp.int32, key.shape, 1)
                key = jnp.where(kpos <= qpos, key, NEG_INF_KEY)
            keys_ref[j] = key
            sorted_ref[j] = sort_groups(key)

        _pipeline(n + 1, [scores, weigh])

    _for_blocks(n_blk - 1, DSA_UNROLL, lambda js: idx_step(js, False))
    idx_step([n_blk - 1], True)

    def over_blocks(per_block, init):
        def pair(i, acc):
            return per_block(2 * i + 1, per_block(2 * i, acc))

        def last(_, acc):
            return per_block(n_blk - 1, acc)

        return lax.fori_loop(0, n_blk % 2, last, lax.fori_loop(0, n_blk // 2, pair, init))

    def count_ge(cand):
        def per_block(j, acc):
            c = jnp.where(keys_ref[j] >= cand, 1.0, 0.0)
            return acc + jnp.sum(c.reshape(COUNT_ACC, tk // COUNT_ACC, tq), axis=0)
        acc = over_blocks(per_block, jnp.zeros((tk // COUNT_ACC, tq), F32))
        return jnp.sum(acc, axis=0, keepdims=True)

    def count_ge_sorted(cand):
        def one(flag):
            return jnp.where(flag, 1.0, 0.0)

        def per_block(j, accs):
            accs = list(accs)
            blk = sorted_ref[j]
            for g in range(n_group):
                s = [blk[SLAB * (SORT_GROUP * g + k):SLAB * (SORT_GROUP * g + k + 1)]
                     for k in range(SORT_GROUP)]
                b4 = s[3] >= cand
                b2 = jnp.where(b4, s[5], s[1]) >= cand
                b1 = jnp.where(b4, jnp.where(b2, s[6], s[4]), jnp.where(b2, s[2], s[0])) >= cand
                o = 3 * (g % 2)
                accs[o] += one(b4)
                accs[o + 1] += one(b2)
                accs[o + 2] += one(b1)
            for r in range(SORT_GROUP * n_group, n_slab):
                accs[2] += one(blk[SLAB * r:SLAB * (r + 1)] >= cand)
            return tuple(accs)

        z = jnp.zeros((SLAB, tq), F32)
        a = over_blocks(per_block, (z,) * 6)
        total = 4.0 * (a[0] + a[3]) + 2.0 * (a[1] + a[4]) + (a[2] + a[5])
        return jnp.sum(total, axis=0, keepdims=True)

    def bit_body(i, state):
        t_u, done, thr_hit = state
        cand_u = t_u | lax.shift_left(jnp.int32(1), 31 - i)
        cand = cand_u ^ INT_MIN
        cnt = count_ge_sorted(cand)
        hit = cnt == topk
        thr_hit = jnp.where(hit, jnp.where(done > 0.0, thr_hit, cand), thr_hit)
        done = jnp.where(hit, 1.0, done)
        return jnp.where(cnt >= topk, cand_u, t_u), done, thr_hit

    zero = jnp.zeros((1, tq), jnp.int32)
    state = (zero, jnp.zeros((1, tq), F32), zero)
    first = 0
    for last in DESCENT_CHECKS + (32,):
        stop = last if first == 0 else jnp.where(jnp.min(state[1]) < 1.0, last, first)
        state = lax.fori_loop(first, stop, bit_body, state)
        first = last
    t_u, done, thr_hit = state
    thr = jnp.where(done > 0.0, thr_hit, t_u ^ INT_MIN)

    tie = jnp.where(done > 0.0, 0.0, jnp.where(thr > NEG_INF_KEY, 1.0, 0.0))

    @pl.when(jnp.max(tie) > 0.0)
    def _():
        need = jnp.where(tie > 0.0, topk - count_ge(thr + 1), float(tk * keys_ref.shape[0] + 1))
        row = lax.broadcasted_iota(jnp.int32, (tk, tk), 0)
        col = lax.broadcasted_iota(jnp.int32, (tk, tk), 1)
        before = jnp.where(col < row, 1.0, 0.0).astype(BF16)

        def body(j, run):
            kb = keys_ref[j]
            eq = jnp.where(kb == thr, 1.0, 0.0)
            rank = _dot(before, eq.astype(BF16)) + run
            keys_ref[j] = jnp.where(kb == thr, jnp.where(rank >= need, thr - 1, kb), kb)
            return run + jnp.sum(eq, axis=0, keepdims=True)

        lax.fori_loop(0, n_blk, body, jnp.zeros((1, tq), F32))

    dqt = jnp.concatenate([dqt_ref[0, h] for h in range(DSA_HEADS)], axis=1)
    m_ref[...] = jnp.full(m_ref.shape, NEG, F32)
    acc_ref[...] = jnp.zeros(acc_ref.shape, F32)

    x_ref[...] = pair_scores(dk_ref, dqt, 0, 0, DSA_HEADS)

    def att_step(js, variants):
        n_pairs = (DSA_HEADS + 1) // 2
        items = [(j, g) for j in js for g in range(n_pairs)]
        n = len(items)

        def scores(i, _):
            if i == 0:
                return x_ref[...]
            if i == n:
                return pair_scores(dk_ref, dqt, 0, jnp.minimum(js[-1] + 1, n_blk - 1), DSA_HEADS)
            j, g = items[i]
            return pair_scores(dk_ref, dqt, g, j, DSA_HEADS)

        def running_max(i, s):
            if i == n:
                x_ref[...] = s
                return None
            j, g = items[i]
            variant = variants[i // n_pairs]
            sel = keys_ref[j] >= thr
            out = []
            for h in range(2 * g, min(2 * (g + 1), DSA_HEADS)):
                a = s[:, (h - 2 * g) * tq:(h - 2 * g + 1) * tq]
                if variant is not None:
                    a = a + tab_ref[variant, h]
                a = jnp.where(sel, a, NEG)
                m_prev = m_ref[h]
                m_new = jnp.maximum(m_prev, jnp.max(a, axis=0, keepdims=True))
                m_ref[h] = m_new
                out.append((h, a, m_prev, m_new))
            return out

        def weights_pv(i, carry):
            if i == n:
                return
            j, _ = items[i]
            vt = dvt_ref[0, :, pl.ds(pl.multiple_of(j * tk, tk), tk)]
            for h, a, m_prev, m_new in carry:
                p = jnp.exp2(a - m_new)
                acc_ref[h] = jnp.exp2(m_prev - m_new) * acc_ref[h] + _dot(vt, p.astype(BF16))

        _pipeline(n + 1, [scores, running_max, weights_pv])

    _for_blocks(jnp.maximum(n_blk - 2, 0), DSA_UNROLL, lambda js: att_step(js, [None] * len(js)))

    @pl.when(jnp.logical_and(n_blk >= 2, sub == 0))
    def _():
        att_step([n_blk - 2, n_blk - 1], [DSA_SUB, 0])

    @pl.when(jnp.logical_and(n_blk >= 2, sub != 0))
    def _():
        att_step([n_blk - 2, n_blk - 1], [None, sub])

    @pl.when(n_blk == 1)
    def _():
        att_step([0], [sub])
    _finish(lambda h: acc_ref[h], DSA_HEADS, o_ref)


def _dsa_call(iqt, ik, iwt, dqt, dk, dvt, tab, topk):
    B, _, _, S = iqt.shape
    tq = DSA_TQ
    qh = lambda n: pl.BlockSpec((1, n, HEAD, tq), lambda b, i: (b, 0, 0, i))
    res = lambda shape: pl.BlockSpec(shape, lambda b, i: (b, 0, 0), pipeline_mode=pl.Buffered(1))
    return pl.pallas_call(
        functools.partial(_dsa_kernel, topk=topk),
        grid=(B, S // tq),
        in_specs=[qh(IDX_HEADS), res((1, S, HEAD)),
                  pl.BlockSpec((1, IDX_HEADS, tq), lambda b, i: (b, 0, i)),
                  qh(DSA_HEADS), res((1, S, HEAD)), res((1, VROWS, S)), _const_spec(tab.shape)],
        out_specs=pl.BlockSpec((1, tq, OUT_W), lambda b, i: (b, i, 0)),
        out_shape=jax.ShapeDtypeStruct((B, S, OUT_W), BF16),
        scratch_shapes=[pltpu.VMEM((S // DSA_TK, DSA_TK, tq), jnp.int32),
                        pltpu.VMEM((S // DSA_TK, DSA_TK, tq), jnp.int32),
                        pltpu.VMEM((DSA_HEADS, 1, tq), F32),
                        pltpu.VMEM((DSA_HEADS, VROWS, tq), F32),
                        pltpu.VMEM((DSA_TK, 2 * tq), F32)],
        compiler_params=_params("parallel", "arbitrary"),
        name="dsa",
    )(iqt, ik, iwt, dqt, dk, dvt, tab)


def _post_kernel(h_ref, om_ref, od_ref, os_ref, p_ref, wo_ref, gf_ref, wg_ref, wu_ref, wd_ref,
                 gp_ref, wpg_ref, wpp_ref, gfin_ref, out_ref, *, final):
    h = (h_ref[0] + _dot(om_ref[0], wo_ref[0]) + _dot(od_ref[0], wo_ref[1])
         + _dot(os_ref[0], wo_ref[2]))

    hf = _rms(h, gf_ref[...]).astype(BF16)
    d_ff = wg_ref.shape[1]
    ffn = jnp.zeros(h.shape, F32)
    for c0 in range(0, d_ff, FF_CHUNK):
        g = _dot(hf, wg_ref[:, c0:c0 + FF_CHUNK])
        u = _dot(hf, wu_ref[:, c0:c0 + FF_CHUNK])
        ffn = ffn + _dot((g * jax.nn.sigmoid(g) * u).astype(BF16), wd_ref[c0:c0 + FF_CHUNK, :])
    h = h + ffn

    gate = jax.nn.sigmoid(_dot(_rms(h, gp_ref[...]).astype(BF16), wpg_ref[...]))
    h = h + gate * _dot(p_ref[0, 0].astype(BF16), wpp_ref[...])
    if final:
        h = _rms(h, gfin_ref[...])
    out_ref[0] = h


def _post_call(h, om, od, os_, p, layer, wo, gf, wg, wu, wd, gp, wpg, wpp, gfin, final):
    B, S, D = h.shape
    mix = pl.BlockSpec((1, TM, OUT_W), lambda b, t: (b, t, 0))
    tok = pl.BlockSpec((1, TM, D), lambda b, t: (b, t, 0))
    return pl.pallas_call(
        functools.partial(_post_kernel, final=final),
        grid=(B, S // TM),
        in_specs=[tok, mix, mix, mix,
                  pl.BlockSpec((1, 1, TM, p.shape[-1]), lambda b, t: (layer, b, t, 0)),
                  _const_spec(wo.shape), _const_spec(gf.shape), _const_spec(wg.shape),
                  _const_spec(wu.shape), _const_spec(wd.shape), _const_spec(gp.shape),
                  _const_spec(wpg.shape), _const_spec(wpp.shape), _const_spec(gfin.shape)],
        out_specs=tok,
        out_shape=jax.ShapeDtypeStruct((B, S, D), F32),
        compiler_params=_params("parallel", "parallel"),
        name="post",
    )(h, om, od, os_, p, wo, gf, wg, wu, wd, gp, wpg, wpp, gfin)


def _swap_halves(w):
    half = w.shape[-1] // 2
    return jnp.concatenate([w[..., half:], w[..., :half]], axis=-1)


def _pack_w_in(w_in):
    L, D, _ = w_in.shape
    z = lambda n: jnp.zeros((L, D, n), F32)
    o = 0
    cols = {}
    for name, n in (("cq", MLA_Q_LORA), ("ckv", MLA_KV_LORA), ("kr", MLA_ROPE),
                    ("dq", DSA_HEADS * DSA_DIM), ("dk", DSA_DIM), ("dv", DSA_DIM),
                    ("iq", IDX_HEADS * IDX_DIM), ("ik", IDX_DIM), ("iw", IDX_HEADS),
                    ("sq", SB_HEADS * SB_DIM), ("sk", SB_HEADS * SB_DIM), ("sv", SB_HEADS * SB_DIM)):
        cols[name] = w_in[:, :, o:o + n]
        o += n
    pad_r = LANES - MLA_NOPE - MLA_ROPE
    std = jnp.concatenate([
        cols["cq"], cols["ckv"],
        z(MLA_NOPE), cols["kr"], z(pad_r),
        z(MLA_NOPE), _swap_halves(cols["kr"]), z(pad_r),
        cols["sk"], cols["ik"],
        cols["dk"], z(LANES - DSA_DIM)], axis=-1)
    tr = jnp.concatenate([
        cols["dq"] * DSA_DIM ** -0.5, cols["iq"] * IDX_DIM ** -0.5, cols["sq"] * SB_DIM ** -0.5,
        cols["sv"], cols["dv"], cols["iw"], z(ROW_IW[1] - ROW_IW[0] - IDX_HEADS)], axis=-1)
    assert std.shape[-1] == W_STD_COLS and tr.shape[-1] == W_T_ROWS
    return std.astype(BF16), jnp.swapaxes(tr, 1, 2).astype(BF16)


def _pack_mla(w_uq, w_ukv):
    L = w_uq.shape[0]
    dq = MLA_NOPE + MLA_ROPE
    pad_r = LANES - dq
    uq = w_uq.reshape(L, MLA_Q_LORA, MLA_HEADS, dq)
    zq = lambda n: jnp.zeros((L, MLA_Q_LORA, MLA_HEADS, n), F32)
    wqa = jnp.concatenate([uq, zq(pad_r)], axis=-1)
    wqb = jnp.concatenate([zq(MLA_NOPE), _swap_halves(uq[..., MLA_NOPE:]), zq(pad_r)], axis=-1)
    ukv = w_ukv.reshape(L, MLA_KV_LORA, MLA_HEADS, MLA_NOPE + MLA_V)
    wk = jnp.concatenate([ukv[..., :MLA_NOPE],
                          jnp.zeros((L, MLA_KV_LORA, MLA_HEADS, LANES - MLA_NOPE), F32)], axis=-1)
    wv = ukv[..., MLA_NOPE:]
    flat = lambda a: a.reshape(L, a.shape[1], -1)
    tr = lambda a: jnp.swapaxes(flat(a), 1, 2).astype(BF16)
    return tr(wqa), tr(wqb), flat(wk).astype(BF16), tr(wv)


def _rope_tables(S):
    half = MLA_ROPE // 2
    inv = ROPE_THETA ** (-jnp.arange(half, dtype=F32) / half)
    ang = jnp.arange(S, dtype=jnp.int32).astype(F32)[:, None] * inv[None, :]
    cos, sin = jnp.cos(ang), jnp.sin(ang)
    pad_r = LANES - MLA_NOPE - MLA_ROPE
    cos_t = jnp.concatenate([jnp.ones((S, MLA_NOPE), F32), cos, cos, jnp.zeros((S, pad_r), F32)], axis=-1)
    sin_t = jnp.concatenate([jnp.zeros((S, MLA_NOPE), F32), -sin, sin, jnp.zeros((S, pad_r), F32)], axis=-1)
    return cos_t, sin_t


def _pack_w_o(w_o):
    L, _, D = w_o.shape
    a = MLA_HEADS * MLA_V
    b = a + DSA_HEADS * DSA_DIM
    pad = jnp.zeros((L, OUT_W - DSA_HEADS * DSA_DIM, D), F32)
    blocks = [w_o[:, :a], jnp.concatenate([w_o[:, a:b], pad], axis=1),
              jnp.concatenate([w_o[:, b:], pad], axis=1)]
    return jnp.stack(blocks, axis=1).astype(BF16)


def kernel(x, p, w_in, attn_norm, mla_q_norm, mla_w_uq, mla_kv_norm, mla_w_ukv, rel_bias, w_o,
           ffn_norm, w_gate, w_up, w_down, ple_norm, w_ple_gate, w_ple_proj, final_norm):
    B, S, D = x.shape
    depth = w_in.shape[0]
    assert S % TM == 0 and S % DSA_TK == 0 and S % MLA_TK == 0
    assert w_gate.shape[-1] % FF_CHUNK == 0
    topk = min(DSA_TOPK, S // 4)

    w_std, w_tr = _pack_w_in(w_in)
    wqa, wqb, wk, wv = _pack_mla(mla_w_uq, mla_w_ukv)
    cos_s, sin_s = _rope_tables(S)
    q_scale = (MLA_NOPE + MLA_ROPE) ** -0.5 * LOG2E
    cos_t, sin_t = cos_s.T * q_scale, sin_s.T * q_scale
    row = lambda g: g.reshape(depth, 1, -1)
    g_attn, g_q, g_kv, g_ffn, g_ple = map(row, (attn_norm, mla_q_norm, mla_kv_norm, ffn_norm, ple_norm))
    g_fin = final_norm.reshape(1, -1)
    wo = _pack_w_o(w_o)
    wg, wu, wd, wpg, wpp = (a.astype(BF16) for a in (w_gate, w_up, w_down, w_ple_gate, w_ple_proj))
    tri = jnp.triu(jnp.ones((SB_T, SB_T), F32), 0)
    tail = jnp.zeros((16, SB_T), F32).at[0].set(1.0)
    u = jnp.concatenate([tri, tail], axis=0).astype(BF16)

    tab = _bias_call(rel_bias)

    h = x
    for i in range(depth):
        (mqt, mk, mvt, dqt, dk, dvt, iqt, ik, iwt, sqt, sk, svt) = _proj_call(
            h, g_attn[i], w_std[i], w_tr[i], g_q[i], wqa[i], wqb[i], g_kv[i], wk[i], wv[i],
            cos_s, sin_s, cos_t, sin_t)
        o_mla = _mla_call(mqt, mk, mvt)
        o_dsa = _dsa_call(iqt, ik, iwt, dqt, dk, dvt, tab, topk)
        o_sb = _sb_call(sqt, sk, svt, u)
        h = _post_call(h, o_mla, o_dsa, o_sb, p, i, wo[i], g_ffn[i], wg[i], wu[i], wd[i],
                       g_ple[i], wpg[i], wpp[i], g_fin, final=(i == depth - 1))
    return h
```

```python
import functools
import math

import jax
import jax.numpy as jnp
from jax import lax
from jax.experimental import pallas as pl
from jax.experimental.pallas import tpu as pltpu

EPS = 1e-6
MLA_HEADS = 6
MLA_Q_LORA = 384
MLA_KV_LORA = 256
MLA_NOPE = 64
MLA_ROPE = 32
MLA_V = 64
ROPE_THETA = 10000.0
DSA_HEADS = 5
DSA_DIM = 64
IDX_HEADS = 8
IDX_DIM = 64
DSA_TOPK = 256
SB_HEADS = 5
SB_DIM = 64
REL_BUCKETS = 32
REL_MAX_DIST = 128

LANES = 128
HEAD = 64
VROWS = 80
OUT_W = 384
VMEM_LIMIT = 56 * 1024 * 1024

TM = 512
MLA_TQ, MLA_TK = 256, 512
MLA_UNROLL = 4
MLA_QK_PARTS = 2
SB_T = 256
SB_UNROLL = 8
DSA_TQ, DSA_TK = 128, 512
DSA_SUB = DSA_TK // DSA_TQ
DSA_UNROLL = 4
COUNT_ACC = 8
DESCENT_CHECKS = (24, 26, 28)
SLAB = 8
SORT_GROUP = 7
SORT_NETWORK = ((0, 6), (2, 3), (4, 5), (0, 2), (1, 4), (3, 6), (0, 1), (2, 5), (3, 4),
                (1, 2), (4, 6), (2, 3), (4, 5), (1, 2), (3, 4), (5, 6))
FF_CHUNK = 1408

NEG = -1e30
INT_MIN = -2 ** 31
NEG_INF_KEY = -2139095041
LOG2E = math.log2(math.e)

F32 = jnp.float32
BF16 = jnp.bfloat16

SEG_CQ = (0, 384)
SEG_CKV = (384, 640)
SEG_KRA = (640, 768)
SEG_KRB = (768, 896)
SEG_SK_IK = (896, 1280)
SEG_DK = (1280, 1408)
W_STD_COLS = 1408
ROW_DQ = (0, 320)
ROW_IQ = (320, 832)
ROW_SQ = (832, 1152)
ROW_SV = (1152, 1472)
ROW_DV = (1472, 1536)
ROW_IW = (1536, 1552)
W_T_ROWS = 1552


def _rms(x, g):
    return x * lax.rsqrt(jnp.mean(x * x, axis=-1, keepdims=True) + EPS) * g


def _dot(a, b):
    return jnp.dot(a, b, preferred_element_type=F32)


def _const_spec(shape):
    n = len(shape)
    return pl.BlockSpec(shape, lambda *_: (0,) * n, pipeline_mode=pl.Buffered(1))


def _params(*sem):
    return pltpu.CompilerParams(dimension_semantics=sem, vmem_limit_bytes=VMEM_LIMIT)


def _pipeline(n, stages):
    held = [dict() for _ in stages]
    for t in range(n + len(stages) - 1):
        for s, stage in enumerate(stages):
            i = t - s
            if 0 <= i < n:
                held[s][i] = stage(i, held[s - 1].pop(i) if s else None)


def _for_blocks(n, unroll, step):
    def body(i, c):
        step([unroll * i + d for d in range(unroll)])
        return c

    lax.fori_loop(0, n // unroll, body, 0)
    rem = n % unroll
    group = unroll // 2
    while group:
        @pl.when(rem & group != 0)
        def _(group=group):
            base = n - (rem & (2 * group - 1))
            step([base + d for d in range(group)])
        group //= 2


def _ones_rows(n):
    r = lax.broadcasted_iota(jnp.int32, (VROWS - HEAD, n), 0)
    return jnp.where(r == 0, 1.0, 0.0).astype(BF16)


def _proj_kernel(h_ref, g_ref, w_ref, wt_ref, gq_ref, wqa_ref, wqb_ref, gkv_ref, wk_ref, wv_ref,
                 cos_ref, sin_ref, cost_ref, sint_ref,
                 mqt_ref, mk_ref, mvt_ref, dqt_ref, dk_ref, dvt_ref, iqt_ref, ik_ref, iwt_ref,
                 sqt_ref, sk_ref, svt_ref):
    hn = _rms(h_ref[0], g_ref[...])
    hnt = hn.T.astype(BF16)
    hn = hn.astype(BF16)
    tm = hn.shape[0]
    ones = _ones_rows(tm)

    def seg(ab):
        return _dot(hn, w_ref[:, ab[0]:ab[1]])

    def seg_t(ab):
        return _dot(wt_ref[ab[0]:ab[1], :], hnt)

    cqn_t = _rms(seg(SEG_CQ), gq_ref[...]).T.astype(BF16)
    qa = _dot(wqa_ref[...], cqn_t)
    qb = _dot(wqb_ref[...], cqn_t)
    cos_t = cost_ref[...]
    sin_t = sint_ref[...]
    for h in range(MLA_HEADS):
        sl = slice(h * LANES, (h + 1) * LANES)
        mqt_ref[0, h] = (qa[sl] * cos_t + qb[sl] * sin_t).astype(BF16)

    ckvn = _rms(seg(SEG_CKV), gkv_ref[...])
    ckvn_t = ckvn.T.astype(BF16)
    ckvn = ckvn.astype(BF16)
    kn = _dot(ckvn, wk_ref[...])
    kr = seg(SEG_KRA) * cos_ref[...] + seg(SEG_KRB) * sin_ref[...]
    for h in range(MLA_HEADS):
        mk_ref[0, h] = (kn[:, h * LANES:(h + 1) * LANES] + kr).astype(BF16)
    vt = _dot(wv_ref[...], ckvn_t)
    for h in range(MLA_HEADS):
        mvt_ref[0, h, 0:HEAD, :] = vt[h * HEAD:(h + 1) * HEAD].astype(BF16)
        mvt_ref[0, h, HEAD:VROWS, :] = ones

    a = seg(SEG_SK_IK)
    for h in range(SB_HEADS):
        sk_ref[0, h] = a[:, h * HEAD:(h + 1) * HEAD].astype(BF16)
    ik_ref[0] = a[:, 320:384].astype(BF16)
    dk_ref[0] = seg(SEG_DK)[:, 0:HEAD].astype(BF16)

    a = seg_t(ROW_DQ) * LOG2E
    for h in range(DSA_HEADS):
        dqt_ref[0, h] = a[h * HEAD:(h + 1) * HEAD].astype(BF16)
    a = seg_t(ROW_IQ)
    for h in range(IDX_HEADS):
        iqt_ref[0, h] = a[h * HEAD:(h + 1) * HEAD].astype(BF16)
    a = seg_t(ROW_SQ) * LOG2E
    for h in range(SB_HEADS):
        sqt_ref[0, h] = a[h * HEAD:(h + 1) * HEAD].astype(BF16)
    a = seg_t(ROW_SV)
    for h in range(SB_HEADS):
        svt_ref[0, h] = a[h * HEAD:(h + 1) * HEAD].astype(BF16)
    dvt_ref[0, 0:HEAD, :] = seg_t(ROW_DV).astype(BF16)
    dvt_ref[0, HEAD:VROWS, :] = ones
    iwt_ref[0] = seg_t(ROW_IW)[0:IDX_HEADS] * IDX_HEADS ** -0.5


def _proj_call(h, g, w, wt, gq, wqa, wqb, gkv, wk, wv, cos_s, sin_s, cos_t, sin_t):
    B, S, D = h.shape
    nt = S // TM
    tok = lambda w_: pl.BlockSpec((1, TM, w_), lambda b, t: (b, t, 0))
    heads = lambda n, w_: pl.BlockSpec((1, n, TM, w_), lambda b, t: (b, 0, t, 0))
    heads_t = lambda n, r: pl.BlockSpec((1, n, r, TM), lambda b, t: (b, 0, 0, t))
    rows_t = lambda r: pl.BlockSpec((1, r, TM), lambda b, t: (b, 0, t))
    hs = lambda n, w_: jax.ShapeDtypeStruct((B, n, S, w_), BF16)
    hts = lambda n, r: jax.ShapeDtypeStruct((B, n, r, S), BF16)
    return pl.pallas_call(
        _proj_kernel,
        grid=(B, nt),
        in_specs=[tok(D), _const_spec(g.shape), _const_spec(w.shape), _const_spec(wt.shape),
                  _const_spec(gq.shape), _const_spec(wqa.shape), _const_spec(wqb.shape),
                  _const_spec(gkv.shape), _const_spec(wk.shape), _const_spec(wv.shape),
                  pl.BlockSpec((TM, LANES), lambda b, t: (t, 0)),
                  pl.BlockSpec((TM, LANES), lambda b, t: (t, 0)),
                  pl.BlockSpec((LANES, TM), lambda b, t: (0, t)),
                  pl.BlockSpec((LANES, TM), lambda b, t: (0, t))],
        out_specs=[heads_t(MLA_HEADS, LANES), heads(MLA_HEADS, LANES), heads_t(MLA_HEADS, VROWS),
                   heads_t(DSA_HEADS, HEAD), tok(HEAD), rows_t(VROWS),
                   heads_t(IDX_HEADS, HEAD), tok(HEAD), rows_t(IDX_HEADS),
                   heads_t(SB_HEADS, HEAD), heads(SB_HEADS, HEAD), heads_t(SB_HEADS, HEAD)],
        out_shape=[hts(MLA_HEADS, LANES), hs(MLA_HEADS, LANES), hts(MLA_HEADS, VROWS),
                   hts(DSA_HEADS, HEAD), jax.ShapeDtypeStruct((B, S, HEAD), BF16),
                   jax.ShapeDtypeStruct((B, VROWS, S), BF16),
                   hts(IDX_HEADS, HEAD), jax.ShapeDtypeStruct((B, S, HEAD), BF16),
                   jax.ShapeDtypeStruct((B, IDX_HEADS, S), F32),
                   hts(SB_HEADS, HEAD), hs(SB_HEADS, HEAD), hts(SB_HEADS, HEAD)],
        compiler_params=_params("parallel", "parallel"),
        name="proj",
    )(h, g, w, wt, gq, wqa, wqb, gkv, wk, wv, cos_s, sin_s, cos_t, sin_t)


def _finish(head_acc, n_heads, o_ref):
    accs = [head_acc(h) for h in range(n_heads)]
    outs = [a[0:HEAD, :] / a[HEAD:HEAD + 1, :] for a in accs]
    pad = OUT_W - n_heads * HEAD
    if pad:
        outs.append(jnp.zeros((pad, outs[0].shape[1]), F32))
    o_ref[0] = jnp.concatenate(outs, axis=0).T.astype(BF16)


def _mla_kernel(qt_ref, k_ref, vt_ref, o_ref, m_ref, acc_ref, s_ref):
    qi = pl.program_id(1)
    tq, tk = MLA_TQ, MLA_TK
    m_ref[...] = jnp.full(m_ref.shape, NEG, F32)
    acc_ref[...] = jnp.zeros(acc_ref.shape, F32)
    n_full = (qi * tq) // tk
    n_all = ((qi + 1) * tq + tk - 1) // tk

    def qk(h, j):
        k0 = pl.multiple_of(j * tk, tk)
        part = tk // MLA_QK_PARTS
        return jnp.concatenate(
            [_dot(k_ref[0, h, pl.ds(k0 + i * part, part), :], qt_ref[0, h]) for i in range(MLA_QK_PARTS)],
            axis=0)

    s_ref[...] = qk(0, 0)

    def step(js, masked):
        items = [(j, h) for j in js for h in range(MLA_HEADS)]
        n = len(items)

        def scores(i, _):
            if i == 0:
                return s_ref[...]
            if i == n:
                return qk(0, jnp.minimum(js[-1] + 1, n_all - 1))
            j, h = items[i]
            return qk(h, j)

        def running_max(i, s):
            if i == n:
                s_ref[...] = s
                return None
            j, h = items[i]
            if masked[i // MLA_HEADS]:
                kpos = j * tk + lax.broadcasted_iota(jnp.int32, (tk, tq), 0)
                qpos = qi * tq + lax.broadcasted_iota(jnp.int32, (tk, tq), 1)
                s = jnp.where(kpos <= qpos, s, NEG)
            m_prev = m_ref[h]
            m_new = jnp.maximum(m_prev, jnp.max(s, axis=0, keepdims=True))
            m_ref[h] = m_new
            return s, m_prev, m_new

        def weights_pv(i, carry):
            if i == n:
                return
            j, h = items[i]
            s, m_prev, m_new = carry
            p = jnp.exp2(s - m_new)
            k0 = pl.multiple_of(j * tk, tk)
            acc_ref[h] = (jnp.exp2(m_prev - m_new) * acc_ref[h]
                          + _dot(vt_ref[0, h, :, pl.ds(k0, tk)], p.astype(BF16)))

        _pipeline(n + 1, [scores, running_max, weights_pv])

    _for_blocks(jnp.maximum(n_full - 1, 0), MLA_UNROLL, lambda js: step(js, [False] * len(js)))

    @pl.when(n_full >= 1)
    def _():
        step([n_full - 1, n_full], [False, True])

    @pl.when(n_full == 0)
    def _():
        step([0], [True])

    _finish(lambda h: acc_ref[h], MLA_HEADS, o_ref)


def _mla_call(qt, k, vt):
    B, H, _, S = qt.shape
    res = lambda shape: pl.BlockSpec(shape, lambda b, i: (b, 0, 0, 0))
    return pl.pallas_call(
        _mla_kernel,
        grid=(B, S // MLA_TQ),
        in_specs=[pl.BlockSpec((1, H, LANES, MLA_TQ), lambda b, i: (b, 0, 0, i)),
                  res((1, H, S, LANES)), res((1, H, VROWS, S))],
        out_specs=pl.BlockSpec((1, MLA_TQ, OUT_W), lambda b, i: (b, i, 0)),
        out_shape=jax.ShapeDtypeStruct((B, S, OUT_W), BF16),
        scratch_shapes=[pltpu.VMEM((H, 1, MLA_TQ), F32), pltpu.VMEM((H, VROWS, MLA_TQ), F32),
                        pltpu.VMEM((MLA_TK, MLA_TQ), F32)],
        compiler_params=_params("parallel", "arbitrary"),
        name="mla",
    )(qt, k, vt)


def _sb_kernel(qt_ref, k_ref, vt_ref, u_ref, o_ref, c_ref, acc_ref, z_ref):
    qi = pl.program_id(1)
    t = SB_T
    c_ref[...] = jnp.zeros(c_ref.shape, F32)
    acc_ref[...] = jnp.zeros(acc_ref.shape, F32)

    def qk(h, j):
        k0 = pl.multiple_of(j * t, t)
        half = t // 2
        return jnp.concatenate(
            [_dot(k_ref[0, h, pl.ds(k0 + s * half, half), :], qt_ref[0, h]) for s in range(2)],
            axis=0)

    strict = (lax.broadcasted_iota(jnp.int32, (t, t), 0) < lax.broadcasted_iota(jnp.int32, (t, t), 1))

    def suffix_sums(z2, masked):
        neg_abs = pltpu.bitcast(pltpu.bitcast(z2, jnp.int32) | INT_MIN, F32)
        nl = jnp.maximum(z2, 0.0) + jnp.log2(1.0 + jnp.exp2(neg_abs))
        if masked:
            nl = jnp.where(strict, nl, 0.0)
        nlb = nl.astype(BF16)
        sums = jnp.concatenate([_dot(u_ref[0:t // 2, :], nlb), _dot(u_ref[t // 2:, :], nlb)], axis=0)
        return z2, sums

    z_ref[...] = qk(0, qi)

    def step(js, masked):
        items = [(j, h) for j in js for h in range(SB_HEADS)]
        n = len(items)

        def logits(i, _):
            if i == 0:
                return z_ref[...]
            if i == n:
                return qk(0, jnp.maximum(js[-1] - 1, 0))
            j, h = items[i]
            return qk(h, j)

        def sums_stage(i, z):
            if i == n:
                z_ref[...] = z
                return None
            return suffix_sums(z, masked[i // SB_HEADS])

        def weights_pv(i, carry):
            if i == n:
                return
            j, h = items[i]
            z2, sums = carry
            w = jnp.exp2(z2 - sums[0:t])
            if masked[i // SB_HEADS]:
                w = jnp.where(strict, w, 0.0)
            k0 = pl.multiple_of(j * t, t)
            acc_ref[h] += jnp.exp2(-c_ref[h]) * _dot(vt_ref[0, h, :, pl.ds(k0, t)], w.astype(BF16))
            c_ref[h] += sums[t:t + 1]

        _pipeline(n + 1, [logits, sums_stage, weights_pv])

    @pl.when(qi >= 1)
    def _():
        step([qi, qi - 1], [True, False])

    @pl.when(qi == 0)
    def _():
        step([0], [True])

    _for_blocks(jnp.maximum(qi - 1, 0), SB_UNROLL,
                lambda idx: step([qi - 2 - i for i in idx], [False] * len(idx)))

    outs = [acc_ref[h] for h in range(SB_HEADS)]
    outs.append(jnp.zeros((OUT_W - SB_HEADS * HEAD, t), F32))
    o_ref[0] = jnp.concatenate(outs, axis=0).T.astype(BF16)


def _sb_call(qt, k, vt, u):
    B, H, _, S = qt.shape
    res = lambda shape: pl.BlockSpec(shape, lambda b, i: (b, 0, 0, 0))
    return pl.pallas_call(
        _sb_kernel,
        grid=(B, S // SB_T),
        in_specs=[pl.BlockSpec((1, H, HEAD, SB_T), lambda b, i: (b, 0, 0, i)),
                  res((1, H, S, HEAD)), res((1, H, HEAD, S)), _const_spec(u.shape)],
        out_specs=pl.BlockSpec((1, SB_T, OUT_W), lambda b, i: (b, i, 0)),
        out_shape=jax.ShapeDtypeStruct((B, S, OUT_W), BF16),
        scratch_shapes=[pltpu.VMEM((H, 1, SB_T), F32), pltpu.VMEM((H, HEAD, SB_T), F32),
                        pltpu.VMEM((SB_T, SB_T), F32)],
        compiler_params=_params("parallel", "arbitrary"),
        name="stickbreak",
    )(qt, k, vt, u)


def _bias_kernel(rb_ref, o_ref):
    v = pl.program_id(0)
    shape = (DSA_TK, DSA_TQ)
    off = jnp.where(v < DSA_SUB, v * DSA_TQ, DSA_TK)
    dist = off + lax.broadcasted_iota(jnp.int32, shape, 1) - lax.broadcasted_iota(jnp.int32, shape, 0)
    max_exact = REL_BUCKETS // 2
    d = jnp.maximum(dist, 1).astype(F32)
    large = max_exact + (jnp.log(d / max_exact) / math.log(REL_MAX_DIST / max_exact)
                         * (REL_BUCKETS - max_exact)).astype(jnp.int32)
    large = jnp.minimum(large, REL_BUCKETS - 1)
    bucket = jnp.where(dist < max_exact, dist, large)
    for h in range(DSA_HEADS):
        t = jnp.zeros(shape, F32)
        for b in range(REL_BUCKETS):
            t = jnp.where(bucket == b, (rb_ref[b, h] - rb_ref[REL_BUCKETS - 1, h]) * LOG2E, t)
        o_ref[0, h] = jnp.where(dist >= 0, t, NEG)


def _bias_call(rel_bias):
    nv = DSA_SUB + 1
    return pl.pallas_call(
        _bias_kernel,
        grid=(nv,),
        in_specs=[pl.BlockSpec(memory_space=pltpu.SMEM)],
        out_specs=pl.BlockSpec((1, DSA_HEADS, DSA_TK, DSA_TQ), lambda v: (v, 0, 0, 0)),
        out_shape=jax.ShapeDtypeStruct((nv, DSA_HEADS, DSA_TK, DSA_TQ), F32),
        compiler_params=_params("arbitrary"),
        name="bias_tiles",
    )(rel_bias)


def _dsa_kernel(iqt_ref, ik_ref, iwt_ref, dqt_ref, dk_ref, dvt_ref, tab_ref, o_ref,
                keys_ref, sorted_ref, m_ref, acc_ref, x_ref, *, topk):
    qi = pl.program_id(1)
    n_blk = qi // DSA_SUB + 1
    sub = qi % DSA_SUB
    tq, tk = DSA_TQ, DSA_TK

    iqt = jnp.concatenate([iqt_ref[0, h] for h in range(IDX_HEADS)], axis=1)
    iw = iwt_ref[0]

    def pair_scores(k_ref_, qt_all, g, j, n_heads):
        k0 = pl.multiple_of(j * tk, tk)
        cols = qt_all[:, 2 * g * tq:min(2 * (g + 1), n_heads) * tq]
        half = tk // 2
        return jnp.concatenate(
            [_dot(k_ref_[0, pl.ds(k0 + i * half, half), :], cols) for i in range(2)], axis=0)

    n_slab = tk // SLAB
    n_group = n_slab // SORT_GROUP
    zero_shift = 23 - (keys_ref.shape[0] * tk - 1).bit_length()
    assert zero_shift >= 0
    zero_key = -1 - lax.shift_left(lax.broadcasted_iota(jnp.int32, (tk, tq), 0), zero_shift)

    def sort_groups(key):
        slabs = [key[SLAB * i:SLAB * (i + 1)] for i in range(n_slab)]
        for g in range(n_group):
            v = slabs[SORT_GROUP * g:SORT_GROUP * (g + 1)]
            for a, b in SORT_NETWORK:
                v[a], v[b] = jnp.maximum(v[a], v[b]), jnp.minimum(v[a], v[b])
            slabs[SORT_GROUP * g:SORT_GROUP * (g + 1)] = v
        return jnp.concatenate(slabs, axis=0)

    x_ref[...] = pair_scores(ik_ref, iqt, 0, 0, IDX_HEADS)

    def idx_step(js, masked):
        n_pairs = IDX_HEADS // 2
        items = [(j, g) for j in js for g in range(n_pairs)]
        n = len(items)
        total = {}

        def scores(i, _):
            if i == 0:
                return x_ref[...]
            if i == n:
                return pair_scores(ik_ref, iqt, 0, jnp.minimum(js[-1] + 1, n_blk - 1), IDX_HEADS)
            j, g = items[i]
            return pair_scores(ik_ref, iqt, g, j, IDX_HEADS)

        def weigh(i, x):
            if i == n:
                x_ref[...] = x
                return
            j, g = items[i]
            part = (jnp.maximum(x[:, 0:tq], 0.0) * iw[2 * g:2 * g + 1]
                    + jnp.maximum(x[:, tq:2 * tq], 0.0) * iw[2 * g + 1:2 * g + 2])
            blk = i // n_pairs
            total[blk] = part if g == 0 else total[blk] + part
            if g < n_pairs - 1:
                return
            bits = pltpu.bitcast(total.pop(blk), jnp.int32)
            key = bits ^ (lax.shift_right_arithmetic(bits, 31) & 0x7FFFFFFF)
            key = jnp.where((bits & 0x7F800000) == 0, zero_key - lax.shift_left(j * tk, zero_shift), key)
            if masked:
                kpos = j * tk + lax.broadcasted_iota(jnp.int32, key.shape, 0)
                qpos = qi * tq + lax.broadcasted_iota(jnp.int32, key.shape, 1)
                key = jnp.where(kpos <= qpos, key, NEG_INF_KEY)
            keys_ref[j] = key
            sorted_ref[j] = sort_groups(key)

        _pipeline(n + 1, [scores, weigh])

    _for_blocks(n_blk - 1, DSA_UNROLL, lambda js: idx_step(js, False))
    idx_step([n_blk - 1], True)

    def over_blocks(per_block, init):
        def pair(i, acc):
            return per_block(2 * i + 1, per_block(2 * i, acc))

        def last(_, acc):
            return per_block(n_blk - 1, acc)

        return lax.fori_loop(0, n_blk % 2, last, lax.fori_loop(0, n_blk // 2, pair, init))

    def count_ge(cand):
        def per_block(j, acc):
            c = jnp.where(keys_ref[j] >= cand, 1.0, 0.0)
            return acc + jnp.sum(c.reshape(COUNT_ACC, tk // COUNT_ACC, tq), axis=0)
        acc = over_blocks(per_block, jnp.zeros((tk // COUNT_ACC, tq), F32))
        return jnp.sum(acc, axis=0, keepdims=True)

    def count_ge_sorted(cand):
        def one(flag):
            return jnp.where(flag, 1.0, 0.0)

        def per_block(j, accs):
            accs = list(accs)
            blk = sorted_ref[j]
            for g in range(n_group):
                s = [blk[SLAB * (SORT_GROUP * g + k):SLAB * (SORT_GROUP * g + k + 1)]
                     for k in range(SORT_GROUP)]
                b4 = s[3] >= cand
                b2 = jnp.where(b4, s[5], s[1]) >= cand
                b1 = jnp.where(b4, jnp.where(b2, s[6], s[4]), jnp.where(b2, s[2], s[0])) >= cand
                o = 3 * (g % 2)
                accs[o] += one(b4)
                accs[o + 1] += one(b2)
                accs[o + 2] += one(b1)
            for r in range(SORT_GROUP * n_group, n_slab):
                accs[2] += one(blk[SLAB * r:SLAB * (r + 1)] >= cand)
            return tuple(accs)

        z = jnp.zeros((SLAB, tq), F32)
        a = over_blocks(per_block, (z,) * 6)
        total = 4.0 * (a[0] + a[3]) + 2.0 * (a[1] + a[4]) + (a[2] + a[5])
        return jnp.sum(total, axis=0, keepdims=True)

    def bit_body(i, state):
        t_u, done, thr_hit = state
        cand_u = t_u | lax.shift_left(jnp.int32(1), 31 - i)
        cand = cand_u ^ INT_MIN
        cnt = count_ge_sorted(cand)
        hit = cnt == topk
        thr_hit = jnp.where(hit, jnp.where(done > 0.0, thr_hit, cand), thr_hit)
        done = jnp.where(hit, 1.0, done)
        return jnp.where(cnt >= topk, cand_u, t_u), done, thr_hit

    zero = jnp.zeros((1, tq), jnp.int32)
    state = (zero, jnp.zeros((1, tq), F32), zero)
    first = 0
    for last in DESCENT_CHECKS + (32,):
        stop = last if first == 0 else jnp.where(jnp.min(state[1]) < 1.0, last, first)
        state = lax.fori_loop(first, stop, bit_body, state)
        first = last
    t_u, done, thr_hit = state
    thr = jnp.where(done > 0.0, thr_hit, t_u ^ INT_MIN)

    tie = jnp.where(done > 0.0, 0.0, jnp.where(thr > NEG_INF_KEY, 1.0, 0.0))

    @pl.when(jnp.max(tie) > 0.0)
    def _():
        need = jnp.where(tie > 0.0, topk - count_ge(thr + 1), float(tk * keys_ref.shape[0] + 1))
        row = lax.broadcasted_iota(jnp.int32, (tk, tk), 0)
        col = lax.broadcasted_iota(jnp.int32, (tk, tk), 1)
        before = jnp.where(col < row, 1.0, 0.0).astype(BF16)

        def body(j, run):
            kb = keys_ref[j]
            eq = jnp.where(kb == thr, 1.0, 0.0)
            rank = _dot(before, eq.astype(BF16)) + run
            keys_ref[j] = jnp.where(kb == thr, jnp.where(rank >= need, thr - 1, kb), kb)
            return run + jnp.sum(eq, axis=0, keepdims=True)

        lax.fori_loop(0, n_blk, body, jnp.zeros((1, tq), F32))

    dqt = jnp.concatenate([dqt_ref[0, h] for h in range(DSA_HEADS)], axis=1)
    m_ref[...] = jnp.full(m_ref.shape, NEG, F32)
    acc_ref[...] = jnp.zeros(acc_ref.shape, F32)

    x_ref[...] = pair_scores(dk_ref, dqt, 0, 0, DSA_HEADS)

    def att_step(js, variants):
        n_pairs = (DSA_HEADS + 1) // 2
        items = [(j, g) for j in js for g in range(n_pairs)]
        n = len(items)

        def scores(i, _):
            if i == 0:
                return x_ref[...]
            if i == n:
                return pair_scores(dk_ref, dqt, 0, jnp.minimum(js[-1] + 1, n_blk - 1), DSA_HEADS)
            j, g = items[i]
            return pair_scores(dk_ref, dqt, g, j, DSA_HEADS)

        def running_max(i, s):
            if i == n:
                x_ref[...] = s
                return None
            j, g = items[i]
            variant = variants[i // n_pairs]
            sel = keys_ref[j] >= thr
            out = []
            for h in range(2 * g, min(2 * (g + 1), DSA_HEADS)):
                a = s[:, (h - 2 * g) * tq:(h - 2 * g + 1) * tq]
                if variant is not None:
                    a = a + tab_ref[variant, h]
                a = jnp.where(sel, a, NEG)
                m_prev = m_ref[h]
                m_new = jnp.maximum(m_prev, jnp.max(a, axis=0, keepdims=True))
                m_ref[h] = m_new
                out.append((h, a, m_prev, m_new))
            return out

        def weights_pv(i, carry):
            if i == n:
                return
            j, _ = items[i]
            vt = dvt_ref[0, :, pl.ds(pl.multiple_of(j * tk, tk), tk)]
            for h, a, m_prev, m_new in carry:
                p = jnp.exp2(a - m_new)
                acc_ref[h] = jnp.exp2(m_prev - m_new) * acc_ref[h] + _dot(vt, p.astype(BF16))

        _pipeline(n + 1, [scores, running_max, weights_pv])

    _for_blocks(jnp.maximum(n_blk - 2, 0), DSA_UNROLL, lambda js: att_step(js, [None] * len(js)))

    @pl.when(jnp.logical_and(n_blk >= 2, sub == 0))
    def _():
        att_step([n_blk - 2, n_blk - 1], [DSA_SUB, 0])

    @pl.when(jnp.logical_and(n_blk >= 2, sub != 0))
    def _():
        att_step([n_blk - 2, n_blk - 1], [None, sub])

    @pl.when(n_blk == 1)
    def _():
        att_step([0], [sub])
    _finish(lambda h: acc_ref[h], DSA_HEADS, o_ref)


def _dsa_call(iqt, ik, iwt, dqt, dk, dvt, tab, topk):
    B, _, _, S = iqt.shape
    tq = DSA_TQ
    qh = lambda n: pl.BlockSpec((1, n, HEAD, tq), lambda b, i: (b, 0, 0, i))
    res = lambda shape: pl.BlockSpec(shape, lambda b, i: (b, 0, 0))
    return pl.pallas_call(
        functools.partial(_dsa_kernel, topk=topk),
        grid=(B, S // tq),
        in_specs=[qh(IDX_HEADS), res((1, S, HEAD)),
                  pl.BlockSpec((1, IDX_HEADS, tq), lambda b, i: (b, 0, i)),
                  qh(DSA_HEADS), res((1, S, HEAD)), res((1, VROWS, S)), _const_spec(tab.shape)],
        out_specs=pl.BlockSpec((1, tq, OUT_W), lambda b, i: (b, i, 0)),
        out_shape=jax.ShapeDtypeStruct((B, S, OUT_W), BF16),
        scratch_shapes=[pltpu.VMEM((S // DSA_TK, DSA_TK, tq), jnp.int32),
                        pltpu.VMEM((S // DSA_TK, DSA_TK, tq), jnp.int32),
                        pltpu.VMEM((DSA_HEADS, 1, tq), F32),
                        pltpu.VMEM((DSA_HEADS, VROWS, tq), F32),
                        pltpu.VMEM((DSA_TK, 2 * tq), F32)],
        compiler_params=_params("parallel", "arbitrary"),
        name="dsa",
    )(iqt, ik, iwt, dqt, dk, dvt, tab)


def _post_kernel(h_ref, om_ref, od_ref, os_ref, p_ref, wo_ref, gf_ref, wg_ref, wu_ref, wd_ref,
                 gp_ref, wpg_ref, wpp_ref, gfin_ref, out_ref, *, final):
    h = (h_ref[0] + _dot(om_ref[0], wo_ref[0]) + _dot(od_ref[0], wo_ref[1])
         + _dot(os_ref[0], wo_ref[2]))

    hf = _rms(h, gf_ref[...]).astype(BF16)
    d_ff = wg_ref.shape[1]
    ffn = jnp.zeros(h.shape, F32)
    for c0 in range(0, d_ff, FF_CHUNK):
        g = _dot(hf, wg_ref[:, c0:c0 + FF_CHUNK])
        u = _dot(hf, wu_ref[:, c0:c0 + FF_CHUNK])
        ffn = ffn + _dot((g * jax.nn.sigmoid(g) * u).astype(BF16), wd_ref[c0:c0 + FF_CHUNK, :])
    h = h + ffn

    gate = jax.nn.sigmoid(_dot(_rms(h, gp_ref[...]).astype(BF16), wpg_ref[...]))
    h = h + gate * _dot(p_ref[0, 0].astype(BF16), wpp_ref[...])
    if final:
        h = _rms(h, gfin_ref[...])
    out_ref[0] = h


def _post_call(h, om, od, os_, p, layer, wo, gf, wg, wu, wd, gp, wpg, wpp, gfin, final):
    B, S, D = h.shape
    mix = pl.BlockSpec((1, TM, OUT_W), lambda b, t: (b, t, 0))
    tok = pl.BlockSpec((1, TM, D), lambda b, t: (b, t, 0))
    return pl.pallas_call(
        functools.partial(_post_kernel, final=final),
        grid=(B, S // TM),
        in_specs=[tok, mix, mix, mix,
                  pl.BlockSpec((1, 1, TM, p.shape[-1]), lambda b, t: (layer, b, t, 0)),
                  _const_spec(wo.shape), _const_spec(gf.shape), _const_spec(wg.shape),
                  _const_spec(wu.shape), _const_spec(wd.shape), _const_spec(gp.shape),
                  _const_spec(wpg.shape), _const_spec(wpp.shape), _const_spec(gfin.shape)],
        out_specs=tok,
        out_shape=jax.ShapeDtypeStruct((B, S, D), F32),
        compiler_params=_params("parallel", "parallel"),
        name="post",
    )(h, om, od, os_, p, wo, gf, wg, wu, wd, gp, wpg, wpp, gfin)


def _swap_halves(w):
    half = w.shape[-1] // 2
    return jnp.concatenate([w[..., half:], w[..., :half]], axis=-1)


def _pack_w_in(w_in):
    L, D, _ = w_in.shape
    z = lambda n: jnp.zeros((L, D, n), F32)
    o = 0
    cols = {}
    for name, n in (("cq", MLA_Q_LORA), ("ckv", MLA_KV_LORA), ("kr", MLA_ROPE),
                    ("dq", DSA_HEADS * DSA_DIM), ("dk", DSA_DIM), ("dv", DSA_DIM),
                    ("iq", IDX_HEADS * IDX_DIM), ("ik", IDX_DIM), ("iw", IDX_HEADS),
                    ("sq", SB_HEADS * SB_DIM), ("sk", SB_HEADS * SB_DIM), ("sv", SB_HEADS * SB_DIM)):
        cols[name] = w_in[:, :, o:o + n]
        o += n
    pad_r = LANES - MLA_NOPE - MLA_ROPE
    std = jnp.concatenate([
        cols["cq"], cols["ckv"],
        z(MLA_NOPE), cols["kr"], z(pad_r),
        z(MLA_NOPE), _swap_halves(cols["kr"]), z(pad_r),
        cols["sk"], cols["ik"],
        cols["dk"], z(LANES - DSA_DIM)], axis=-1)
    tr = jnp.concatenate([
        cols["dq"] * DSA_DIM ** -0.5, cols["iq"] * IDX_DIM ** -0.5, cols["sq"] * SB_DIM ** -0.5,
        cols["sv"], cols["dv"], cols["iw"], z(ROW_IW[1] - ROW_IW[0] - IDX_HEADS)], axis=-1)
    assert std.shape[-1] == W_STD_COLS and tr.shape[-1] == W_T_ROWS
    return std.astype(BF16), jnp.swapaxes(tr, 1, 2).astype(BF16)


def _pack_mla(w_uq, w_ukv):
    L = w_uq.shape[0]
    dq = MLA_NOPE + MLA_ROPE
    pad_r = LANES - dq
    uq = w_uq.reshape(L, MLA_Q_LORA, MLA_HEADS, dq)
    zq = lambda n: jnp.zeros((L, MLA_Q_LORA, MLA_HEADS, n), F32)
    wqa = jnp.concatenate([uq, zq(pad_r)], axis=-1)
    wqb = jnp.concatenate([zq(MLA_NOPE), _swap_halves(uq[..., MLA_NOPE:]), zq(pad_r)], axis=-1)
    ukv = w_ukv.reshape(L, MLA_KV_LORA, MLA_HEADS, MLA_NOPE + MLA_V)
    wk = jnp.concatenate([ukv[..., :MLA_NOPE],
                          jnp.zeros((L, MLA_KV_LORA, MLA_HEADS, LANES - MLA_NOPE), F32)], axis=-1)
    wv = ukv[..., MLA_NOPE:]
    flat = lambda a: a.reshape(L, a.shape[1], -1)
    tr = lambda a: jnp.swapaxes(flat(a), 1, 2).astype(BF16)
    return tr(wqa), tr(wqb), flat(wk).astype(BF16), tr(wv)


def _rope_tables(S):
    half = MLA_ROPE // 2
    inv = ROPE_THETA ** (-jnp.arange(half, dtype=F32) / half)
    ang = jnp.arange(S, dtype=jnp.int32).astype(F32)[:, None] * inv[None, :]
    cos, sin = jnp.cos(ang), jnp.sin(ang)
    pad_r = LANES - MLA_NOPE - MLA_ROPE
    cos_t = jnp.concatenate([jnp.ones((S, MLA_NOPE), F32), cos, cos, jnp.zeros((S, pad_r), F32)], axis=-1)
    sin_t = jnp.concatenate([jnp.zeros((S, MLA_NOPE), F32), -sin, sin, jnp.zeros((S, pad_r), F32)], axis=-1)
    return cos_t, sin_t


def _pack_w_o(w_o):
    L, _, D = w_o.shape
    a = MLA_HEADS * MLA_V
    b = a + DSA_HEADS * DSA_DIM
    pad = jnp.zeros((L, OUT_W - DSA_HEADS * DSA_DIM, D), F32)
    blocks = [w_o[:, :a], jnp.concatenate([w_o[:, a:b], pad], axis=1),
              jnp.concatenate([w_o[:, b:], pad], axis=1)]
    return jnp.stack(blocks, axis=1).astype(BF16)


def kernel(x, p, w_in, attn_norm, mla_q_norm, mla_w_uq, mla_kv_norm, mla_w_ukv, rel_bias, w_o,
           ffn_norm, w_gate, w_up, w_down, ple_norm, w_ple_gate, w_ple_proj, final_norm):
    B, S, D = x.shape
    depth = w_in.shape[0]
    assert S % TM == 0 and S % DSA_TK == 0 and S % MLA_TK == 0
    assert w_gate.shape[-1] % FF_CHUNK == 0
    topk = min(DSA_TOPK, S // 4)

    w_std, w_tr = _pack_w_in(w_in)
    wqa, wqb, wk, wv = _pack_mla(mla_w_uq, mla_w_ukv)
    cos_s, sin_s = _rope_tables(S)
    q_scale = (MLA_NOPE + MLA_ROPE) ** -0.5 * LOG2E
    cos_t, sin_t = cos_s.T * q_scale, sin_s.T * q_scale
    row = lambda g: g.reshape(depth, 1, -1)
    g_attn, g_q, g_kv, g_ffn, g_ple = map(row, (attn_norm, mla_q_norm, mla_kv_norm, ffn_norm, ple_norm))
    g_fin = final_norm.reshape(1, -1)
    wo = _pack_w_o(w_o)
    wg, wu, wd, wpg, wpp = (a.astype(BF16) for a in (w_gate, w_up, w_down, w_ple_gate, w_ple_proj))
    tri = jnp.triu(jnp.ones((SB_T, SB_T), F32), 0)
    tail = jnp.zeros((16, SB_T), F32).at[0].set(1.0)
    u = jnp.concatenate([tri, tail], axis=0).astype(BF16)

    tab = _bias_call(rel_bias)

    h = x
    for i in range(depth):
        (mqt, mk, mvt, dqt, dk, dvt, iqt, ik, iwt, sqt, sk, svt) = _proj_call(
            h, g_attn[i], w_std[i], w_tr[i], g_q[i], wqa[i], wqb[i], g_kv[i], wk[i], wv[i],
            cos_s, sin_s, cos_t, sin_t)
        o_mla = _mla_call(mqt, mk, mvt)
        o_dsa = _dsa_call(iqt, ik, iwt, dqt, dk, dvt, tab, topk)
        o_sb = _sb_call(sqt, sk, svt, u)
        h = _post_call(h, o_mla, o_dsa, o_sb, p, i, wo[i], g_ffn[i], wg[i], wu[i], wd[i],
                       g_ple[i], wpg[i], wpp[i], g_fin, final=(i == depth - 1))
    return h
```

```python
import functools
import math

import jax
import jax.numpy as jnp
from jax import lax
from jax.experimental import pallas as pl
from jax.experimental.pallas import tpu as pltpu

EPS = 1e-6
MLA_HEADS = 6
MLA_Q_LORA = 384
MLA_KV_LORA = 256
MLA_NOPE = 64
MLA_ROPE = 32
MLA_V = 64
ROPE_THETA = 10000.0
DSA_HEADS = 5
DSA_DIM = 64
IDX_HEADS = 8
IDX_DIM = 64
DSA_TOPK = 256
SB_HEADS = 5
SB_DIM = 64
REL_BUCKETS = 32
REL_MAX_DIST = 128

LANES = 128
HEAD = 64
VROWS = 80
OUT_W = 384
VMEM_LIMIT = 56 * 1024 * 1024

TM = 512
MLA_TQ, MLA_TK = 256, 512
MLA_UNROLL = 4
MLA_QK_PARTS = 2
SB_T = 256
SB_UNROLL = 8
DSA_TQ, DSA_TK = 128, 512
DSA_SUB = DSA_TK // DSA_TQ
DSA_UNROLL = 4
COUNT_ACC = 8
DESCENT_CHECKS = (24, 26, 28)
SLAB = 8
SORT_GROUP = 7
SORT_NETWORK = ((0, 6), (2, 3), (4, 5), (0, 2), (1, 4), (3, 6), (0, 1), (2, 5), (3, 4),
                (1, 2), (4, 6), (2, 3), (4, 5), (1, 2), (3, 4), (5, 6))
FF_CHUNK = 1408

NEG = -1e30
INT_MIN = -2 ** 31
NEG_INF_KEY = -2139095041
LOG2E = math.log2(math.e)

F32 = jnp.float32
BF16 = jnp.bfloat16

SEG_CQ = (0, 384)
SEG_CKV = (384, 640)
SEG_KRA = (640, 768)
SEG_KRB = (768, 896)
SEG_SK_IK = (896, 1280)
SEG_DK = (1280, 1408)
W_STD_COLS = 1408
ROW_DQ = (0, 320)
ROW_IQ = (320, 832)
ROW_SQ = (832, 1152)
ROW_SV = (1152, 1472)
ROW_DV = (1472, 1536)
ROW_IW = (1536, 1552)
W_T_ROWS = 1552


def _rms(x, g):
    return x * lax.rsqrt(jnp.mean(x * x, axis=-1, keepdims=True) + EPS) * g


def _dot(a, b):
    return jnp.dot(a, b, preferred_element_type=F32)


def _const_spec(shape):
    n = len(shape)
    return pl.BlockSpec(shape, lambda *_: (0,) * n, pipeline_mode=pl.Buffered(1))


def _params(*sem):
    return pltpu.CompilerParams(dimension_semantics=sem, vmem_limit_bytes=VMEM_LIMIT)


def _pipeline(n, stages):
    held = [dict() for _ in stages]
    for t in range(n + len(stages) - 1):
        for s, stage in enumerate(stages):
            i = t - s
            if 0 <= i < n:
                held[s][i] = stage(i, held[s - 1].pop(i) if s else None)


def _for_blocks(n, unroll, step):
    def body(i, c):
        step([unroll * i + d for d in range(unroll)])
        return c

    lax.fori_loop(0, n // unroll, body, 0)
    rem = n % unroll
    group = unroll // 2
    while group:
        @pl.when(rem & group != 0)
        def _(group=group):
            base = n - (rem & (2 * group - 1))
            step([base + d for d in range(group)])
        group //= 2


def _ones_rows(n):
    r = lax.broadcasted_iota(jnp.int32, (VROWS - HEAD, n), 0)
    return jnp.where(r == 0, 1.0, 0.0).astype(BF16)


def _proj_kernel(h_ref, g_ref, w_ref, wt_ref, gq_ref, wqa_ref, wqb_ref, gkv_ref, wk_ref, wv_ref,
                 cos_ref, sin_ref, cost_ref, sint_ref,
                 mqt_ref, mk_ref, mvt_ref, dqt_ref, dk_ref, dvt_ref, iqt_ref, ik_ref, iwt_ref,
                 sqt_ref, sk_ref, svt_ref):
    hn = _rms(h_ref[0], g_ref[...])
    hnt = hn.T.astype(BF16)
    hn = hn.astype(BF16)
    tm = hn.shape[0]
    ones = _ones_rows(tm)

    def seg(ab):
        return _dot(hn, w_ref[:, ab[0]:ab[1]])

    def seg_t(ab):
        return _dot(wt_ref[ab[0]:ab[1], :], hnt)

    cqn_t = _rms(seg(SEG_CQ), gq_ref[...]).T.astype(BF16)
    qa = _dot(wqa_ref[...], cqn_t)
    qb = _dot(wqb_ref[...], cqn_t)
    cos_t = cost_ref[...]
    sin_t = sint_ref[...]
    for h in range(MLA_HEADS):
        sl = slice(h * LANES, (h + 1) * LANES)
        mqt_ref[0, h] = (qa[sl] * cos_t + qb[sl] * sin_t).astype(BF16)

    ckvn = _rms(seg(SEG_CKV), gkv_ref[...])
    ckvn_t = ckvn.T.astype(BF16)
    ckvn = ckvn.astype(BF16)
    kn = _dot(ckvn, wk_ref[...])
    kr = seg(SEG_KRA) * cos_ref[...] + seg(SEG_KRB) * sin_ref[...]
    for h in range(MLA_HEADS):
        mk_ref[0, h] = (kn[:, h * LANES:(h + 1) * LANES] + kr).astype(BF16)
    vt = _dot(wv_ref[...], ckvn_t)
    for h in range(MLA_HEADS):
        mvt_ref[0, h, 0:HEAD, :] = vt[h * HEAD:(h + 1) * HEAD].astype(BF16)
        mvt_ref[0, h, HEAD:VROWS, :] = ones

    a = seg(SEG_SK_IK)
    for h in range(SB_HEADS):
        sk_ref[0, h] = a[:, h * HEAD:(h + 1) * HEAD].astype(BF16)
    ik_ref[0] = a[:, 320:384].astype(BF16)
    dk_ref[0] = seg(SEG_DK)[:, 0:HEAD].astype(BF16)

    a = seg_t(ROW_DQ) * LOG2E
    for h in range(DSA_HEADS):
        dqt_ref[0, h] = a[h * HEAD:(h + 1) * HEAD].astype(BF16)
    a = seg_t(ROW_IQ)
    for h in range(IDX_HEADS):
        iqt_ref[0, h] = a[h * HEAD:(h + 1) * HEAD].astype(BF16)
    a = seg_t(ROW_SQ) * LOG2E
    for h in range(SB_HEADS):
        sqt_ref[0, h] = a[h * HEAD:(h + 1) * HEAD].astype(BF16)
    a = seg_t(ROW_SV)
    for h in range(SB_HEADS):
        svt_ref[0, h] = a[h * HEAD:(h + 1) * HEAD].astype(BF16)
    dvt_ref[0, 0:HEAD, :] = seg_t(ROW_DV).astype(BF16)
    dvt_ref[0, HEAD:VROWS, :] = ones
    iwt_ref[0] = seg_t(ROW_IW)[0:IDX_HEADS] * IDX_HEADS ** -0.5


def _proj_call(h, g, w, wt, gq, wqa, wqb, gkv, wk, wv, cos_s, sin_s, cos_t, sin_t):
    B, S, D = h.shape
    nt = S // TM
    tok = lambda w_: pl.BlockSpec((1, TM, w_), lambda b, t: (b, t, 0))
    heads = lambda n, w_: pl.BlockSpec((1, n, TM, w_), lambda b, t: (b, 0, t, 0))
    heads_t = lambda n, r: pl.BlockSpec((1, n, r, TM), lambda b, t: (b, 0, 0, t))
    rows_t = lambda r: pl.BlockSpec((1, r, TM), lambda b, t: (b, 0, t))
    hs = lambda n, w_: jax.ShapeDtypeStruct((B, n, S, w_), BF16)
    hts = lambda n, r: jax.ShapeDtypeStruct((B, n, r, S), BF16)
    return pl.pallas_call(
        _proj_kernel,
        grid=(B, nt),
        in_specs=[tok(D), _const_spec(g.shape), _const_spec(w.shape), _const_spec(wt.shape),
                  _const_spec(gq.shape), _const_spec(wqa.shape), _const_spec(wqb.shape),
                  _const_spec(gkv.shape), _const_spec(wk.shape), _const_spec(wv.shape),
                  pl.BlockSpec((TM, LANES), lambda b, t: (t, 0)),
                  pl.BlockSpec((TM, LANES), lambda b, t: (t, 0)),
                  pl.BlockSpec((LANES, TM), lambda b, t: (0, t)),
                  pl.BlockSpec((LANES, TM), lambda b, t: (0, t))],
        out_specs=[heads_t(MLA_HEADS, LANES), heads(MLA_HEADS, LANES), heads_t(MLA_HEADS, VROWS),
                   heads_t(DSA_HEADS, HEAD), tok(HEAD), rows_t(VROWS),
                   heads_t(IDX_HEADS, HEAD), tok(HEAD), rows_t(IDX_HEADS),
                   heads_t(SB_HEADS, HEAD), heads(SB_HEADS, HEAD), heads_t(SB_HEADS, HEAD)],
        out_shape=[hts(MLA_HEADS, LANES), hs(MLA_HEADS, LANES), hts(MLA_HEADS, VROWS),
                   hts(DSA_HEADS, HEAD), jax.ShapeDtypeStruct((B, S, HEAD), BF16),
                   jax.ShapeDtypeStruct((B, VROWS, S), BF16),
                   hts(IDX_HEADS, HEAD), jax.ShapeDtypeStruct((B, S, HEAD), BF16),
                   jax.ShapeDtypeStruct((B, IDX_HEADS, S), F32),
                   hts(SB_HEADS, HEAD), hs(SB_HEADS, HEAD), hts(SB_HEADS, HEAD)],
        compiler_params=_params("parallel", "parallel"),
        name="proj",
    )(h, g, w, wt, gq, wqa, wqb, gkv, wk, wv, cos_s, sin_s, cos_t, sin_t)


def _finish(head_acc, n_heads, o_ref):
    accs = [head_acc(h) for h in range(n_heads)]
    outs = [a[0:HEAD, :] / a[HEAD:HEAD + 1, :] for a in accs]
    pad = OUT_W - n_heads * HEAD
    if pad:
        outs.append(jnp.zeros((pad, outs[0].shape[1]), F32))
    o_ref[0] = jnp.concatenate(outs, axis=0).T.astype(BF16)


def _mla_kernel(qt_ref, k_ref, vt_ref, o_ref, m_ref, acc_ref, s_ref):
    qi = pl.program_id(1)
    tq, tk = MLA_TQ, MLA_TK
    m_ref[...] = jnp.full(m_ref.shape, NEG, F32)
    acc_ref[...] = jnp.zeros(acc_ref.shape, F32)
    n_full = (qi * tq) // tk
    n_all = ((qi + 1) * tq + tk - 1) // tk

    def qk(h, j):
        k0 = pl.multiple_of(j * tk, tk)
        part = tk // MLA_QK_PARTS
        return jnp.concatenate(
            [_dot(k_ref[0, h, pl.ds(k0 + i * part, part), :], qt_ref[0, h]) for i in range(MLA_QK_PARTS)],
            axis=0)

    s_ref[...] = qk(0, 0)

    def step(js, masked):
        items = [(j, h) for j in js for h in range(MLA_HEADS)]
        n = len(items)

        def scores(i, _):
            if i == 0:
                return s_ref[...]
            if i == n:
                return qk(0, jnp.minimum(js[-1] + 1, n_all - 1))
            j, h = items[i]
            return qk(h, j)

        def running_max(i, s):
            if i == n:
                s_ref[...] = s
                return None
            j, h = items[i]
            if masked[i // MLA_HEADS]:
                kpos = j * tk + lax.broadcasted_iota(jnp.int32, (tk, tq), 0)
                qpos = qi * tq + lax.broadcasted_iota(jnp.int32, (tk, tq), 1)
                s = jnp.where(kpos <= qpos, s, NEG)
            m_prev = m_ref[h]
            m_new = jnp.maximum(m_prev, jnp.max(s, axis=0, keepdims=True))
            m_ref[h] = m_new
            return s, m_prev, m_new

        def weights_pv(i, carry):
            if i == n:
                return
            j, h = items[i]
            s, m_prev, m_new = carry
            p = jnp.exp2(s - m_new)
            k0 = pl.multiple_of(j * tk, tk)
            acc_ref[h] = (jnp.exp2(m_prev - m_new) * acc_ref[h]
                          + _dot(vt_ref[0, h, :, pl.ds(k0, tk)], p.astype(BF16)))

        _pipeline(n + 1, [scores, running_max, weights_pv])

    _for_blocks(jnp.maximum(n_full - 1, 0), MLA_UNROLL, lambda js: step(js, [False] * len(js)))

    @pl.when(n_full >= 1)
    def _():
        step([n_full - 1, n_full], [False, True])

    @pl.when(n_full == 0)
    def _():
        step([0], [True])

    _finish(lambda h: acc_ref[h], MLA_HEADS, o_ref)


def _mla_call(qt, k, vt):
    B, H, _, S = qt.shape
    res = lambda shape: pl.BlockSpec(shape, lambda b, i: (b, 0, 0, 0))
    return pl.pallas_call(
        _mla_kernel,
        grid=(B, S // MLA_TQ),
        in_specs=[pl.BlockSpec((1, H, LANES, MLA_TQ), lambda b, i: (b, 0, 0, i)),
                  res((1, H, S, LANES)), res((1, H, VROWS, S))],
        out_specs=pl.BlockSpec((1, MLA_TQ, OUT_W), lambda b, i: (b, i, 0)),
        out_shape=jax.ShapeDtypeStruct((B, S, OUT_W), BF16),
        scratch_shapes=[pltpu.VMEM((H, 1, MLA_TQ), F32), pltpu.VMEM((H, VROWS, MLA_TQ), F32),
                        pltpu.VMEM((MLA_TK, MLA_TQ), F32)],
        compiler_params=_params("parallel", "arbitrary"),
        name="mla",
    )(qt, k, vt)


def _sb_kernel(qt_ref, k_ref, vt_ref, u_ref, o_ref, c_ref, acc_ref, z_ref):
    qi = pl.program_id(1)
    t = SB_T
    c_ref[...] = jnp.zeros(c_ref.shape, F32)
    acc_ref[...] = jnp.zeros(acc_ref.shape, F32)

    def qk(h, j):
        k0 = pl.multiple_of(j * t, t)
        half = t // 2
        return jnp.concatenate(
            [_dot(k_ref[0, h, pl.ds(k0 + s * half, half), :], qt_ref[0, h]) for s in range(2)],
            axis=0)

    strict = (lax.broadcasted_iota(jnp.int32, (t, t), 0) < lax.broadcasted_iota(jnp.int32, (t, t), 1))

    def suffix_sums(z2, masked):
        neg_abs = pltpu.bitcast(pltpu.bitcast(z2, jnp.int32) | INT_MIN, F32)
        nl = jnp.maximum(z2, 0.0) + jnp.log2(1.0 + jnp.exp2(neg_abs))
        if masked:
            nl = jnp.where(strict, nl, 0.0)
        nlb = nl.astype(BF16)
        sums = jnp.concatenate([_dot(u_ref[0:t // 2, :], nlb), _dot(u_ref[t // 2:, :], nlb)], axis=0)
        return z2, sums

    z_ref[...] = qk(0, qi)

    def step(js, masked):
        items = [(j, h) for j in js for h in range(SB_HEADS)]
        n = len(items)

        def logits(i, _):
            if i == 0:
                return z_ref[...]
            if i == n:
                return qk(0, jnp.maximum(js[-1] - 1, 0))
            j, h = items[i]
            return qk(h, j)

        def sums_stage(i, z):
            if i == n:
                z_ref[...] = z
                return None
            return suffix_sums(z, masked[i // SB_HEADS])

        def weights_pv(i, carry):
            if i == n:
                return
            j, h = items[i]
            z2, sums = carry
            w = jnp.exp2(z2 - sums[0:t])
            if masked[i // SB_HEADS]:
                w = jnp.where(strict, w, 0.0)
            k0 = pl.multiple_of(j * t, t)
            acc_ref[h] += jnp.exp2(-c_ref[h]) * _dot(vt_ref[0, h, :, pl.ds(k0, t)], w.astype(BF16))
            c_ref[h] += sums[t:t + 1]

        _pipeline(n + 1, [logits, sums_stage, weights_pv])

    @pl.when(qi >= 1)
    def _():
        step([qi, qi - 1], [True, False])

    @pl.when(qi == 0)
    def _():
        step([0], [True])

    _for_blocks(jnp.maximum(qi - 1, 0), SB_UNROLL,
                lambda idx: step([qi - 2 - i for i in idx], [False] * len(idx)))

    outs = [acc_ref[h] for h in range(SB_HEADS)]
    outs.append(jnp.zeros((OUT_W - SB_HEADS * HEAD, t), F32))
    o_ref[0] = jnp.concatenate(outs, axis=0).T.astype(BF16)


def _sb_call(qt, k, vt, u):
    B, H, _, S = qt.shape
    res = lambda shape: pl.BlockSpec(shape, lambda b, i: (b, 0, 0, 0))
    return pl.pallas_call(
        _sb_kernel,
        grid=(B, S // SB_T),
        in_specs=[pl.BlockSpec((1, H, HEAD, SB_T), lambda b, i: (b, 0, 0, i)),
                  res((1, H, S, HEAD)), res((1, H, HEAD, S)), _const_spec(u.shape)],
        out_specs=pl.BlockSpec((1, SB_T, OUT_W), lambda b, i: (b, i, 0)),
        out_shape=jax.ShapeDtypeStruct((B, S, OUT_W), BF16),
        scratch_shapes=[pltpu.VMEM((H, 1, SB_T), F32), pltpu.VMEM((H, HEAD, SB_T), F32),
                        pltpu.VMEM((SB_T, SB_T), F32)],
        compiler_params=_params("parallel", "arbitrary"),
        name="stickbreak",
    )(qt, k, vt, u)


def _bias_kernel(rb_ref, o_ref):
    v = pl.program_id(0)
    shape = (DSA_TK, DSA_TQ)
    off = jnp.where(v < DSA_SUB, v * DSA_TQ, DSA_TK)
    dist = off + lax.broadcasted_iota(jnp.int32, shape, 1) - lax.broadcasted_iota(jnp.int32, shape, 0)
    max_exact = REL_BUCKETS // 2
    d = jnp.maximum(dist, 1).astype(F32)
    large = max_exact + (jnp.log(d / max_exact) / math.log(REL_MAX_DIST / max_exact)
                         * (REL_BUCKETS - max_exact)).astype(jnp.int32)
    large = jnp.minimum(large, REL_BUCKETS - 1)
    bucket = jnp.where(dist < max_exact, dist, large)
    for h in range(DSA_HEADS):
        t = jnp.zeros(shape, F32)
        for b in range(REL_BUCKETS):
            t = jnp.where(bucket == b, (rb_ref[b, h] - rb_ref[REL_BUCKETS - 1, h]) * LOG2E, t)
        o_ref[0, h] = jnp.where(dist >= 0, t, NEG)


def _bias_call(rel_bias):
    nv = DSA_SUB + 1
    return pl.pallas_call(
        _bias_kernel,
        grid=(nv,),
        in_specs=[pl.BlockSpec(memory_space=pltpu.SMEM)],
        out_specs=pl.BlockSpec((1, DSA_HEADS, DSA_TK, DSA_TQ), lambda v: (v, 0, 0, 0)),
        out_shape=jax.ShapeDtypeStruct((nv, DSA_HEADS, DSA_TK, DSA_TQ), F32),
        compiler_params=_params("arbitrary"),
        name="bias_tiles",
    )(rel_bias)


def _dsa_kernel(iqt_ref, ik_ref, iwt_ref, dqt_ref, dk_ref, dvt_ref, tab_ref, o_ref,
                keys_ref, sorted_ref, m_ref, acc_ref, x_ref, *, topk):
    qi = pl.program_id(1)
    n_blk = qi // DSA_SUB + 1
    sub = qi % DSA_SUB
    tq, tk = DSA_TQ, DSA_TK

    iqt = jnp.concatenate([iqt_ref[0, h] for h in range(IDX_HEADS)], axis=1)
    iw = iwt_ref[0]

    def pair_scores(k_ref_, qt_all, g, j, n_heads):
        k0 = pl.multiple_of(j * tk, tk)
        cols = qt_all[:, 2 * g * tq:min(2 * (g + 1), n_heads) * tq]
        half = tk // 2
        return jnp.concatenate(
            [_dot(k_ref_[0, pl.ds(k0 + i * half, half), :], cols) for i in range(2)], axis=0)

    n_slab = tk // SLAB
    n_group = n_slab // SORT_GROUP
    zero_shift = 23 - (keys_ref.shape[0] * tk - 1).bit_length()
    assert zero_shift >= 0
    zero_key = -1 - lax.shift_left(lax.broadcasted_iota(jnp.int32, (tk, tq), 0), zero_shift)

    def sort_groups(key):
        slabs = [key[SLAB * i:SLAB * (i + 1)] for i in range(n_slab)]
        for g in range(n_group):
            v = slabs[SORT_GROUP * g:SORT_GROUP * (g + 1)]
            for a, b in SORT_NETWORK:
                v[a], v[b] = jnp.maximum(v[a], v[b]), jnp.minimum(v[a], v[b])
            slabs[SORT_GROUP * g:SORT_GROUP * (g + 1)] = v
        return jnp.concatenate(slabs, axis=0)

    x_ref[...] = pair_scores(ik_ref, iqt, 0, 0, IDX_HEADS)

    def idx_step(js, masked):
        n_pairs = IDX_HEADS // 2
        items = [(j, g) for j in js for g in range(n_pairs)]
        n = len(items)
        total = {}

        def scores(i, _):
            if i == 0:
                return x_ref[...]
            if i == n:
                return pair_scores(ik_ref, iqt, 0, jnp.minimum(js[-1] + 1, n_blk - 1), IDX_HEADS)
            j, g = items[i]
            return pair_scores(ik_ref, iqt, g, j, IDX_HEADS)

        def weigh(i, x):
            if i == n:
                x_ref[...] = x
                return
            j, g = items[i]
            part = (jnp.maximum(x[:, 0:tq], 0.0) * iw[2 * g:2 * g + 1]
                    + jnp.maximum(x[:, tq:2 * tq], 0.0) * iw[2 * g + 1:2 * g + 2])
            blk = i // n_pairs
            total[blk] = part if g == 0 else total[blk] + part
            if g < n_pairs - 1:
                return
            bits = pltpu.bitcast(total.pop(blk), jnp.int32)
            key = bits ^ (lax.shift_right_arithmetic(bits, 31) & 0x7FFFFFFF)
            key = jnp.where((bits & 0x7F800000) == 0, zero_key - lax.shift_left(j * tk, zero_shift), key)
            if masked:
                kpos = j * tk + lax.broadcasted_iota(jnp.int32, key.shape, 0)
                qpos = qi * tq + lax.broadcasted_iota(jnp.int32, key.shape, 1)
                key = jnp.where(kpos <= qpos, key, NEG_INF_KEY)
            keys_ref[j] = key
            sorted_ref[j] = sort_groups(key)

        _pipeline(n + 1, [scores, weigh])

    _for_blocks(n_blk - 1, DSA_UNROLL, lambda js: idx_step(js, False))
    idx_step([n_blk - 1], True)

    def over_blocks(per_block, init):
        def pair(i, acc):
            return per_block(2 * i + 1, per_block(2 * i, acc))

        def last(_, acc):
            return per_block(n_blk - 1, acc)

        return lax.fori_loop(0, n_blk % 2, last, lax.fori_loop(0, n_blk // 2, pair, init))

    def count_ge(cand):
        def per_block(j, acc):
            c = jnp.where(keys_ref[j] >= cand, 1.0, 0.0)
            return acc + jnp.sum(c.reshape(COUNT_ACC, tk // COUNT_ACC, tq), axis=0)
        acc = over_blocks(per_block, jnp.zeros((tk // COUNT_ACC, tq), F32))
        return jnp.sum(acc, axis=0, keepdims=True)

    def count_ge_sorted(cand):
        def one(flag):
            return jnp.where(flag, 1.0, 0.0)

        def per_block(j, accs):
            accs = list(accs)
            blk = sorted_ref[j]
            for g in range(n_group):
                s = [blk[SLAB * (SORT_GROUP * g + k):SLAB * (SORT_GROUP * g + k + 1)]
                     for k in range(SORT_GROUP)]
                b4 = s[3] >= cand
                b2 = jnp.where(b4, s[5], s[1]) >= cand
                b1 = jnp.where(b4, jnp.where(b2, s[6], s[4]), jnp.where(b2, s[2], s[0])) >= cand
                o = 3 * (g % 2)
                accs[o] += one(b4)
                accs[o + 1] += one(b2)
                accs[o + 2] += one(b1)
            for r in range(SORT_GROUP * n_group, n_slab):
                accs[2] += one(blk[SLAB * r:SLAB * (r + 1)] >= cand)
            return tuple(accs)

        z = jnp.zeros((SLAB, tq), F32)
        a = over_blocks(per_block, (z,) * 6)
        total = 4.0 * (a[0] + a[3]) + 2.0 * (a[1] + a[4]) + (a[2] + a[5])
        return jnp.sum(total, axis=0, keepdims=True)

    def bit_body(i, state):
        t_u, done, thr_hit = state
        cand_u = t_u | lax.shift_left(jnp.int32(1), 31 - i)
        cand = cand_u ^ INT_MIN
        cnt = count_ge_sorted(cand)
        hit = cnt == topk
        thr_hit = jnp.where(hit, jnp.where(done > 0.0, thr_hit, cand), thr_hit)
        done = jnp.where(hit, 1.0, done)
        return jnp.where(cnt >= topk, cand_u, t_u), done, thr_hit

    zero = jnp.zeros((1, tq), jnp.int32)
    state = (zero, jnp.zeros((1, tq), F32), zero)
    first = 0
    unresolved = None
    for last in DESCENT_CHECKS + (32,):
        if first:
            unresolved = jnp.min(state[1]) < 1.0
        stop = last if first == 0 else jnp.where(unresolved, last, first)
        state = lax.fori_loop(first, stop, bit_body, state)
        first = last
    t_u, done, thr_hit = state
    thr = jnp.where(done > 0.0, thr_hit, t_u ^ INT_MIN)

    tie = jnp.where(done > 0.0, 0.0, jnp.where(thr > NEG_INF_KEY, 1.0, 0.0))

    def fix_ties():
        need = jnp.where(tie > 0.0, topk - count_ge(thr + 1), float(tk * keys_ref.shape[0] + 1))
        row = lax.broadcasted_iota(jnp.int32, (tk, tk), 0)
        col = lax.broadcasted_iota(jnp.int32, (tk, tk), 1)
        before = jnp.where(col < row, 1.0, 0.0).astype(BF16)

        def body(j, run):
            kb = keys_ref[j]
            eq = jnp.where(kb == thr, 1.0, 0.0)
            rank = _dot(before, eq.astype(BF16)) + run
            keys_ref[j] = jnp.where(kb == thr, jnp.where(rank >= need, thr - 1, kb), kb)
            return run + jnp.sum(eq, axis=0, keepdims=True)

        lax.fori_loop(0, n_blk, body, jnp.zeros((1, tq), F32))

    @pl.when(unresolved)
    def _():
        pl.when(jnp.max(tie) > 0.0)(fix_ties)

    dqt = jnp.concatenate([dqt_ref[0, h] for h in range(DSA_HEADS)], axis=1)
    m_ref[...] = jnp.full(m_ref.shape, NEG, F32)
    acc_ref[...] = jnp.zeros(acc_ref.shape, F32)

    x_ref[...] = pair_scores(dk_ref, dqt, 0, 0, DSA_HEADS)

    def att_step(js, variants):
        n_pairs = (DSA_HEADS + 1) // 2
        items = [(j, g) for j in js for g in range(n_pairs)]
        n = len(items)

        def scores(i, _):
            if i == 0:
                return x_ref[...]
            if i == n:
                return pair_scores(dk_ref, dqt, 0, jnp.minimum(js[-1] + 1, n_blk - 1), DSA_HEADS)
            j, g = items[i]
            return pair_scores(dk_ref, dqt, g, j, DSA_HEADS)

        def running_max(i, s):
            if i == n:
                x_ref[...] = s
                return None
            j, g = items[i]
            variant = variants[i // n_pairs]
            sel = keys_ref[j] >= thr
            out = []
            for h in range(2 * g, min(2 * (g + 1), DSA_HEADS)):
                a = s[:, (h - 2 * g) * tq:(h - 2 * g + 1) * tq]
                if variant is not None:
                    a = a + tab_ref[variant, h]
                a = jnp.where(sel, a, NEG)
                m_prev = m_ref[h]
                m_new = jnp.maximum(m_prev, jnp.max(a, axis=0, keepdims=True))
                m_ref[h] = m_new
                out.append((h, a, m_prev, m_new))
            return out

        def weights_pv(i, carry):
            if i == n:
                return
            j, _ = items[i]
            vt = dvt_ref[0, :, pl.ds(pl.multiple_of(j * tk, tk), tk)]
            for h, a, m_prev, m_new in carry:
                p = jnp.exp2(a - m_new)
                acc_ref[h] = jnp.exp2(m_prev - m_new) * acc_ref[h] + _dot(vt, p.astype(BF16))

        _pipeline(n + 1, [scores, running_max, weights_pv])

    _for_blocks(jnp.maximum(n_blk - 2, 0), DSA_UNROLL, lambda js: att_step(js, [None] * len(js)))

    @pl.when(jnp.logical_and(n_blk >= 2, sub == 0))
    def _():
        att_step([n_blk - 2, n_blk - 1], [DSA_SUB, 0])

    @pl.when(jnp.logical_and(n_blk >= 2, sub != 0))
    def _():
        att_step([n_blk - 2, n_blk - 1], [None, sub])

    @pl.when(n_blk == 1)
    def _():
        att_step([0], [sub])
    _finish(lambda h: acc_ref[h], DSA_HEADS, o_ref)


def _dsa_call(iqt, ik, iwt, dqt, dk, dvt, tab, topk):
    B, _, _, S = iqt.shape
    tq = DSA_TQ
    qh = lambda n: pl.BlockSpec((1, n, HEAD, tq), lambda b, i: (b, 0, 0, i))
    res = lambda shape: pl.BlockSpec(shape, lambda b, i: (b, 0, 0))
    return pl.pallas_call(
        functools.partial(_dsa_kernel, topk=topk),
        grid=(B, S // tq),
        in_specs=[qh(IDX_HEADS), res((1, S, HEAD)),
                  pl.BlockSpec((1, IDX_HEADS, tq), lambda b, i: (b, 0, i)),
                  qh(DSA_HEADS), res((1, S, HEAD)), res((1, VROWS, S)), _const_spec(tab.shape)],
        out_specs=pl.BlockSpec((1, tq, OUT_W), lambda b, i: (b, i, 0)),
        out_shape=jax.ShapeDtypeStruct((B, S, OUT_W), BF16),
        scratch_shapes=[pltpu.VMEM((S // DSA_TK, DSA_TK, tq), jnp.int32),
                        pltpu.VMEM((S // DSA_TK, DSA_TK, tq), jnp.int32),
                        pltpu.VMEM((DSA_HEADS, 1, tq), F32),
                        pltpu.VMEM((DSA_HEADS, VROWS, tq), F32),
                        pltpu.VMEM((DSA_TK, 2 * tq), F32)],
        compiler_params=_params("parallel", "arbitrary"),
        name="dsa",
    )(iqt, ik, iwt, dqt, dk, dvt, tab)


def _post_kernel(h_ref, om_ref, od_ref, os_ref, p_ref, wo_ref, gf_ref, wg_ref, wu_ref, wd_ref,
                 gp_ref, wpg_ref, wpp_ref, gfin_ref, out_ref, *, final):
    h = (h_ref[0] + _dot(om_ref[0], wo_ref[0]) + _dot(od_ref[0], wo_ref[1])
         + _dot(os_ref[0], wo_ref[2]))

    hf = _rms(h, gf_ref[...]).astype(BF16)
    d_ff = wg_ref.shape[1]
    ffn = jnp.zeros(h.shape, F32)
    for c0 in range(0, d_ff, FF_CHUNK):
        g = _dot(hf, wg_ref[:, c0:c0 + FF_CHUNK])
        u = _dot(hf, wu_ref[:, c0:c0 + FF_CHUNK])
        ffn = ffn + _dot((g * jax.nn.sigmoid(g) * u).astype(BF16), wd_ref[c0:c0 + FF_CHUNK, :])
    h = h + ffn

    gate = jax.nn.sigmoid(_dot(_rms(h, gp_ref[...]).astype(BF16), wpg_ref[...]))
    h = h + gate * _dot(p_ref[0, 0].astype(BF16), wpp_ref[...])
    if final:
        h = _rms(h, gfin_ref[...])
    out_ref[0] = h


def _post_call(h, om, od, os_, p, layer, wo, gf, wg, wu, wd, gp, wpg, wpp, gfin, final):
    B, S, D = h.shape
    mix = pl.BlockSpec((1, TM, OUT_W), lambda b, t: (b, t, 0))
    tok = pl.BlockSpec((1, TM, D), lambda b, t: (b, t, 0))
    return pl.pallas_call(
        functools.partial(_post_kernel, final=final),
        grid=(B, S // TM),
        in_specs=[tok, mix, mix, mix,
                  pl.BlockSpec((1, 1, TM, p.shape[-1]), lambda b, t: (layer, b, t, 0)),
                  _const_spec(wo.shape), _const_spec(gf.shape), _const_spec(wg.shape),
                  _const_spec(wu.shape), _const_spec(wd.shape), _const_spec(gp.shape),
                  _const_spec(wpg.shape), _const_spec(wpp.shape), _const_spec(gfin.shape)],
        out_specs=tok,
        out_shape=jax.ShapeDtypeStruct((B, S, D), F32),
        compiler_params=_params("parallel", "parallel"),
        name="post",
    )(h, om, od, os_, p, wo, gf, wg, wu, wd, gp, wpg, wpp, gfin)


def _swap_halves(w):
    half = w.shape[-1] // 2
    return jnp.concatenate([w[..., half:], w[..., :half]], axis=-1)


def _pack_w_in(w_in):
    L, D, _ = w_in.shape
    z = lambda n: jnp.zeros((L, D, n), F32)
    o = 0
    cols = {}
    for name, n in (("cq", MLA_Q_LORA), ("ckv", MLA_KV_LORA), ("kr", MLA_ROPE),
                    ("dq", DSA_HEADS * DSA_DIM), ("dk", DSA_DIM), ("dv", DSA_DIM),
                    ("iq", IDX_HEADS * IDX_DIM), ("ik", IDX_DIM), ("iw", IDX_HEADS),
                    ("sq", SB_HEADS * SB_DIM), ("sk", SB_HEADS * SB_DIM), ("sv", SB_HEADS * SB_DIM)):
        cols[name] = w_in[:, :, o:o + n]
        o += n
    pad_r = LANES - MLA_NOPE - MLA_ROPE
    std = jnp.concatenate([
        cols["cq"], cols["ckv"],
        z(MLA_NOPE), cols["kr"], z(pad_r),
        z(MLA_NOPE), _swap_halves(cols["kr"]), z(pad_r),
        cols["sk"], cols["ik"],
        cols["dk"], z(LANES - DSA_DIM)], axis=-1)
    tr = jnp.concatenate([
        cols["dq"] * DSA_DIM ** -0.5, cols["iq"] * IDX_DIM ** -0.5, cols["sq"] * SB_DIM ** -0.5,
        cols["sv"], cols["dv"], cols["iw"], z(ROW_IW[1] - ROW_IW[0] - IDX_HEADS)], axis=-1)
    assert std.shape[-1] == W_STD_COLS and tr.shape[-1] == W_T_ROWS
    return std.astype(BF16), jnp.swapaxes(tr, 1, 2).astype(BF16)


def _pack_mla(w_uq, w_ukv):
    L = w_uq.shape[0]
    dq = MLA_NOPE + MLA_ROPE
    pad_r = LANES - dq
    uq = w_uq.reshape(L, MLA_Q_LORA, MLA_HEADS, dq)
    zq = lambda n: jnp.zeros((L, MLA_Q_LORA, MLA_HEADS, n), F32)
    wqa = jnp.concatenate([uq, zq(pad_r)], axis=-1)
    wqb = jnp.concatenate([zq(MLA_NOPE), _swap_halves(uq[..., MLA_NOPE:]), zq(pad_r)], axis=-1)
    ukv = w_ukv.reshape(L, MLA_KV_LORA, MLA_HEADS, MLA_NOPE + MLA_V)
    wk = jnp.concatenate([ukv[..., :MLA_NOPE],
                          jnp.zeros((L, MLA_KV_LORA, MLA_HEADS, LANES - MLA_NOPE), F32)], axis=-1)
    wv = ukv[..., MLA_NOPE:]
    flat = lambda a: a.reshape(L, a.shape[1], -1)
    tr = lambda a: jnp.swapaxes(flat(a), 1, 2).astype(BF16)
    return tr(wqa), tr(wqb), flat(wk).astype(BF16), tr(wv)


def _rope_tables(S):
    half = MLA_ROPE // 2
    inv = ROPE_THETA ** (-jnp.arange(half, dtype=F32) / half)
    ang = jnp.arange(S, dtype=jnp.int32).astype(F32)[:, None] * inv[None, :]
    cos, sin = jnp.cos(ang), jnp.sin(ang)
    pad_r = LANES - MLA_NOPE - MLA_ROPE
    cos_t = jnp.concatenate([jnp.ones((S, MLA_NOPE), F32), cos, cos, jnp.zeros((S, pad_r), F32)], axis=-1)
    sin_t = jnp.concatenate([jnp.zeros((S, MLA_NOPE), F32), -sin, sin, jnp.zeros((S, pad_r), F32)], axis=-1)
    return cos_t, sin_t


def _pack_w_o(w_o):
    L, _, D = w_o.shape
    a = MLA_HEADS * MLA_V
    b = a + DSA_HEADS * DSA_DIM
    pad = jnp.zeros((L, OUT_W - DSA_HEADS * DSA_DIM, D), F32)
    blocks = [w_o[:, :a], jnp.concatenate([w_o[:, a:b], pad], axis=1),
              jnp.concatenate([w_o[:, b:], pad], axis=1)]
    return jnp.stack(blocks, axis=1).astype(BF16)


def kernel(x, p, w_in, attn_norm, mla_q_norm, mla_w_uq, mla_kv_norm, mla_w_ukv, rel_bias, w_o,
           ffn_norm, w_gate, w_up, w_down, ple_norm, w_ple_gate, w_ple_proj, final_norm):
    B, S, D = x.shape
    depth = w_in.shape[0]
    assert S % TM == 0 and S % DSA_TK == 0 and S % MLA_TK == 0
    assert w_gate.shape[-1] % FF_CHUNK == 0
    topk = min(DSA_TOPK, S // 4)

    w_std, w_tr = _pack_w_in(w_in)
    wqa, wqb, wk, wv = _pack_mla(mla_w_uq, mla_w_ukv)
    cos_s, sin_s = _rope_tables(S)
    q_scale = (MLA_NOPE + MLA_ROPE) ** -0.5 * LOG2E
    cos_t, sin_t = cos_s.T * q_scale, sin_s.T * q_scale
    row = lambda g: g.reshape(depth, 1, -1)
    g_attn, g_q, g_kv, g_ffn, g_ple = map(row, (attn_norm, mla_q_norm, mla_kv_norm, ffn_norm, ple_norm))
    g_fin = final_norm.reshape(1, -1)
    wo = _pack_w_o(w_o)
    wg, wu, wd, wpg, wpp = (a.astype(BF16) for a in (w_gate, w_up, w_down, w_ple_gate, w_ple_proj))
    tri = jnp.triu(jnp.ones((SB_T, SB_T), F32), 0)
    tail = jnp.zeros((16, SB_T), F32).at[0].set(1.0)
    u = jnp.concatenate([tri, tail], axis=0).astype(BF16)

    tab = _bias_call(rel_bias)

    h = x
    for i in range(depth):
        (mqt, mk, mvt, dqt, dk, dvt, iqt, ik, iwt, sqt, sk, svt) = _proj_call(
            h, g_attn[i], w_std[i], w_tr[i], g_q[i], wqa[i], wqb[i], g_kv[i], wk[i], wv[i],
            cos_s, sin_s, cos_t, sin_t)
        o_mla = _mla_call(mqt, mk, mvt)
        o_dsa = _dsa_call(iqt, ik, iwt, dqt, dk, dvt, tab, topk)
        o_sb = _sb_call(sqt, sk, svt, u)
        h = _post_call(h, o_mla, o_dsa, o_sb, p, i, wo[i], g_ffn[i], wg[i], wu[i], wd[i],
                       g_ple[i], wpg[i], wpp[i], g_fin, final=(i == depth - 1))
    return h
```
